```python
import jax, jax.numpy as jnp
from jax import lax
import numpy as np

D_MODEL = 1024
BATCH = 4
SEQ = 8192
DEPTH = 1
DEC_BATCH = 16
DEC_SEQ = 32
PAST_LEN = 1024

CHUNK = 64
D_POOL = 512
N_POOL_GROUPS = 4
POOL_GROUP = D_POOL // N_POOL_GROUPS
POOL_WINDOWS = (2, 4, 8, 16)
POOL_HIST = max(POOL_WINDOWS) - 1
D_CONV = 512
CONV_WIDTH = 31
CONV_HIST = CONV_WIDTH - 1
N_EXPERT_GROUPS = 4
EXPERTS_PER_GROUP = 8
TOP_K_INNER = 2
D_EXPERT = 256
IN_COLS = D_POOL + 2 * D_CONV + 2 * D_MODEL
RMS_EPS = 1e-6
LN_EPS = 1e-5

kernel_name = "gated_pool_conformer_hiermoe_stream"


def rmsnorm(x, g):
    xf = x.astype(jnp.float32)
    y = xf * lax.rsqrt(jnp.mean(xf * xf, axis=-1, keepdims=True) + RMS_EPS)
    return (y * g.astype(jnp.float32)).astype(x.dtype)


def pool_mixer(u, hist, past, w_grp, scale):
    B, L, _ = u.shape
    seq = jnp.concatenate([hist.astype(u.dtype), u], axis=1).astype(jnp.float32)
    cs = jnp.pad(jnp.cumsum(seq, axis=1), ((0, 0), (1, 0), (0, 0)))
    t = jnp.arange(L, dtype=jnp.float32)
    means = []
    for gi, w in enumerate(POOL_WINDOWS):
        c0, c1 = gi * POOL_GROUP, (gi + 1) * POOL_GROUP
        end = cs[:, POOL_HIST + 1:POOL_HIST + 1 + L, c0:c1]
        start = cs[:, POOL_HIST + 1 - w:POOL_HIST + 1 - w + L, c0:c1]
        cnt = jnp.minimum(jnp.float32(w), past + t + 1.0)
        means.append((end - start) / cnt[None, :, None])
    mean = jnp.stack(means, axis=2)
    diff = mean - u.astype(jnp.float32).reshape(B, L, N_POOL_GROUPS, POOL_GROUP)
    y = jnp.einsum('blgc,gcd->blgd', diff.astype(u.dtype), w_grp).reshape(B, L, D_POOL)
    return y * scale


def conv_module(v, hist, w_dw, b_dw, ln_g, ln_b):
    seq = jnp.concatenate([hist.astype(v.dtype), v], axis=1)
    y = lax.conv_general_dilated(seq, w_dw[:, None, :], window_strides=(1,), padding='VALID',
                                 dimension_numbers=('NWC', 'WIO', 'NWC'),
                                 feature_group_count=D_CONV)
    yf = (y + b_dw).astype(jnp.float32)
    mu = jnp.mean(yf, axis=-1, keepdims=True)
    var = jnp.mean(jnp.square(yf - mu), axis=-1, keepdims=True)
    yn = (yf - mu) * lax.rsqrt(var + LN_EPS) * ln_g.astype(jnp.float32) + ln_b.astype(jnp.float32)
    return jax.nn.silu(yn).astype(v.dtype)


def hier_moe(h, w_rg, b_rg, w_re, b_re, w_gate, w_up, w_down):
    B, L, _ = h.shape
    hf = h.reshape(B * L, D_MODEL)
    lg = (hf @ w_rg).astype(jnp.float32) + b_rg.astype(jnp.float32)
    pg = jax.nn.softmax(lg, axis=-1)
    g_top = jnp.argmax(lg, axis=-1)
    p_sel = jnp.take_along_axis(pg, g_top[:, None], axis=-1)[:, 0]
    le = ((hf @ w_re).astype(jnp.float32) + b_re.astype(jnp.float32)).reshape(-1, N_EXPERT_GROUPS, EXPERTS_PER_GROUP)
    le_sel = jnp.take_along_axis(le, g_top[:, None, None], axis=1)[:, 0]
    top_v, top_i = lax.top_k(le_sel, TOP_K_INNER)
    w2 = jax.nn.softmax(top_v, axis=-1) * p_sel[:, None]
    combine = jnp.sum(jax.nn.one_hot(top_i, EXPERTS_PER_GROUP, dtype=jnp.float32) * w2[..., None], axis=1)
    y = jnp.zeros((B * L, D_MODEL), jnp.float32)
    for g in range(N_EXPERT_GROUPS):
        cg = combine * (g_top == g).astype(jnp.float32)[:, None]
        a = jnp.einsum('td,edf->tef', hf, w_gate[g])
        b = jnp.einsum('td,edf->tef', hf, w_up[g])
        act = (jax.nn.silu(a) * b) * cg[..., None]
        y = y + jnp.einsum('tef,efd->td', act, w_down[g]).astype(jnp.float32)
    return y.astype(h.dtype).reshape(B, L, D_MODEL)


def layer(x, pool_hist, conv_hist, past, p):
    h = rmsnorm(x, p['g_mix'])
    z = h @ p['w_in']
    o = 0
    u_pool = z[..., o:o + D_POOL]; o += D_POOL
    ga = z[..., o:o + D_CONV]; o += D_CONV
    gb = z[..., o:o + D_CONV]; o += D_CONV
    gate_pool = z[..., o:o + D_MODEL]; o += D_MODEL
    gate_conv = z[..., o:o + D_MODEL]
    v = ga * jax.nn.sigmoid(gb)
    y_pool = pool_mixer(u_pool, pool_hist, past, p['w_pool_grp'], p['pool_scale']) @ p['w_pool_out']
    y_conv = conv_module(v, conv_hist, p['w_dw'], p['b_dw'], p['ln_g'], p['ln_b']) @ p['w_conv_out']
    merged = jax.nn.sigmoid(gate_pool) * y_pool + jax.nn.sigmoid(gate_conv) * y_conv
    x = x + merged @ p['w_out']
    x = x + hier_moe(rmsnorm(x, p['g_ffn']), p['w_rg'], p['b_rg'], p['w_re'], p['b_re'],
                     p['w_gate'], p['w_up'], p['w_down'])
    new_pool = jnp.concatenate([pool_hist.astype(u_pool.dtype), u_pool], axis=1)[:, -POOL_HIST:]
    new_conv = jnp.concatenate([conv_hist.astype(v.dtype), v], axis=1)[:, -CONV_HIST:]
    return x, new_pool, new_conv


def setup_inputs(seed: int = 0) -> dict:
    key = jax.random.key(seed)
    ks = jax.random.split(key, 24)
    n = lambda k, s, sc: jax.random.normal(k, s, jnp.float32) * sc
    G, E, F = N_EXPERT_GROUPS, EXPERTS_PER_GROUP, D_EXPERT
    return {
        'x_prompt': n(ks[0], (BATCH, SEQ, D_MODEL), 1.0),
        'x_sample': n(ks[1], (DEC_BATCH, DEC_SEQ, D_MODEL), 1.0),
        'state_pool': n(ks[2], (DEPTH, DEC_BATCH, POOL_HIST, D_POOL), 1.0),
        'state_conv': n(ks[3], (DEPTH, DEC_BATCH, CONV_HIST, D_CONV), 0.5),
        'g_mix': 1.0 + n(ks[4], (DEPTH, D_MODEL), 0.05),
        'w_in': n(ks[5], (DEPTH, D_MODEL, IN_COLS), D_MODEL ** -0.5),
        'w_pool_grp': n(ks[6], (DEPTH, N_POOL_GROUPS, POOL_GROUP, POOL_GROUP), POOL_GROUP ** -0.5),
        'pool_scale': 1.0 + n(ks[7], (DEPTH, D_POOL), 0.1),
        'w_pool_out': n(ks[8], (DEPTH, D_POOL, D_MODEL), D_POOL ** -0.5),
        'w_dw': n(ks[9], (DEPTH, CONV_WIDTH, D_CONV), CONV_WIDTH ** -0.5),
        'b_dw': n(ks[10], (DEPTH, D_CONV), 0.02),
        'ln_g': 1.0 + n(ks[11], (DEPTH, D_CONV), 0.05),
        'ln_b': n(ks[12], (DEPTH, D_CONV), 0.02),
        'w_conv_out': n(ks[13], (DEPTH, D_CONV, D_MODEL), D_CONV ** -0.5),
        'w_out': n(ks[14], (DEPTH, D_MODEL, D_MODEL), D_MODEL ** -0.5),
        'g_ffn': 1.0 + n(ks[15], (DEPTH, D_MODEL), 0.05),
        'w_rg': n(ks[16], (DEPTH, D_MODEL, G), D_MODEL ** -0.5),
        'b_rg': n(ks[17], (DEPTH, G), 0.01),
        'w_re': n(ks[18], (DEPTH, D_MODEL, G * E), D_MODEL ** -0.5),
        'b_re': n(ks[19], (DEPTH, G * E), 0.01),
        'w_gate': n(ks[20], (DEPTH, G, E, D_MODEL, F), D_MODEL ** -0.5),
        'w_up': n(ks[21], (DEPTH, G, E, D_MODEL, F), D_MODEL ** -0.5),
        'w_down': n(ks[22], (DEPTH, G, E, F, D_MODEL), F ** -0.5),
        'g_final': 1.0 + n(ks[23], (D_MODEL,), 0.05),
    }


def reference(x_prompt, x_sample, state_pool, state_conv, g_mix, w_in, w_pool_grp, pool_scale, w_pool_out,
              w_dw, b_dw, ln_g, ln_b, w_conv_out, w_out, g_ffn, w_rg, b_rg, w_re, b_re, w_gate, w_up, w_down,
              g_final):
    xp, xs = x_prompt, x_sample
    Bp = x_prompt.shape[0]
    npp, ncp, nps, ncs = [], [], [], []
    for l in range(DEPTH):
        p = {'g_mix': g_mix[l], 'w_in': w_in[l], 'w_pool_grp': w_pool_grp[l], 'pool_scale': pool_scale[l],
             'w_pool_out': w_pool_out[l], 'w_dw': w_dw[l], 'b_dw': b_dw[l], 'ln_g': ln_g[l], 'ln_b': ln_b[l],
             'w_conv_out': w_conv_out[l], 'w_out': w_out[l], 'g_ffn': g_ffn[l], 'w_rg': w_rg[l], 'b_rg': b_rg[l],
             'w_re': w_re[l], 'b_re': b_re[l], 'w_gate': w_gate[l], 'w_up': w_up[l], 'w_down': w_down[l]}
        zp = jnp.zeros((Bp, POOL_HIST, D_POOL), xp.dtype)
        zc = jnp.zeros((Bp, CONV_HIST, D_CONV), xp.dtype)
        xp, pp, cp = layer(xp, zp, zc, 0, p)
        xs, ps, cs = layer(xs, state_pool[l], state_conv[l], PAST_LEN, p)
        npp.append(pp); ncp.append(cp); nps.append(ps); ncs.append(cs)
    y_prompt = rmsnorm(xp, g_final)
    y_sample = rmsnorm(xs, g_final)
    new_pool_prompt = jnp.stack(npp, axis=0)
    new_conv_prompt = jnp.stack(ncp, axis=0)
    new_pool_sample = jnp.stack(nps, axis=0)
    new_conv_sample = jnp.stack(ncs, axis=0)
    return (y_prompt, y_sample, new_pool_prompt, new_conv_prompt, new_pool_sample, new_conv_sample)
```

```python
import functools

import jax
import jax.numpy as jnp
from jax import lax
from jax.experimental import pallas as pl
from jax.experimental.pallas import tpu as pltpu

D_MODEL = 1024
D_POOL = 512
N_POOL_GROUPS = 4
POOL_GROUP = D_POOL // N_POOL_GROUPS
POOL_WINDOWS = (2, 4, 8, 16)
POOL_HIST = max(POOL_WINDOWS) - 1
D_CONV = 512
CONV_WIDTH = 31
CONV_HIST = CONV_WIDTH - 1
N_EXPERT_GROUPS = 4
EXPERTS_PER_GROUP = 8
N_EXPERTS = N_EXPERT_GROUPS * EXPERTS_PER_GROUP
D_EXPERT = 256
RMS_EPS = 1e-6
LN_EPS = 1e-5
PAST_LEN = 1024

LANES = 128
POOL_PAD = 16
CONV_PAD = 32
ROUTER_COLS = LANES
VMEM_LIMIT = 56 * 1024 * 1024

BF16 = jnp.bfloat16
F32 = jnp.float32


def _dot(a, b):
    return jnp.dot(a, b, preferred_element_type=F32)


def _rmsnorm(x, g):
    return x * lax.rsqrt(jnp.mean(x * x, axis=-1, keepdims=True) + RMS_EPS) * g


def _routing(logits):
    lane = lax.broadcasted_iota(jnp.int32, logits.shape, 1)
    lane_f = lane.astype(F32)
    neg = jnp.float32(-jnp.inf)
    big = jnp.float32(1e9)
    is_grp = (lane >= N_EXPERTS) & (lane < N_EXPERTS + N_EXPERT_GROUPS)
    glog = jnp.where(is_grp, logits, neg)
    gmax = jnp.max(glog, axis=1, keepdims=True)
    gidx = jnp.min(jnp.where(glog == gmax, lane_f, big), axis=1, keepdims=True) - float(N_EXPERTS)
    gsum = jnp.sum(jnp.where(is_grp, jnp.exp(glog - gmax), 0.0), axis=1, keepdims=True)
    p_sel = 1.0 / gsum
    lane_grp = (lane >> 3).astype(F32)
    is_sel = (lane < N_EXPERTS) & (lane_grp == gidx)
    elog = jnp.where(is_sel, logits, neg)
    v1 = jnp.max(elog, axis=1, keepdims=True)
    i1 = jnp.min(jnp.where(elog == v1, lane_f, big), axis=1, keepdims=True)
    elog2 = jnp.where(lane_f == i1, neg, elog)
    v2 = jnp.max(elog2, axis=1, keepdims=True)
    i2 = jnp.min(jnp.where(elog2 == v2, lane_f, big), axis=1, keepdims=True)
    t = jnp.exp(v2 - v1)
    wa = 1.0 / (1.0 + t)
    wb = t / (1.0 + t)
    return jnp.where(lane_f == i1, wa * p_sel, jnp.where(lane_f == i2, wb * p_sel, 0.0))


def _mixer_kernel(past, sb, ls,
                  x_ref, hp_ref, hc_ref, gmix_ref, win_ref, wgrp_ref, pscale_ref, wpo_ref, wdw_ref, bdw_ref,
                  lng_ref, lnb_ref, wco_ref, wout_ref, gffn_ref, wrh_ref, wrl_ref, br_ref,
                  x1_ref, h2_ref, cw_ref, npool_ref, nconv_ref,
                  ubuf, vbuf):
    j = pl.program_id(1)
    rows = sb * ls

    @pl.when(j == 0)
    def _():
        ubuf[:, POOL_PAD - POOL_HIST:POOL_PAD, :] = hp_ref[...]
        vbuf[:, CONV_PAD - CONV_HIST:CONV_PAD, :] = hc_ref[...]

    x = x_ref[...].reshape(rows, D_MODEL)
    h = _rmsnorm(x, gmix_ref[...]).astype(BF16)

    u = _dot(h, win_ref[:, 0:D_POOL])
    ubuf[:, POOL_PAD:POOL_PAD + ls, :] = u.reshape(sb, ls, D_POOL)
    t_idx = lax.broadcasted_iota(jnp.int32, (sb, ls, 1), 1)
    frames = (past + 1 + j * ls + t_idx).astype(F32)
    ys = []
    for gi, w in enumerate(POOL_WINDOWS):
        c0, c1 = gi * POOL_GROUP, (gi + 1) * POOL_GROUP
        cur = ubuf[:, POOL_PAD:POOL_PAD + ls, c0:c1]
        s = cur
        for k in range(1, w):
            s = s + ubuf[:, POOL_PAD - k:POOL_PAD - k + ls, c0:c1]
        mean = s / jnp.minimum(jnp.float32(w), frames)
        diff = (mean - cur).reshape(rows, POOL_GROUP).astype(BF16)
        ys.append(_dot(diff, wgrp_ref[gi]))
    yp = (jnp.concatenate(ys, axis=1) * pscale_ref[...]).astype(BF16)
    y_pool = _dot(yp, wpo_ref[...])
    new_pool = ubuf[:, POOL_PAD + ls - POOL_HIST:POOL_PAD + ls, :]
    ubuf[:, POOL_PAD - POOL_HIST:POOL_PAD, :] = new_pool

    ga = _dot(h, win_ref[:, D_POOL:D_POOL + D_CONV])
    gb = _dot(h, win_ref[:, D_POOL + D_CONV:D_POOL + 2 * D_CONV])
    v = ga * jax.nn.sigmoid(gb)
    vbuf[:, CONV_PAD:CONV_PAD + ls, :] = v.reshape(sb, ls, D_CONV)
    acc = jnp.zeros((sb, ls, D_CONV), F32) + bdw_ref[...]
    for k in range(CONV_WIDTH):
        off = CONV_PAD - CONV_HIST + k
        acc = acc + vbuf[:, off:off + ls, :] * wdw_ref[k:k + 1, :]
    yf = acc.reshape(rows, D_CONV)
    mu = jnp.mean(yf, axis=-1, keepdims=True)
    yc = yf - mu
    var = jnp.mean(yc * yc, axis=-1, keepdims=True)
    yn = yc * lax.rsqrt(var + LN_EPS) * lng_ref[...] + lnb_ref[...]
    y_conv = _dot((yn * jax.nn.sigmoid(yn)).astype(BF16), wco_ref[...])
    new_conv = vbuf[:, CONV_PAD + ls - CONV_HIST:CONV_PAD + ls, :]
    vbuf[:, CONV_PAD - CONV_HIST:CONV_PAD, :] = new_conv

    @pl.when(j == pl.num_programs(1) - 1)
    def _():
        npool_ref[...] = new_pool
        nconv_ref[...] = new_conv

    c_gp = D_POOL + 2 * D_CONV
    gate_pool = _dot(h, win_ref[:, c_gp:c_gp + D_MODEL])
    merged = jax.nn.sigmoid(gate_pool) * y_pool
    gate_conv = _dot(h, win_ref[:, c_gp + D_MODEL:c_gp + 2 * D_MODEL])
    merged = merged + jax.nn.sigmoid(gate_conv) * y_conv
    x1 = x + _dot(merged.astype(BF16), wout_ref[...])
    x1_ref[...] = x1.reshape(sb, ls, D_MODEL)

    h2 = _rmsnorm(x1, gffn_ref[...])
    h2_hi = h2.astype(BF16)
    h2_lo = (h2 - h2_hi.astype(F32)).astype(BF16)
    h2_ref[...] = h2_hi.reshape(sb, ls, D_MODEL)
    logits = (_dot(h2_hi, wrh_ref[...]) + (_dot(h2_hi, wrl_ref[...]) + _dot(h2_lo, wrh_ref[...]))) + br_ref[...]
    cw_ref[...] = _routing(logits).reshape(sb, ls, ROUTER_COLS)


def _const_spec(shape):
    nd = len(shape)
    return pl.BlockSpec(shape, lambda b, j: (0,) * nd, pipeline_mode=pl.Buffered(1))


def _mixer(x, hist_pool, hist_conv, past, sb, ls, wts):
    nb, seq, _ = x.shape
    assert nb % sb == 0 and seq % ls == 0 and ls % 8 == 0 and ls >= CONV_HIST
    grid = (nb // sb, seq // ls)
    row_spec = lambda width: pl.BlockSpec((sb, ls, width), lambda b, j: (b, j, 0))
    hist_spec = lambda n, width: pl.BlockSpec((sb, n, width), lambda b, j: (b, 0, 0))
    in_specs = [row_spec(D_MODEL), hist_spec(POOL_HIST, D_POOL), hist_spec(CONV_HIST, D_CONV)]
    in_specs += [_const_spec(w.shape) for w in wts]
    out_shape = (
        jax.ShapeDtypeStruct((nb, seq, D_MODEL), F32),
        jax.ShapeDtypeStruct((nb, seq, D_MODEL), BF16),
        jax.ShapeDtypeStruct((nb, seq, ROUTER_COLS), F32),
        jax.ShapeDtypeStruct((nb, POOL_HIST, D_POOL), F32),
        jax.ShapeDtypeStruct((nb, CONV_HIST, D_CONV), F32),
    )
    out_specs = (row_spec(D_MODEL), row_spec(D_MODEL), row_spec(ROUTER_COLS),
                 hist_spec(POOL_HIST, D_POOL), hist_spec(CONV_HIST, D_CONV))
    return pl.pallas_call(
        functools.partial(_mixer_kernel, past, sb, ls),
        grid=grid,
        in_specs=in_specs,
        out_specs=out_specs,
        out_shape=out_shape,
        scratch_shapes=[pltpu.VMEM((sb, POOL_PAD + ls, D_POOL), F32),
                        pltpu.VMEM((sb, CONV_PAD + ls, D_CONV), F32)],
        compiler_params=pltpu.CompilerParams(dimension_semantics=("arbitrary", "arbitrary"),
                                             vmem_limit_bytes=VMEM_LIMIT),
        name="mixer",
    )(x, hist_pool, hist_conv, *wts)


def _moe_kernel(h2_ref, cw_ref, x1_ref, wg_ref, wu_ref, wd_ref, gfin_ref, y_ref, acc_ref):
    e = pl.program_id(1)

    @pl.when(e == 0)
    def _():
        acc_ref[...] = jnp.zeros_like(acc_ref)

    h = h2_ref[...]
    a = _dot(h, wg_ref[...].astype(BF16))
    b = _dot(h, wu_ref[...].astype(BF16))
    cw = cw_ref[...]
    lane = lax.broadcasted_iota(jnp.int32, cw.shape, 1)
    c = jnp.sum(jnp.where(lane == e, cw, 0.0), axis=1, keepdims=True)
    act = (a * jax.nn.sigmoid(a)) * b * c
    acc_ref[...] += _dot(act.astype(BF16), wd_ref[...].astype(BF16))

    @pl.when(e == pl.num_programs(1) - 1)
    def _():
        y_ref[...] = _rmsnorm(x1_ref[...] + acc_ref[...], gfin_ref[...])


def _moe(h2, cw, x1, w_gate, w_up, w_down, g_final, tm):
    n_tok = h2.shape[0]
    assert n_tok % tm == 0
    tok_spec = lambda width: pl.BlockSpec((tm, width), lambda i, e: (i, 0))
    return pl.pallas_call(
        _moe_kernel,
        grid=(n_tok // tm, N_EXPERTS),
        in_specs=[tok_spec(D_MODEL), tok_spec(ROUTER_COLS), tok_spec(D_MODEL),
                  pl.BlockSpec((None, D_MODEL, D_EXPERT), lambda i, e: (e, 0, 0)),
                  pl.BlockSpec((None, D_MODEL, D_EXPERT), lambda i, e: (e, 0, 0)),
                  pl.BlockSpec((None, D_EXPERT, D_MODEL), lambda i, e: (e, 0, 0)),
                  pl.BlockSpec((1, D_MODEL), lambda i, e: (0, 0))],
        out_specs=tok_spec(D_MODEL),
        out_shape=jax.ShapeDtypeStruct((n_tok, D_MODEL), F32),
        scratch_shapes=[pltpu.VMEM((tm, D_MODEL), F32)],
        compiler_params=pltpu.CompilerParams(dimension_semantics=("arbitrary", "arbitrary"),
                                             vmem_limit_bytes=VMEM_LIMIT),
        name="moe",
    )(h2, cw, x1, w_gate, w_up, w_down, g_final)


def kernel(x_prompt, x_sample, state_pool, state_conv, g_mix, w_in, w_pool_grp, pool_scale, w_pool_out, w_dw, b_dw, ln_g, ln_b, w_conv_out, w_out, g_ffn, w_rg, b_rg, w_re, b_re, w_gate, w_up, w_down, g_final):
    depth = g_mix.shape[0]
    assert depth == 1
    l = 0
    bp, seq_p, _ = x_prompt.shape
    bs, seq_s, _ = x_sample.shape
    past = PAST_LEN

    w_r = jnp.zeros((D_MODEL, ROUTER_COLS), F32)
    w_r = w_r.at[:, :N_EXPERTS].set(w_re[l]).at[:, N_EXPERTS:N_EXPERTS + N_EXPERT_GROUPS].set(w_rg[l])
    b_r = jnp.zeros((1, ROUTER_COLS), F32)
    b_r = b_r.at[0, :N_EXPERTS].set(b_re[l]).at[0, N_EXPERTS:N_EXPERTS + N_EXPERT_GROUPS].set(b_rg[l])
    w_r_hi = w_r.astype(BF16)
    w_r_lo = (w_r - w_r_hi.astype(F32)).astype(BF16)
    row = lambda a: a.reshape(1, -1)
    wts = (row(g_mix[l]), w_in[l].astype(BF16), w_pool_grp[l].astype(BF16), row(pool_scale[l]),
           w_pool_out[l].astype(BF16), w_dw[l], row(b_dw[l]), row(ln_g[l]), row(ln_b[l]),
           w_conv_out[l].astype(BF16), w_out[l].astype(BF16), row(g_ffn[l]), w_r_hi, w_r_lo, b_r)

    wg = w_gate[l].reshape(N_EXPERTS, D_MODEL, D_EXPERT)
    wu = w_up[l].reshape(N_EXPERTS, D_MODEL, D_EXPERT)
    wd = w_down[l].reshape(N_EXPERTS, D_EXPERT, D_MODEL)
    gfin = row(g_final)

    zp = jnp.zeros((bp, POOL_HIST, D_POOL), F32)
    zc = jnp.zeros((bp, CONV_HIST, D_CONV), F32)
    x1p, h2p, cwp, npp, ncp = _mixer(x_prompt, zp, zc, 0, 1, 512, wts)
    x1s, h2s, cws, nps, ncs = _mixer(x_sample, state_pool[l], state_conv[l], past, bs, seq_s, wts)

    flat = lambda a: a.reshape(-1, a.shape[-1])
    yp = _moe(flat(h2p), flat(cwp), flat(x1p), wg, wu, wd, gfin, 1024)
    ys = _moe(flat(h2s), flat(cws), flat(x1s), wg, wu, wd, gfin, bs * seq_s)
    return (yp.reshape(bp, seq_p, D_MODEL), ys.reshape(bs, seq_s, D_MODEL),
            npp[None], ncp[None], nps[None], ncs[None])
```

```python
import functools

import jax
import jax.numpy as jnp
from jax import lax
from jax.experimental import pallas as pl
from jax.experimental.pallas import tpu as pltpu

D_MODEL = 1024
D_POOL = 512
N_POOL_GROUPS = 4
POOL_GROUP = D_POOL // N_POOL_GROUPS
POOL_WINDOWS = (2, 4, 8, 16)
POOL_HIST = max(POOL_WINDOWS) - 1
D_CONV = 512
CONV_WIDTH = 31
CONV_HIST = CONV_WIDTH - 1
N_EXPERT_GROUPS = 4
EXPERTS_PER_GROUP = 8
N_EXPERTS = N_EXPERT_GROUPS * EXPERTS_PER_GROUP
TOP_K = 2
D_EXPERT = 256
RMS_EPS = 1e-6
LN_EPS = 1e-5
PAST_LEN = 1024

LANES = 128
SUBLANES = 8
POOL_PAD = 16
CONV_PAD = 32
CONV_CHUNK = 64
ROUTER_COLS = LANES
VMEM_LIMIT = 56 * 1024 * 1024

TOK_TILE = 512
BLK = 2 * SUBLANES
TILE_BLOCKS = (TOP_K * TOK_TILE + N_EXPERTS * (BLK - 1)) // BLK
TILE_ROWS = TILE_BLOCKS * BLK
CHUNK_BLOCKS = 32
CHUNK_ROWS = CHUNK_BLOCKS * BLK
ROUTE_POS1, ROUTE_POS2, ROUTE_W1, ROUTE_W2 = 0, 1, 2, 3

BF16 = jnp.bfloat16
F32 = jnp.float32
I32 = jnp.int32


def _dot(a, b):
    return jnp.dot(a, b, preferred_element_type=F32)


def _rmsnorm(x, g):
    return x * lax.rsqrt(jnp.mean(x * x, axis=-1, keepdims=True) + RMS_EPS) * g


def _routing(logits):
    lane = lax.broadcasted_iota(I32, logits.shape, 1)
    lane_f = lane.astype(F32)
    neg = jnp.float32(-jnp.inf)
    big = jnp.float32(1e9)
    is_grp = (lane >= N_EXPERTS) & (lane < N_EXPERTS + N_EXPERT_GROUPS)
    glog = jnp.where(is_grp, logits, neg)
    gmax = jnp.max(glog, axis=1, keepdims=True)
    gidx = jnp.min(jnp.where(glog == gmax, lane_f, big), axis=1, keepdims=True) - float(N_EXPERTS)
    gsum = jnp.sum(jnp.where(is_grp, jnp.exp(glog - gmax), 0.0), axis=1, keepdims=True)
    p_sel = 1.0 / gsum
    lane_grp = (lane >> 3).astype(F32)
    is_sel = (lane < N_EXPERTS) & (lane_grp == gidx)
    elog = jnp.where(is_sel, logits, neg)
    v1 = jnp.max(elog, axis=1, keepdims=True)
    i1 = jnp.min(jnp.where(elog == v1, lane_f, big), axis=1, keepdims=True)
    elog2 = jnp.where(lane_f == i1, neg, elog)
    v2 = jnp.max(elog2, axis=1, keepdims=True)
    i2 = jnp.min(jnp.where(elog2 == v2, lane_f, big), axis=1, keepdims=True)
    t = jnp.exp(v2 - v1)
    return i1, i2, p_sel / (1.0 + t), p_sel * (t / (1.0 + t))


def _dispatch(h2, i1, i2, w1, w2):
    rows = h2.shape[0]
    lane = lax.broadcasted_iota(I32, (rows, LANES), 1)
    lane_f = lane.astype(F32)
    hit1 = lane_f == i1
    hit2 = lane_f == i2
    sel = jnp.where(hit1 | hit2, 1.0, 0.0)
    earlier = lax.broadcasted_iota(I32, (rows, rows), 1) < lax.broadcasted_iota(I32, (rows, rows), 0)
    rank = _dot(jnp.where(earlier, 1.0, 0.0).astype(BF16), sel.astype(BF16))
    cnt = jnp.sum(sel, axis=0, keepdims=True)
    nblk = jnp.floor((cnt + float(BLK - 1)) * (1.0 / BLK))
    before = lax.broadcasted_iota(I32, (LANES, LANES), 0) < lax.broadcasted_iota(I32, (LANES, LANES), 1)
    bstart = _dot(jnp.broadcast_to(nblk, (SUBLANES, LANES)).astype(BF16),
                  jnp.where(before, 1.0, 0.0).astype(BF16))[0:1, :]
    pos = float(BLK) * bstart + rank
    pos1 = jnp.sum(jnp.where(hit1, pos, 0.0), axis=1, keepdims=True)
    pos2 = jnp.sum(jnp.where(hit2, pos, 0.0), axis=1, keepdims=True)
    route = jnp.where(lane == ROUTE_POS1, pos1,
                      jnp.where(lane == ROUTE_POS2, pos2,
                                jnp.where(lane == ROUTE_W1, w1, jnp.where(lane == ROUTE_W2, w2, 0.0))))
    route_t = route.T
    p1 = route_t[ROUTE_POS1:ROUTE_POS1 + 1, :].astype(I32)
    p2 = route_t[ROUTE_POS2:ROUTE_POS2 + 1, :].astype(I32)
    r_idx = lax.broadcasted_iota(I32, (TILE_ROWS, rows), 0)
    onehot = jnp.where((r_idx == p1) | (r_idx == p2), 1.0, 0.0).astype(BF16)
    return route, _dot(onehot, h2).astype(BF16), nblk


N_MIXER_IN = 18
N_MIXER_OUT = 6


def _mixer_kernel(past, sb, ls, nj, n_valid, n_steps, *refs):
    t = pl.program_id(0)
    if n_valid == n_steps:
        _mixer_tile(past, sb, ls, t % nj, nj, *refs)
        return
    x1_ref, route_ref, xs_ref, nblk_ref = refs[-(N_MIXER_OUT + 4):][:4]

    @pl.when(t < n_valid)
    def _():
        _mixer_tile(past, sb, ls, t % nj, nj, *refs)

    @pl.when(t >= n_valid)
    def _():
        for ref in (x1_ref, route_ref, xs_ref, nblk_ref):
            ref[...] = jnp.zeros(ref.shape, ref.dtype)


def _mixer_tile(past, sb, ls, j, nj, *refs):
    (x_ref, hp_ref, hc_ref, gmix_ref, win_ref, wgrp_ref, pscale_ref, wpo_ref, wdw_ref, bdw_ref,
     lng_ref, lnb_ref, wco_ref, wout_ref, gffn_ref, wrh_ref, wrl_ref, br_ref) = refs[:N_MIXER_IN]
    x1_ref, route_ref, xs_ref, nblk_ref, npool_ref, nconv_ref, ubuf, vbuf, sbuf, ybuf = refs[-(N_MIXER_OUT + 4):]
    rows = sb * ls

    @pl.when(j == 0)
    def _():
        ubuf[:, POOL_PAD - POOL_HIST:POOL_PAD, :] = hp_ref[...]
        vbuf[:, CONV_PAD - CONV_HIST:CONV_PAD, :] = hc_ref[...]

    x = x_ref[...].reshape(rows, D_MODEL)
    h = _rmsnorm(x, gmix_ref[...]).astype(BF16)

    c_gp = D_POOL + 2 * D_CONV
    u = _dot(h, win_ref[:, 0:D_POOL])
    ubuf[:, POOL_PAD:POOL_PAD + ls, :] = u.reshape(sb, ls, D_POOL)
    ga = _dot(h, win_ref[:, D_POOL:D_POOL + D_CONV])
    gb = _dot(h, win_ref[:, D_POOL + D_CONV:c_gp])
    v = ga * jax.nn.sigmoid(gb)
    vbuf[:, CONV_PAD:CONV_PAD + ls, :] = v.reshape(sb, ls, D_CONV)
    gate_pool = _dot(h, win_ref[:, c_gp:c_gp + D_MODEL])
    gate_conv = _dot(h, win_ref[:, c_gp + D_MODEL:c_gp + 2 * D_MODEL])

    t_idx = lax.broadcasted_iota(I32, (sb, ls, 1), 1)
    frames = (past + 1 + j * ls + t_idx).astype(F32)
    ys = []
    for gi, w in enumerate(POOL_WINDOWS):
        c0, c1 = gi * POOL_GROUP, (gi + 1) * POOL_GROUP
        cur = ubuf[:, POOL_PAD:POOL_PAD + ls, c0:c1]
        s = cur
        for k in range(1, w):
            s = s + ubuf[:, POOL_PAD - k:POOL_PAD - k + ls, c0:c1]
        mean = s / jnp.minimum(jnp.float32(w), frames)
        diff = (mean - cur).reshape(rows, POOL_GROUP).astype(BF16)
        ys.append(_dot(diff, wgrp_ref[gi]))
    yp = (jnp.concatenate(ys, axis=1) * pscale_ref[...]).astype(BF16)
    y_pool = _dot(yp, wpo_ref[...])
    new_pool = ubuf[:, POOL_PAD + ls - POOL_HIST:POOL_PAD + ls, :]
    ubuf[:, POOL_PAD - POOL_HIST:POOL_PAD, :] = new_pool

    first = CONV_PAD - CONV_HIST
    taps = [[o for o in range(first, first + CONV_WIDTH) if o % SUBLANES == r] for r in range(SUBLANES)]
    for r in range(1, SUBLANES):
        span = taps[r][-1] - taps[r][0] + ls
        sbuf[r - 1, :, 0:span, :] = vbuf[:, taps[r][0]:taps[r][0] + span, :]
    nb_c, nt_c = (1, CONV_CHUNK) if ls >= CONV_CHUNK else (CONV_CHUNK // ls, ls)
    for b0 in range(0, sb, nb_c):
        for t0 in range(0, ls, nt_c):
            acc = jnp.zeros((nb_c, nt_c, D_CONV), F32) + bdw_ref[...]
            for r in range(SUBLANES):
                for o in taps[r]:
                    if r == 0:
                        tap = vbuf[b0:b0 + nb_c, o + t0:o + t0 + nt_c, :]
                    else:
                        lo = o - taps[r][0] + t0
                        tap = sbuf[r - 1, b0:b0 + nb_c, lo:lo + nt_c, :]
                    acc = acc + tap * wdw_ref[o - first:o - first + 1, :]
            yf = acc.reshape(nb_c * nt_c, D_CONV)
            mu = jnp.mean(yf, axis=-1, keepdims=True)
            yc = yf - mu
            var = jnp.mean(yc * yc, axis=-1, keepdims=True)
            yn = yc * lax.rsqrt(var + LN_EPS) * lng_ref[...] + lnb_ref[...]
            r0 = b0 * ls + t0
            ybuf[r0:r0 + nb_c * nt_c, :] = (yn * jax.nn.sigmoid(yn)).astype(BF16)
    y_conv = _dot(ybuf[...], wco_ref[...])
    new_conv = vbuf[:, CONV_PAD + ls - CONV_HIST:CONV_PAD + ls, :]
    vbuf[:, CONV_PAD - CONV_HIST:CONV_PAD, :] = new_conv

    @pl.when(j == nj - 1)
    def _():
        npool_ref[...] = new_pool
        nconv_ref[...] = new_conv

    merged = jax.nn.sigmoid(gate_pool) * y_pool + jax.nn.sigmoid(gate_conv) * y_conv
    x1 = x + _dot(merged.astype(BF16), wout_ref[...])
    x1_ref[...] = x1

    h2 = _rmsnorm(x1, gffn_ref[...])
    h2_hi = h2.astype(BF16)
    h2_lo = (h2 - h2_hi.astype(F32)).astype(BF16)
    logits = (_dot(h2_hi, wrh_ref[...]) + (_dot(h2_hi, wrl_ref[...]) + _dot(h2_lo, wrh_ref[...]))) + br_ref[...]
    route, xs, nblk = _dispatch(h2_hi, *_routing(logits))
    route_ref[...] = route
    xs_ref[...] = xs
    nblk_ref[...] = jnp.broadcast_to(nblk, (SUBLANES, LANES)).astype(I32)


def _const_spec(shape):
    nd = len(shape)
    return pl.BlockSpec(shape, lambda t: (0,) * nd, pipeline_mode=pl.Buffered(1))


def _mixer(x, hist_pool, hist_conv, past, sb, ls, wts, n_tiles, tile0, shared):
    nb, seq, _ = x.shape
    assert nb % sb == 0 and seq % ls == 0 and sb * ls == TOK_TILE and ls % SUBLANES == 0 and ls >= CONV_HIST
    nj = seq // ls
    n_valid = (nb // sb) * nj
    n_steps = n_valid if shared is not None else n_tiles - tile0
    own = lambda t: jnp.minimum(t, n_valid - 1)
    hist_spec = lambda n, width: pl.BlockSpec((sb, n, width), lambda t: (own(t) // nj, 0, 0))
    in_specs = [pl.BlockSpec((sb, ls, D_MODEL), lambda t: (own(t) // nj, own(t) % nj, 0)),
                hist_spec(POOL_HIST, D_POOL), hist_spec(CONV_HIST, D_CONV)]
    in_specs += [_const_spec(w.shape) for w in wts]
    shared_shapes = (
        jax.ShapeDtypeStruct((n_tiles * TOK_TILE, D_MODEL), F32),
        jax.ShapeDtypeStruct((n_tiles * TOK_TILE, ROUTER_COLS), F32),
        jax.ShapeDtypeStruct((n_tiles, TILE_ROWS, D_MODEL), BF16),
        jax.ShapeDtypeStruct((n_tiles, SUBLANES, LANES), I32),
    )
    out_shape = shared_shapes + (jax.ShapeDtypeStruct((nb, POOL_HIST, D_POOL), F32),
                                 jax.ShapeDtypeStruct((nb, CONV_HIST, D_CONV), F32))
    out_specs = (pl.BlockSpec((TOK_TILE, D_MODEL), lambda t: (tile0 + t, 0)),
                 pl.BlockSpec((TOK_TILE, ROUTER_COLS), lambda t: (tile0 + t, 0)),
                 pl.BlockSpec((None, TILE_ROWS, D_MODEL), lambda t: (tile0 + t, 0, 0)),
                 pl.BlockSpec((None, SUBLANES, LANES), lambda t: (tile0 + t, 0, 0)),
                 hist_spec(POOL_HIST, D_POOL), hist_spec(CONV_HIST, D_CONV))
    args = [x, hist_pool, hist_conv, *wts]
    aliases = {}
    if shared is not None:
        in_specs += [pl.BlockSpec(memory_space=pl.ANY)] * len(shared)
        aliases = {len(args) + k: k for k in range(len(shared))}
        args += list(shared)
    return pl.pallas_call(
        functools.partial(_mixer_kernel, past, sb, ls, nj, n_valid, n_steps),
        grid=(n_steps,),
        in_specs=in_specs,
        out_specs=out_specs,
        out_shape=out_shape,
        input_output_aliases=aliases,
        scratch_shapes=[pltpu.VMEM((sb, POOL_PAD + ls, D_POOL), F32),
                        pltpu.VMEM((sb, CONV_PAD + ls, D_CONV), F32),
                        pltpu.VMEM((SUBLANES - 1, sb, CONV_PAD + ls, D_CONV), F32),
                        pltpu.VMEM((sb * ls, D_CONV), BF16)],
        compiler_params=pltpu.CompilerParams(dimension_semantics=("arbitrary",),
                                             vmem_limit_bytes=VMEM_LIMIT),
        name="mixer",
    )(*args)


def _chunk_tables(nblk):
    n_tiles = nblk.shape[0]
    n_chunks_max = -(-n_tiles * TILE_BLOCKS // CHUNK_BLOCKS) + N_EXPERTS
    last = jnp.arange(N_EXPERTS) == N_EXPERTS - 1
    used = jnp.sum(nblk, axis=1, keepdims=True)
    owned = nblk + jnp.where(last[None, :], TILE_BLOCKS - used, 0)
    bstart = jnp.cumsum(owned, axis=1) - owned
    prior = jnp.cumsum(owned, axis=0) - owned
    per_e = jnp.sum(owned, axis=0)
    chunks_e = -(-per_e // CHUNK_BLOCKS)
    cend = jnp.cumsum(chunks_e)
    cstart = cend - chunks_e
    b = jnp.arange(TILE_BLOCKS, dtype=I32)
    owner = jnp.sum((bstart[:, None, :] + owned[:, None, :] <= b[None, :, None]).astype(I32), axis=2)
    owner = jnp.minimum(owner, N_EXPERTS - 1)
    take = lambda a: jnp.take_along_axis(a, owner, axis=1)
    slot = cstart[owner] * CHUNK_BLOCKS + take(prior) + b[None, :] - take(bstart)
    fid = jnp.arange(n_tiles, dtype=I32)[:, None] * TILE_BLOCKS + b[None, :]
    table = jnp.full((n_chunks_max * CHUNK_BLOCKS,), -1, I32).at[slot.reshape(-1)].set(fid.reshape(-1))
    c = jnp.arange(n_chunks_max, dtype=I32)
    chunk_e = jnp.minimum(jnp.sum((cend[None, :] <= c[:, None]).astype(I32), axis=1), N_EXPERTS - 1)
    return table, chunk_e.astype(I32), cend[-1:].astype(I32)


def _expert_kernel(n_blocks, tab_ref, ce_ref, nch_ref, x_hbm, wg_ref, wu_ref, wd_ref, y_hbm,
                   xin, yout, wg_bf, wu_bf, wd_bf, sem_in, sem_out):
    c = pl.program_id(0)
    nc = pl.num_programs(0)
    slot = c % 2

    def in_copy(chunk, sl, s):
        src = jnp.maximum(tab_ref[chunk * CHUNK_BLOCKS + s], 0)
        return pltpu.make_async_copy(x_hbm.at[src], xin.at[sl, pl.ds(s * BLK, BLK), :], sem_in.at[sl])

    def out_copy(chunk, sl, s):
        t = tab_ref[chunk * CHUNK_BLOCKS + s]
        spare = n_blocks + (chunk % 2) * CHUNK_BLOCKS + s
        return pltpu.make_async_copy(yout.at[sl, pl.ds(s * BLK, BLK), :], y_hbm.at[jnp.where(t < 0, spare, t)],
                                     sem_out.at[sl])

    @pl.when(c == 0)
    def _():
        for s in range(CHUNK_BLOCKS):
            in_copy(0, 0, s).start()

    @pl.when(c + 1 < nc)
    def _():
        for s in range(CHUNK_BLOCKS):
            in_copy(c + 1, 1 - slot, s).start()

    for s in range(CHUNK_BLOCKS):
        in_copy(c, slot, s).wait()

    @pl.when((c == 0) | (ce_ref[c] != ce_ref[jnp.maximum(c - 1, 0)]))
    def _():
        wg_bf[...] = wg_ref[...].astype(BF16)
        wu_bf[...] = wu_ref[...].astype(BF16)
        wd_bf[...] = wd_ref[...].astype(BF16)

    @pl.when(c >= 2)
    def _():
        for s in range(CHUNK_BLOCKS):
            out_copy(c - 2, slot, s).wait()

    @pl.when(c < nch_ref[0])
    def _():
        x = xin[slot]
        a = _dot(x, wg_bf[...])
        b = _dot(x, wu_bf[...])
        act = ((a * jax.nn.sigmoid(a)) * b).astype(BF16)
        yout[slot] = _dot(act, wd_bf[...]).astype(BF16)

    for s in range(CHUNK_BLOCKS):
        out_copy(c, slot, s).start()

    @pl.when(c == nc - 1)
    def _():
        for s in range(CHUNK_BLOCKS):
            out_copy(c - 1, 1 - slot, s).wait()
        for s in range(CHUNK_BLOCKS):
            out_copy(c, slot, s).wait()


def _experts(xs_blocks, table, chunk_e, n_chunks, w_gate, w_up, w_down):
    n_blocks = xs_blocks.shape[0]
    n_steps = chunk_e.shape[0]
    assert n_steps >= 2
    w_spec = lambda shape: pl.BlockSpec((None,) + shape, lambda c, tab, ce, nch: (ce[c], 0, 0))
    return pl.pallas_call(
        functools.partial(_expert_kernel, n_blocks),
        grid_spec=pltpu.PrefetchScalarGridSpec(
            num_scalar_prefetch=3,
            grid=(n_steps,),
            in_specs=[pl.BlockSpec(memory_space=pl.ANY),
                      w_spec((D_MODEL, D_EXPERT)), w_spec((D_MODEL, D_EXPERT)), w_spec((D_EXPERT, D_MODEL))],
            out_specs=pl.BlockSpec(memory_space=pl.ANY),
            scratch_shapes=[pltpu.VMEM((2, CHUNK_ROWS, D_MODEL), BF16),
                            pltpu.VMEM((2, CHUNK_ROWS, D_MODEL), BF16),
                            pltpu.VMEM((D_MODEL, D_EXPERT), BF16),
                            pltpu.VMEM((D_MODEL, D_EXPERT), BF16),
                            pltpu.VMEM((D_EXPERT, D_MODEL), BF16),
                            pltpu.SemaphoreType.DMA((2,)),
                            pltpu.SemaphoreType.DMA((2,))]),
        out_shape=jax.ShapeDtypeStruct((n_blocks + 2 * CHUNK_BLOCKS, BLK, D_MODEL), BF16),
        compiler_params=pltpu.CompilerParams(dimension_semantics=("arbitrary",),
                                             vmem_limit_bytes=VMEM_LIMIT),
        name="experts",
    )(table, chunk_e, n_chunks, xs_blocks, w_gate, w_up, w_down)


def _combine_kernel(n_first, y_ref, route_ref, x1_ref, gfin_ref, o1_ref, o2_ref):
    i = pl.program_id(0)
    route = route_ref[...]
    pos1 = route[:, ROUTE_POS1:ROUTE_POS1 + 1].astype(I32)
    pos2 = route[:, ROUTE_POS2:ROUTE_POS2 + 1].astype(I32)
    w1 = route[:, ROUTE_W1:ROUTE_W1 + 1]
    w2 = route[:, ROUTE_W2:ROUTE_W2 + 1]
    r_idx = lax.broadcasted_iota(I32, (TOK_TILE, TILE_ROWS), 1)
    gather = jnp.where(r_idx == pos1, w1, jnp.where(r_idx == pos2, w2, 0.0)).astype(BF16)
    moe = _dot(gather, y_ref[...].reshape(TILE_ROWS, D_MODEL))
    out = _rmsnorm(x1_ref[...] + moe, gfin_ref[...])

    @pl.when(i < n_first)
    def _():
        o1_ref[...] = out

    @pl.when(i >= n_first)
    def _():
        o2_ref[...] = out


def _combine(y_blocks, route, x1, g_final, n_first):
    n_tiles = x1.shape[0] // TOK_TILE
    tok_spec = lambda width: pl.BlockSpec((TOK_TILE, width), lambda i: (i, 0))
    return pl.pallas_call(
        functools.partial(_combine_kernel, n_first),
        grid=(n_tiles,),
        in_specs=[pl.BlockSpec((TILE_BLOCKS, BLK, D_MODEL), lambda i: (i, 0, 0)),
                  tok_spec(ROUTER_COLS), tok_spec(D_MODEL),
                  pl.BlockSpec((1, D_MODEL), lambda i: (0, 0))],
        out_specs=(pl.BlockSpec((TOK_TILE, D_MODEL), lambda i: (jnp.minimum(i, n_first - 1), 0)),
                   pl.BlockSpec((TOK_TILE, D_MODEL), lambda i: (jnp.maximum(i - n_first, 0), 0))),
        out_shape=(jax.ShapeDtypeStruct((n_first * TOK_TILE, D_MODEL), F32),
                   jax.ShapeDtypeStruct(((n_tiles - n_first) * TOK_TILE, D_MODEL), F32)),
        compiler_params=pltpu.CompilerParams(dimension_semantics=("arbitrary",),
                                             vmem_limit_bytes=VMEM_LIMIT),
        name="combine",
    )(y_blocks, route, x1, g_final)


def kernel(x_prompt, x_sample, state_pool, state_conv, g_mix, w_in, w_pool_grp, pool_scale, w_pool_out, w_dw, b_dw, ln_g, ln_b, w_conv_out, w_out, g_ffn, w_rg, b_rg, w_re, b_re, w_gate, w_up, w_down, g_final):
    assert g_mix.shape[0] == 1
    l = 0
    bp, seq_p, _ = x_prompt.shape
    bs, seq_s, _ = x_sample.shape
    tiles_p = bp * seq_p // TOK_TILE
    tiles_s = bs * seq_s // TOK_TILE
    n_tiles = tiles_p + tiles_s

    w_r = jnp.zeros((D_MODEL, ROUTER_COLS), F32)
    w_r = w_r.at[:, :N_EXPERTS].set(w_re[l]).at[:, N_EXPERTS:N_EXPERTS + N_EXPERT_GROUPS].set(w_rg[l])
    b_r = jnp.zeros((1, ROUTER_COLS), F32)
    b_r = b_r.at[0, :N_EXPERTS].set(b_re[l]).at[0, N_EXPERTS:N_EXPERTS + N_EXPERT_GROUPS].set(b_rg[l])
    w_r_hi = w_r.astype(BF16)
    w_r_lo = (w_r - w_r_hi.astype(F32)).astype(BF16)
    row = lambda a: a.reshape(1, -1)
    wts = (row(g_mix[l]), w_in[l].astype(BF16), w_pool_grp[l].astype(BF16), row(pool_scale[l]),
           w_pool_out[l].astype(BF16), w_dw[l], row(b_dw[l]), row(ln_g[l]), row(ln_b[l]),
           w_conv_out[l].astype(BF16), w_out[l].astype(BF16), row(g_ffn[l]), w_r_hi, w_r_lo, b_r)

    zp = jnp.zeros((bp, POOL_HIST, D_POOL), F32)
    zc = jnp.zeros((bp, CONV_HIST, D_CONV), F32)
    *shared, npp, ncp = _mixer(x_prompt, zp, zc, 0, 1, TOK_TILE, wts, n_tiles, 0, None)
    x1, route, xs, nblk, nps, ncs = _mixer(x_sample, state_pool[l], state_conv[l], PAST_LEN, bs, seq_s, wts,
                                           n_tiles, tiles_p, shared)

    table, chunk_e, n_chunks = _chunk_tables(nblk[:, 0, :N_EXPERTS])
    y_blocks = _experts(xs.reshape(n_tiles * TILE_BLOCKS, BLK, D_MODEL), table, chunk_e, n_chunks,
                        w_gate[l].reshape(N_EXPERTS, D_MODEL, D_EXPERT),
                        w_up[l].reshape(N_EXPERTS, D_MODEL, D_EXPERT),
                        w_down[l].reshape(N_EXPERTS, D_EXPERT, D_MODEL))
    yp, ys = _combine(y_blocks, route, x1, row(g_final), tiles_p)
    return (yp.reshape(bp, seq_p, D_MODEL), ys.reshape(bs, seq_s, D_MODEL),
            npp[None], ncp[None], nps[None], ncs[None])
```

```python
import functools

import jax
import jax.numpy as jnp
from jax import lax
from jax.experimental import pallas as pl
from jax.experimental.pallas import tpu as pltpu

D_MODEL = 1024
D_POOL = 512
N_POOL_GROUPS = 4
POOL_GROUP = D_POOL // N_POOL_GROUPS
POOL_WINDOWS = (2, 4, 8, 16)
POOL_HIST = max(POOL_WINDOWS) - 1
D_CONV = 512
CONV_WIDTH = 31
CONV_HIST = CONV_WIDTH - 1
N_EXPERT_GROUPS = 4
EXPERTS_PER_GROUP = 8
N_EXPERTS = N_EXPERT_GROUPS * EXPERTS_PER_GROUP
TOP_K = 2
D_EXPERT = 256
RMS_EPS = 1e-6
LN_EPS = 1e-5
PAST_LEN = 1024

LANES = 128
SUBLANES = 8
POOL_PAD = 16
CONV_PAD = 32
CONV_CHUNK = 64
ROUTER_COLS = LANES
VMEM_LIMIT = 56 * 1024 * 1024

TOK_TILE = 512
BLK = 2 * SUBLANES
TILE_BLOCKS = (TOP_K * TOK_TILE + N_EXPERTS * (BLK - 1)) // BLK
TILE_ROWS = TILE_BLOCKS * BLK
CHUNK_BLOCKS = 32
CHUNK_ROWS = CHUNK_BLOCKS * BLK
ROUTE_POS1, ROUTE_POS2, ROUTE_W1, ROUTE_W2 = 0, 1, 2, 3

BF16 = jnp.bfloat16
F32 = jnp.float32
I32 = jnp.int32


def _dot(a, b):
    return jnp.dot(a, b, preferred_element_type=F32)


def _rmsnorm(x, g):
    return x * lax.rsqrt(jnp.mean(x * x, axis=-1, keepdims=True) + RMS_EPS) * g


def _routing(logits):
    lane = lax.broadcasted_iota(I32, logits.shape, 1)
    lane_f = lane.astype(F32)
    neg = jnp.float32(-jnp.inf)
    big = jnp.float32(1e9)
    is_grp = (lane >= N_EXPERTS) & (lane < N_EXPERTS + N_EXPERT_GROUPS)
    glog = jnp.where(is_grp, logits, neg)
    gmax = jnp.max(glog, axis=1, keepdims=True)
    gidx = jnp.min(jnp.where(glog == gmax, lane_f, big), axis=1, keepdims=True) - float(N_EXPERTS)
    gsum = jnp.sum(jnp.where(is_grp, jnp.exp(glog - gmax), 0.0), axis=1, keepdims=True)
    p_sel = 1.0 / gsum
    lane_grp = (lane >> 3).astype(F32)
    is_sel = (lane < N_EXPERTS) & (lane_grp == gidx)
    elog = jnp.where(is_sel, logits, neg)
    v1 = jnp.max(elog, axis=1, keepdims=True)
    i1 = jnp.min(jnp.where(elog == v1, lane_f, big), axis=1, keepdims=True)
    elog2 = jnp.where(lane_f == i1, neg, elog)
    v2 = jnp.max(elog2, axis=1, keepdims=True)
    i2 = jnp.min(jnp.where(elog2 == v2, lane_f, big), axis=1, keepdims=True)
    t = jnp.exp(v2 - v1)
    return i1, i2, p_sel / (1.0 + t), p_sel * (t / (1.0 + t))


def _dispatch(h2, i1, i2, w1, w2):
    rows = h2.shape[0]
    lane = lax.broadcasted_iota(I32, (rows, LANES), 1)
    lane_f = lane.astype(F32)
    hit1 = lane_f == i1
    hit2 = lane_f == i2
    sel = jnp.where(hit1 | hit2, 1.0, 0.0)
    earlier = lax.broadcasted_iota(I32, (rows, rows), 1) < lax.broadcasted_iota(I32, (rows, rows), 0)
    rank = _dot(jnp.where(earlier, 1.0, 0.0).astype(BF16), sel.astype(BF16))
    cnt = jnp.sum(sel, axis=0, keepdims=True)
    nblk = jnp.floor((cnt + float(BLK - 1)) * (1.0 / BLK))
    before = lax.broadcasted_iota(I32, (LANES, LANES), 0) < lax.broadcasted_iota(I32, (LANES, LANES), 1)
    bstart = _dot(jnp.broadcast_to(nblk, (SUBLANES, LANES)).astype(BF16),
                  jnp.where(before, 1.0, 0.0).astype(BF16))[0:1, :]
    pos = float(BLK) * bstart + rank
    pos1 = jnp.sum(jnp.where(hit1, pos, 0.0), axis=1, keepdims=True)
    pos2 = jnp.sum(jnp.where(hit2, pos, 0.0), axis=1, keepdims=True)
    route = jnp.where(lane == ROUTE_POS1, pos1,
                      jnp.where(lane == ROUTE_POS2, pos2,
                                jnp.where(lane == ROUTE_W1, w1, jnp.where(lane == ROUTE_W2, w2, 0.0))))
    route_t = route.T
    p1 = route_t[ROUTE_POS1:ROUTE_POS1 + 1, :].astype(I32)
    p2 = route_t[ROUTE_POS2:ROUTE_POS2 + 1, :].astype(I32)
    r_idx = lax.broadcasted_iota(I32, (TILE_ROWS, rows), 0)
    onehot = jnp.where((r_idx == p1) | (r_idx == p2), 1.0, 0.0).astype(BF16)
    return route, _dot(onehot, h2).astype(BF16), nblk


N_MIXER_IN = 18
N_MIXER_OUT = 6


def _mixer_kernel(past, sb, ls, nj, n_valid, n_steps, *refs):
    t = pl.program_id(0)
    if n_valid == n_steps:
        _mixer_tile(past, sb, ls, t % nj, nj, *refs)
        return
    x1_ref, route_ref, xs_ref, nblk_ref = refs[-(N_MIXER_OUT + 4):][:4]

    @pl.when(t < n_valid)
    def _():
        _mixer_tile(past, sb, ls, t % nj, nj, *refs)

    @pl.when(t >= n_valid)
    def _():
        for ref in (x1_ref, route_ref, xs_ref, nblk_ref):
            ref[...] = jnp.zeros(ref.shape, ref.dtype)


def _mixer_tile(past, sb, ls, j, nj, *refs):
    (x_ref, hp_ref, hc_ref, gmix_ref, win_ref, wgrp_ref, pscale_ref, wpo_ref, wdw_ref, bdw_ref,
     lng_ref, lnb_ref, wco_ref, wout_ref, gffn_ref, wrh_ref, wrl_ref, br_ref) = refs[:N_MIXER_IN]
    x1_ref, route_ref, xs_ref, nblk_ref, npool_ref, nconv_ref, ubuf, vbuf, sbuf, ybuf = refs[-(N_MIXER_OUT + 4):]
    rows = sb * ls

    @pl.when(j == 0)
    def _():
        ubuf[:, POOL_PAD - POOL_HIST:POOL_PAD, :] = hp_ref[...]
        vbuf[:, CONV_PAD - CONV_HIST:CONV_PAD, :] = hc_ref[...]

    x = x_ref[...].reshape(rows, D_MODEL)
    h = _rmsnorm(x, gmix_ref[...]).astype(BF16)

    c_gp = D_POOL + 2 * D_CONV
    u = _dot(h, win_ref[:, 0:D_POOL])
    ubuf[:, POOL_PAD:POOL_PAD + ls, :] = u.reshape(sb, ls, D_POOL)
    ga = _dot(h, win_ref[:, D_POOL:D_POOL + D_CONV])
    gb = _dot(h, win_ref[:, D_POOL + D_CONV:c_gp])
    v = ga * jax.nn.sigmoid(gb)
    vbuf[:, CONV_PAD:CONV_PAD + ls, :] = v.reshape(sb, ls, D_CONV)
    gate_pool = _dot(h, win_ref[:, c_gp:c_gp + D_MODEL])
    gate_conv = _dot(h, win_ref[:, c_gp + D_MODEL:c_gp + 2 * D_MODEL])

    t_idx = lax.broadcasted_iota(I32, (sb, ls, 1), 1)
    frames = (past + 1 + j * ls + t_idx).astype(F32)
    ys = []
    for gi, w in enumerate(POOL_WINDOWS):
        c0, c1 = gi * POOL_GROUP, (gi + 1) * POOL_GROUP
        cur = ubuf[:, POOL_PAD:POOL_PAD + ls, c0:c1]
        s = cur
        for k in range(1, w):
            s = s + ubuf[:, POOL_PAD - k:POOL_PAD - k + ls, c0:c1]
        mean = s / jnp.minimum(jnp.float32(w), frames)
        diff = (mean - cur).reshape(rows, POOL_GROUP).astype(BF16)
        ys.append(_dot(diff, wgrp_ref[gi]))
    yp = (jnp.concatenate(ys, axis=1) * pscale_ref[...]).astype(BF16)
    y_pool = _dot(yp, wpo_ref[...])
    new_pool = ubuf[:, POOL_PAD + ls - POOL_HIST:POOL_PAD + ls, :]
    ubuf[:, POOL_PAD - POOL_HIST:POOL_PAD, :] = new_pool

    first = CONV_PAD - CONV_HIST
    taps = [[o for o in range(first, first + CONV_WIDTH) if o % SUBLANES == r] for r in range(SUBLANES)]
    for r in range(1, SUBLANES):
        span = taps[r][-1] - taps[r][0] + ls
        sbuf[r - 1, :, 0:span, :] = vbuf[:, taps[r][0]:taps[r][0] + span, :]
    nb_c, nt_c = (1, CONV_CHUNK) if ls >= CONV_CHUNK else (CONV_CHUNK // ls, ls)
    for b0 in range(0, sb, nb_c):
        for t0 in range(0, ls, nt_c):
            acc = jnp.zeros((nb_c, nt_c, D_CONV), F32) + bdw_ref[...]
            for r in range(SUBLANES):
                for o in taps[r]:
                    if r == 0:
                        tap = vbuf[b0:b0 + nb_c, o + t0:o + t0 + nt_c, :]
                    else:
                        lo = o - taps[r][0] + t0
                        tap = sbuf[r - 1, b0:b0 + nb_c, lo:lo + nt_c, :]
                    acc = acc + tap * wdw_ref[o - first:o - first + 1, :]
            yf = acc.reshape(nb_c * nt_c, D_CONV)
            mu = jnp.mean(yf, axis=-1, keepdims=True)
            yc = yf - mu
            var = jnp.mean(yc * yc, axis=-1, keepdims=True)
            yn = yc * lax.rsqrt(var + LN_EPS) * lng_ref[...] + lnb_ref[...]
            r0 = b0 * ls + t0
            ybuf[r0:r0 + nb_c * nt_c, :] = (yn * jax.nn.sigmoid(yn)).astype(BF16)
    y_conv = _dot(ybuf[...], wco_ref[...])
    new_conv = vbuf[:, CONV_PAD + ls - CONV_HIST:CONV_PAD + ls, :]
    vbuf[:, CONV_PAD - CONV_HIST:CONV_PAD, :] = new_conv

    @pl.when(j == nj - 1)
    def _():
        npool_ref[...] = new_pool
        nconv_ref[...] = new_conv

    merged = jax.nn.sigmoid(gate_pool) * y_pool + jax.nn.sigmoid(gate_conv) * y_conv
    x1 = x + _dot(merged.astype(BF16), wout_ref[...])
    x1_ref[...] = x1

    h2 = _rmsnorm(x1, gffn_ref[...])
    h2_hi = h2.astype(BF16)
    h2_lo = (h2 - h2_hi.astype(F32)).astype(BF16)
    logits = (_dot(h2_hi, wrh_ref[...]) + (_dot(h2_hi, wrl_ref[...]) + _dot(h2_lo, wrh_ref[...]))) + br_ref[...]
    route, xs, nblk = _dispatch(h2_hi, *_routing(logits))
    route_ref[...] = route
    xs_ref[...] = xs
    nblk_ref[...] = jnp.broadcast_to(nblk, (SUBLANES, LANES)).astype(I32)


def _const_spec(shape):
    nd = len(shape)
    return pl.BlockSpec(shape, lambda t: (0,) * nd, pipeline_mode=pl.Buffered(1))


def _mixer(x, hist_pool, hist_conv, past, sb, ls, wts, n_tiles, tile0, shared):
    nb, seq, _ = x.shape
    assert nb % sb == 0 and seq % ls == 0 and sb * ls == TOK_TILE and ls % SUBLANES == 0 and ls >= CONV_HIST
    nj = seq // ls
    n_valid = (nb // sb) * nj
    n_steps = n_valid if shared is not None else n_tiles - tile0
    own = lambda t: jnp.minimum(t, n_valid - 1)
    hist_spec = lambda n, width: pl.BlockSpec((sb, n, width), lambda t: (own(t) // nj, 0, 0))
    in_specs = [pl.BlockSpec((sb, ls, D_MODEL), lambda t: (own(t) // nj, own(t) % nj, 0)),
                hist_spec(POOL_HIST, D_POOL), hist_spec(CONV_HIST, D_CONV)]
    in_specs += [_const_spec(w.shape) for w in wts]
    shared_shapes = (
        jax.ShapeDtypeStruct((n_tiles * TOK_TILE, D_MODEL), F32),
        jax.ShapeDtypeStruct((n_tiles * TOK_TILE, ROUTER_COLS), F32),
        jax.ShapeDtypeStruct((n_tiles, TILE_ROWS, D_MODEL), BF16),
        jax.ShapeDtypeStruct((n_tiles, SUBLANES, LANES), I32),
    )
    out_shape = shared_shapes + (jax.ShapeDtypeStruct((nb, POOL_HIST, D_POOL), F32),
                                 jax.ShapeDtypeStruct((nb, CONV_HIST, D_CONV), F32))
    out_specs = (pl.BlockSpec((TOK_TILE, D_MODEL), lambda t: (tile0 + t, 0)),
                 pl.BlockSpec((TOK_TILE, ROUTER_COLS), lambda t: (tile0 + t, 0)),
                 pl.BlockSpec((None, TILE_ROWS, D_MODEL), lambda t: (tile0 + t, 0, 0)),
                 pl.BlockSpec((None, SUBLANES, LANES), lambda t: (tile0 + t, 0, 0)),
                 hist_spec(POOL_HIST, D_POOL), hist_spec(CONV_HIST, D_CONV))
    args = [x, hist_pool, hist_conv, *wts]
    aliases = {}
    if shared is not None:
        in_specs += [pl.BlockSpec(memory_space=pl.ANY)] * len(shared)
        aliases = {len(args) + k: k for k in range(len(shared))}
        args += list(shared)
    return pl.pallas_call(
        functools.partial(_mixer_kernel, past, sb, ls, nj, n_valid, n_steps),
        grid=(n_steps,),
        in_specs=in_specs,
        out_specs=out_specs,
        out_shape=out_shape,
        input_output_aliases=aliases,
        scratch_shapes=[pltpu.VMEM((sb, POOL_PAD + ls, D_POOL), F32),
                        pltpu.VMEM((sb, CONV_PAD + ls, D_CONV), F32),
                        pltpu.VMEM((SUBLANES - 1, sb, CONV_PAD + ls, D_CONV), F32),
                        pltpu.VMEM((sb * ls, D_CONV), BF16)],
        compiler_params=pltpu.CompilerParams(dimension_semantics=("arbitrary",),
                                             vmem_limit_bytes=VMEM_LIMIT),
        name="mixer",
    )(*args)


def _chunk_tables(nblk, n_chunks_max):
    n_tiles = nblk.shape[0]
    e_ids = jnp.arange(N_EXPERTS, dtype=I32)
    per_e = jnp.sum(nblk, axis=0)
    chunks_e = -(-per_e // CHUNK_BLOCKS)
    cend = jnp.cumsum(chunks_e)
    cstart = cend - chunks_e
    n_chunks = cend[-1]
    c = jnp.arange(n_chunks_max, dtype=I32)
    chunk_e = jnp.minimum(jnp.sum((cend[None, :] <= c[:, None]).astype(I32), axis=1), N_EXPERTS - 1)
    is_e = chunk_e[:, None] == e_ids[None, :]
    of_chunk = lambda v: jnp.sum(jnp.where(is_e, v[None, :], 0), axis=1)
    of_chunk_t = lambda a: jnp.sum(jnp.where(is_e[:, None, :], a[None, :, :], 0), axis=2)
    cum_t = jnp.cumsum(nblk, axis=0)
    bstart = jnp.cumsum(nblk, axis=1) - nblk
    k = ((c - of_chunk(cstart)) * CHUNK_BLOCKS)[:, None] + jnp.arange(CHUNK_BLOCKS, dtype=I32)[None, :]
    tile = jnp.sum((of_chunk_t(cum_t)[:, None, :] <= k[:, :, None]).astype(I32), axis=2)
    is_t = tile[:, :, None] == jnp.arange(n_tiles, dtype=I32)[None, None, :]
    shift = of_chunk_t(bstart - (cum_t - nblk))
    blk = k + jnp.sum(jnp.where(is_t, shift[:, None, :], 0), axis=2)
    valid = (k < of_chunk(per_e)[:, None]) & (c[:, None] < n_chunks)
    table = jnp.where(valid, tile * TILE_BLOCKS + blk, -1)
    return table.reshape(-1).astype(I32), chunk_e.astype(I32), n_chunks.reshape(1).astype(I32)


def _expert_kernel(spare0, tab_ref, ce_ref, nch_ref, x_hbm, wg_ref, wu_ref, wd_ref, y_hbm,
                   xin, yout, wgu_bf, wd_bf, sem_in, sem_out):
    c = pl.program_id(0)
    nc = pl.num_programs(0)
    nch = nch_ref[0]

    zero_block = spare0 + 2 * CHUNK_BLOCKS

    def src_block(chunk, s):
        t = tab_ref[chunk * CHUNK_BLOCKS + s]
        return jnp.where(t < 0, zero_block, t)

    def dst_block(chunk, s):
        t = tab_ref[chunk * CHUNK_BLOCKS + s]
        return jnp.where(t < 0, spare0 + (chunk % 2) * CHUNK_BLOCKS + s, t)

    def in_copy(block, sl, s):
        return pltpu.make_async_copy(x_hbm.at[block], xin.at[sl, pl.ds(s * BLK, BLK), :], sem_in.at[sl])

    def out_copy(block, sl, s):
        return pltpu.make_async_copy(yout.at[sl, pl.ds(s * BLK, BLK), :], y_hbm.at[block], sem_out.at[sl])

    @pl.when(c < nch)
    def _():
        slot = c % 2

        @pl.when(c == 0)
        def _():
            for s in range(CHUNK_BLOCKS):
                in_copy(src_block(0, s), 0, s).start()

        @pl.when((c == 0) | (ce_ref[c] != ce_ref[jnp.maximum(c - 1, 0)]))
        def _():
            wgu_bf[:, 0:D_EXPERT] = wg_ref[...].astype(BF16)
            wgu_bf[:, D_EXPERT:2 * D_EXPERT] = wu_ref[...].astype(BF16)
            wd_bf[...] = wd_ref[...].astype(BF16)

        for s in range(CHUNK_BLOCKS):
            in_copy(src_block(c, s), slot, s).wait()

        @pl.when(c >= 2)
        def _():
            for s in range(CHUNK_BLOCKS):
                out_copy(dst_block(c - 2, s), slot, s).wait()

        nxt = jnp.minimum(c + 1, nc - 1)
        for s in range(CHUNK_BLOCKS):
            in_copy(jnp.where(c + 1 < nch, src_block(nxt, s), zero_block), 1 - slot, s).start()

        ab = _dot(xin[slot], wgu_bf[...])
        a = ab[:, 0:D_EXPERT]
        b = ab[:, D_EXPERT:2 * D_EXPERT]
        act = ((a * jax.nn.sigmoid(a)) * b).astype(BF16)
        yout[slot] = _dot(act, wd_bf[...]).astype(BF16)

        for s in range(CHUNK_BLOCKS):
            out_copy(dst_block(c, s), slot, s).start()

        @pl.when(c == nch - 1)
        def _():
            for s in range(CHUNK_BLOCKS):
                in_copy(zero_block, 1 - slot, s).wait()

            @pl.when(c >= 1)
            def _():
                for s in range(CHUNK_BLOCKS):
                    out_copy(dst_block(c - 1, s), 1 - slot, s).wait()

            for s in range(CHUNK_BLOCKS):
                out_copy(dst_block(c, s), slot, s).wait()


def _experts(xs_blocks, n_used_blocks, table, chunk_e, n_chunks, w_gate, w_up, w_down):
    assert xs_blocks.shape[0] - n_used_blocks > 2 * CHUNK_BLOCKS
    n_steps = chunk_e.shape[0]
    w_spec = lambda shape: pl.BlockSpec((None,) + shape, lambda c, tab, ce, nch: (ce[c], 0, 0))
    return pl.pallas_call(
        functools.partial(_expert_kernel, n_used_blocks),
        grid_spec=pltpu.PrefetchScalarGridSpec(
            num_scalar_prefetch=3,
            grid=(n_steps,),
            in_specs=[pl.BlockSpec(memory_space=pl.ANY),
                      w_spec((D_MODEL, D_EXPERT)), w_spec((D_MODEL, D_EXPERT)), w_spec((D_EXPERT, D_MODEL))],
            out_specs=pl.BlockSpec(memory_space=pl.ANY),
            scratch_shapes=[pltpu.VMEM((2, CHUNK_ROWS, D_MODEL), BF16),
                            pltpu.VMEM((2, CHUNK_ROWS, D_MODEL), BF16),
                            pltpu.VMEM((D_MODEL, 2 * D_EXPERT), BF16),
                            pltpu.VMEM((D_EXPERT, D_MODEL), BF16),
                            pltpu.SemaphoreType.DMA((2,)),
                            pltpu.SemaphoreType.DMA((2,))]),
        out_shape=jax.ShapeDtypeStruct(xs_blocks.shape, xs_blocks.dtype),
        input_output_aliases={3: 0},
        compiler_params=pltpu.CompilerParams(dimension_semantics=("arbitrary",),
                                             vmem_limit_bytes=VMEM_LIMIT),
        name="experts",
    )(table, chunk_e, n_chunks, xs_blocks, w_gate, w_up, w_down)


def _combine_kernel(n_first, y_ref, route_ref, x1_ref, gfin_ref, o1_ref, o2_ref):
    i = pl.program_id(0)
    route = route_ref[...]
    pos1 = route[:, ROUTE_POS1:ROUTE_POS1 + 1].astype(I32)
    pos2 = route[:, ROUTE_POS2:ROUTE_POS2 + 1].astype(I32)
    w1 = route[:, ROUTE_W1:ROUTE_W1 + 1]
    w2 = route[:, ROUTE_W2:ROUTE_W2 + 1]
    r_idx = lax.broadcasted_iota(I32, (TOK_TILE, TILE_ROWS), 1)
    gather = jnp.where(r_idx == pos1, w1, jnp.where(r_idx == pos2, w2, 0.0)).astype(BF16)
    moe = _dot(gather, y_ref[...].reshape(TILE_ROWS, D_MODEL))
    out = _rmsnorm(x1_ref[...] + moe, gfin_ref[...])

    @pl.when(i < n_first)
    def _():
        o1_ref[...] = out

    @pl.when(i >= n_first)
    def _():
        o2_ref[...] = out


def _combine(y_blocks, route, x1, g_final, n_first, n_tiles):
    tok_spec = lambda width: pl.BlockSpec((TOK_TILE, width), lambda i: (i, 0))
    return pl.pallas_call(
        functools.partial(_combine_kernel, n_first),
        grid=(n_tiles,),
        in_specs=[pl.BlockSpec((TILE_BLOCKS, BLK, D_MODEL), lambda i: (i, 0, 0)),
                  tok_spec(ROUTER_COLS), tok_spec(D_MODEL),
                  pl.BlockSpec((1, D_MODEL), lambda i: (0, 0))],
        out_specs=(pl.BlockSpec((TOK_TILE, D_MODEL), lambda i: (jnp.minimum(i, n_first - 1), 0)),
                   pl.BlockSpec((TOK_TILE, D_MODEL), lambda i: (jnp.maximum(i - n_first, 0), 0))),
        out_shape=(jax.ShapeDtypeStruct((n_first * TOK_TILE, D_MODEL), F32),
                   jax.ShapeDtypeStruct(((n_tiles - n_first) * TOK_TILE, D_MODEL), F32)),
        compiler_params=pltpu.CompilerParams(dimension_semantics=("arbitrary",),
                                             vmem_limit_bytes=VMEM_LIMIT),
        name="combine",
    )(y_blocks, route, x1, g_final)


def kernel(x_prompt, x_sample, state_pool, state_conv, g_mix, w_in, w_pool_grp, pool_scale, w_pool_out, w_dw, b_dw, ln_g, ln_b, w_conv_out, w_out, g_ffn, w_rg, b_rg, w_re, b_re, w_gate, w_up, w_down, g_final):
    assert g_mix.shape[0] == 1
    l = 0
    bp, seq_p, _ = x_prompt.shape
    bs, seq_s, _ = x_sample.shape
    tiles_p = bp * seq_p // TOK_TILE
    tiles_s = bs * seq_s // TOK_TILE
    n_tiles = tiles_p + tiles_s

    w_r = jnp.zeros((D_MODEL, ROUTER_COLS), F32)
    w_r = w_r.at[:, :N_EXPERTS].set(w_re[l]).at[:, N_EXPERTS:N_EXPERTS + N_EXPERT_GROUPS].set(w_rg[l])
    b_r = jnp.zeros((1, ROUTER_COLS), F32)
    b_r = b_r.at[0, :N_EXPERTS].set(b_re[l]).at[0, N_EXPERTS:N_EXPERTS + N_EXPERT_GROUPS].set(b_rg[l])
    w_r_hi = w_r.astype(BF16)
    w_r_lo = (w_r - w_r_hi.astype(F32)).astype(BF16)
    row = lambda a: a.reshape(1, -1)
    wts = (row(g_mix[l]), w_in[l].astype(BF16), w_pool_grp[l].astype(BF16), row(pool_scale[l]),
           w_pool_out[l].astype(BF16), w_dw[l], row(b_dw[l]), row(ln_g[l]), row(ln_b[l]),
           w_conv_out[l].astype(BF16), w_out[l].astype(BF16), row(g_ffn[l]), w_r_hi, w_r_lo, b_r)

    zp = jnp.zeros((bp, POOL_HIST, D_POOL), F32)
    zc = jnp.zeros((bp, CONV_HIST, D_CONV), F32)
    *shared, npp, ncp = _mixer(x_prompt, zp, zc, 0, 1, TOK_TILE, wts, n_tiles + 1, 0, None)
    x1, route, xs, nblk, nps, ncs = _mixer(x_sample, state_pool[l], state_conv[l], PAST_LEN, bs, seq_s, wts,
                                           n_tiles + 1, tiles_p, shared)

    n_chunks_max = -(-n_tiles * TILE_BLOCKS // CHUNK_BLOCKS) + N_EXPERTS
    table, chunk_e, n_chunks = _chunk_tables(nblk[:n_tiles, 0, :N_EXPERTS], n_chunks_max)
    y_blocks = _experts(xs.reshape((n_tiles + 1) * TILE_BLOCKS, BLK, D_MODEL), n_tiles * TILE_BLOCKS,
                        table, chunk_e, n_chunks,
                        w_gate[l].reshape(N_EXPERTS, D_MODEL, D_EXPERT),
                        w_up[l].reshape(N_EXPERTS, D_MODEL, D_EXPERT),
                        w_down[l].reshape(N_EXPERTS, D_EXPERT, D_MODEL))
    yp, ys = _combine(y_blocks, route, x1, row(g_final), tiles_p, n_tiles)
    return (yp.reshape(bp, seq_p, D_MODEL), ys.reshape(bs, seq_s, D_MODEL),
            npp[None], ncp[None], nps[None], ncs[None])
```

```python
import functools

import jax
import jax.numpy as jnp
from jax import lax
from jax.experimental import pallas as pl
from jax.experimental.pallas import tpu as pltpu

D_MODEL = 1024
D_POOL = 512
N_POOL_GROUPS = 4
POOL_GROUP = D_POOL // N_POOL_GROUPS
POOL_WINDOWS = (2, 4, 8, 16)
POOL_HIST = max(POOL_WINDOWS) - 1
D_CONV = 512
CONV_WIDTH = 31
CONV_HIST = CONV_WIDTH - 1
N_EXPERT_GROUPS = 4
EXPERTS_PER_GROUP = 8
N_EXPERTS = N_EXPERT_GROUPS * EXPERTS_PER_GROUP
TOP_K = 2
D_EXPERT = 256
RMS_EPS = 1e-6
LN_EPS = 1e-5
PAST_LEN = 1024

LANES = 128
SUBLANES = 8
POOL_PAD = 16
CONV_PAD = 32
CONV_CHUNK = 64
ROUTER_COLS = LANES
VMEM_LIMIT = 56 * 1024 * 1024

TOK_TILE = 512
BLK = 2 * SUBLANES
TILE_BLOCKS = (TOP_K * TOK_TILE + N_EXPERTS * (BLK - 1)) // BLK
TILE_ROWS = TILE_BLOCKS * BLK
CHUNK_BLOCKS = 32
CHUNK_ROWS = CHUNK_BLOCKS * BLK
ROUTE_POS1, ROUTE_POS2, ROUTE_W1, ROUTE_W2 = 0, 1, 2, 3

BF16 = jnp.bfloat16
F32 = jnp.float32
I32 = jnp.int32


def _dot(a, b):
    return jnp.dot(a, b, preferred_element_type=F32)


def _rmsnorm(x, g):
    return x * lax.rsqrt(jnp.mean(x * x, axis=-1, keepdims=True) + RMS_EPS) * g


def _routing(logits):
    lane = lax.broadcasted_iota(I32, logits.shape, 1)
    lane_f = lane.astype(F32)
    neg = jnp.float32(-jnp.inf)
    big = jnp.float32(1e9)
    is_grp = (lane >= N_EXPERTS) & (lane < N_EXPERTS + N_EXPERT_GROUPS)
    glog = jnp.where(is_grp, logits, neg)
    gmax = jnp.max(glog, axis=1, keepdims=True)
    gidx = jnp.min(jnp.where(glog == gmax, lane_f, big), axis=1, keepdims=True) - float(N_EXPERTS)
    gsum = jnp.sum(jnp.where(is_grp, jnp.exp(glog - gmax), 0.0), axis=1, keepdims=True)
    p_sel = 1.0 / gsum
    lane_grp = (lane >> 3).astype(F32)
    is_sel = (lane < N_EXPERTS) & (lane_grp == gidx)
    elog = jnp.where(is_sel, logits, neg)
    v1 = jnp.max(elog, axis=1, keepdims=True)
    i1 = jnp.min(jnp.where(elog == v1, lane_f, big), axis=1, keepdims=True)
    elog2 = jnp.where(lane_f == i1, neg, elog)
    v2 = jnp.max(elog2, axis=1, keepdims=True)
    i2 = jnp.min(jnp.where(elog2 == v2, lane_f, big), axis=1, keepdims=True)
    t = jnp.exp(v2 - v1)
    return i1, i2, p_sel / (1.0 + t), p_sel * (t / (1.0 + t))


def _dispatch(h2, i1, i2, w1, w2):
    rows = h2.shape[0]
    lane = lax.broadcasted_iota(I32, (rows, LANES), 1)
    lane_f = lane.astype(F32)
    hit1 = lane_f == i1
    hit2 = lane_f == i2
    sel = jnp.where(hit1 | hit2, 1.0, 0.0)
    earlier = lax.broadcasted_iota(I32, (rows, rows), 1) < lax.broadcasted_iota(I32, (rows, rows), 0)
    rank = _dot(jnp.where(earlier, 1.0, 0.0).astype(BF16), sel.astype(BF16))
    cnt = jnp.sum(sel, axis=0, keepdims=True)
    nblk = jnp.floor((cnt + float(BLK - 1)) * (1.0 / BLK))
    before = lax.broadcasted_iota(I32, (LANES, LANES), 0) < lax.broadcasted_iota(I32, (LANES, LANES), 1)
    bstart = _dot(jnp.broadcast_to(nblk, (SUBLANES, LANES)).astype(BF16),
                  jnp.where(before, 1.0, 0.0).astype(BF16))[0:1, :]
    pos = float(BLK) * bstart + rank
    pos1 = jnp.sum(jnp.where(hit1, pos, 0.0), axis=1, keepdims=True)
    pos2 = jnp.sum(jnp.where(hit2, pos, 0.0), axis=1, keepdims=True)
    route = jnp.where(lane == ROUTE_POS1, pos1,
                      jnp.where(lane == ROUTE_POS2, pos2,
                                jnp.where(lane == ROUTE_W1, w1, jnp.where(lane == ROUTE_W2, w2, 0.0))))
    route_t = route.T
    p1 = route_t[ROUTE_POS1:ROUTE_POS1 + 1, :].astype(I32)
    p2 = route_t[ROUTE_POS2:ROUTE_POS2 + 1, :].astype(I32)
    r_idx = lax.broadcasted_iota(I32, (TILE_ROWS, rows), 0)
    onehot = jnp.where((r_idx == p1) | (r_idx == p2), 1.0, 0.0).astype(BF16)
    return route, _dot(onehot, h2).astype(BF16), nblk


N_MIXER_IN = 19
N_MIXER_OUT = 6
N_MIXER_SCRATCH = 7
GATE_CHUNK = 2 * LANES
N_GATE_CHUNKS = 2 * D_MODEL // GATE_CHUNK


def _mixer_kernel(past, sb, ls, nj, n_valid, n_steps, *refs):
    t = pl.program_id(0)
    if n_valid == n_steps:
        _mixer_tile(past, sb, ls, t % nj, nj, *refs)
        return
    outs = refs[-(N_MIXER_OUT + N_MIXER_SCRATCH):][:N_MIXER_OUT]

    @pl.when(t < n_valid)
    def _():
        _mixer_tile(past, sb, ls, t % nj, nj, *refs)

    @pl.when(t >= n_valid)
    def _():
        for ref in outs[:4]:
            ref[...] = jnp.zeros(ref.shape, ref.dtype)


def _mixer_tile(past, sb, ls, j, nj, *refs):
    (x_ref, hp_ref, hc_ref, gmix_ref, win_ref, wgate_ref, wgrp_ref, pscale_ref, wpo_ref, wdw_ref, bdw_ref,
     lng_ref, lnb_ref, wco_ref, wout_ref, gffn_ref, wrh_ref, wrl_ref, br_ref) = refs[:N_MIXER_IN]
    x1_ref, route_ref, xs_ref, nblk_ref, npool_ref, nconv_ref = refs[-(N_MIXER_OUT + N_MIXER_SCRATCH):][:N_MIXER_OUT]
    ubuf, vbuf, sbuf, hbuf, gbuf, cbuf, ybuf = refs[-N_MIXER_SCRATCH:]
    rows = sb * ls
    assert rows // CONV_CHUNK == N_GATE_CHUNKS

    @pl.when(j == 0)
    def _():
        ubuf[:, POOL_PAD - POOL_HIST:POOL_PAD, :] = hp_ref[...]
        vbuf[:, CONV_PAD - CONV_HIST:CONV_PAD, :] = hc_ref[...]

    x = x_ref[...].reshape(rows, D_MODEL)
    hbuf[...] = _rmsnorm(x, gmix_ref[...]).astype(BF16)
    u = _dot(hbuf[...], win_ref[:, 0:D_POOL])
    ubuf[:, POOL_PAD:POOL_PAD + ls, :] = u.reshape(sb, ls, D_POOL)
    ga = _dot(hbuf[...], win_ref[:, D_POOL:D_POOL + D_CONV])
    gb = _dot(hbuf[...], win_ref[:, D_POOL + D_CONV:D_POOL + 2 * D_CONV])
    vbuf[:, CONV_PAD:CONV_PAD + ls, :] = (ga * jax.nn.sigmoid(gb)).reshape(sb, ls, D_CONV)
    for i in range(N_GATE_CHUNKS):
        gbuf[i] = _dot(hbuf[...], wgate_ref[i])

    t_idx = lax.broadcasted_iota(I32, (sb, ls, 1), 1)
    frames = (past + 1 + j * ls + t_idx).astype(F32)
    ys = []
    for gi, w in enumerate(POOL_WINDOWS):
        c0, c1 = gi * POOL_GROUP, (gi + 1) * POOL_GROUP
        cur = ubuf[:, POOL_PAD:POOL_PAD + ls, c0:c1]
        s = cur
        for k in range(1, w):
            s = s + ubuf[:, POOL_PAD - k:POOL_PAD - k + ls, c0:c1]
        mean = s / jnp.minimum(jnp.float32(w), frames)
        ys.append(_dot((mean - cur).reshape(rows, POOL_GROUP).astype(BF16), wgrp_ref[gi]))
    yp = (jnp.concatenate(ys, axis=1) * pscale_ref[...]).astype(BF16)
    y_pool = _dot(yp, wpo_ref[...])
    new_pool = ubuf[:, POOL_PAD + ls - POOL_HIST:POOL_PAD + ls, :]
    ubuf[:, POOL_PAD - POOL_HIST:POOL_PAD, :] = new_pool

    first = CONV_PAD - CONV_HIST
    taps = [[o for o in range(first, first + CONV_WIDTH) if o % SUBLANES == r] for r in range(SUBLANES)]
    for r in range(1, SUBLANES):
        span = taps[r][-1] - taps[r][0] + ls
        sbuf[r - 1, :, 0:span, :] = vbuf[:, taps[r][0]:taps[r][0] + span, :]
    nb_c, nt_c = (1, CONV_CHUNK) if ls >= CONV_CHUNK else (CONV_CHUNK // ls, ls)
    n_grp = nt_c // SUBLANES

    def chunk(i, carry):
        if nb_c == 1:
            bsl, t0 = slice(0, 1), i * CONV_CHUNK
        else:
            bsl, t0 = pl.ds(i * nb_c, nb_c), 0
        for c0 in range(0, D_CONV, LANES):
            cs = slice(c0, c0 + LANES)
            accs = [jnp.zeros((nb_c, SUBLANES, LANES), F32) + bdw_ref[:, cs] for _ in range(n_grp)]
            for r in range(SUBLANES):
                ws = [jnp.broadcast_to(wdw_ref[o - first:o - first + 1, cs], (nb_c, SUBLANES, LANES)) for o in taps[r]]
                for q in range(n_grp + len(taps[r]) - 1):
                    if r == 0:
                        val = vbuf[bsl, pl.ds(t0 + taps[0][0] + q * SUBLANES, SUBLANES), cs]
                    else:
                        val = sbuf[r - 1, bsl, pl.ds(t0 + q * SUBLANES, SUBLANES), cs]
                    for a in range(len(taps[r])):
                        if 0 <= q - a < n_grp:
                            accs[q - a] = accs[q - a] + val * ws[a]
            cbuf[:, cs] = jnp.concatenate(accs, axis=1).reshape(CONV_CHUNK, LANES)
        yf = cbuf[...]
        mu = jnp.mean(yf, axis=-1, keepdims=True)
        yc = yf - mu
        var = jnp.mean(yc * yc, axis=-1, keepdims=True)
        yn = yc * lax.rsqrt(var + LN_EPS) * lng_ref[...] + lnb_ref[...]
        ybuf[pl.ds(i * CONV_CHUNK, CONV_CHUNK), :] = (yn * jax.nn.sigmoid(yn)).astype(BF16)
        return carry

    for i in range(N_GATE_CHUNKS):
        chunk(i, 0)
    y_conv = _dot(ybuf[...], wco_ref[...])
    new_conv = vbuf[:, CONV_PAD + ls - CONV_HIST:CONV_PAD + ls, :]
    vbuf[:, CONV_PAD - CONV_HIST:CONV_PAD, :] = new_conv

    @pl.when(j == nj - 1)
    def _():
        npool_ref[...] = new_pool
        nconv_ref[...] = new_conv

    half = N_GATE_CHUNKS // 2
    merged = []
    for c in range(half):
        cs = slice(c * GATE_CHUNK, (c + 1) * GATE_CHUNK)
        merged.append(jax.nn.sigmoid(gbuf[c]) * y_pool[:, cs] + jax.nn.sigmoid(gbuf[half + c]) * y_conv[:, cs])
    x1 = x + _dot(jnp.concatenate(merged, axis=1).astype(BF16), wout_ref[...])
    x1_ref[...] = x1

    h2 = _rmsnorm(x1, gffn_ref[...])
    h2_hi = h2.astype(BF16)
    h2_lo = (h2 - h2_hi.astype(F32)).astype(BF16)
    logits = (_dot(h2_hi, wrh_ref[...]) + (_dot(h2_hi, wrl_ref[...]) + _dot(h2_lo, wrh_ref[...]))) + br_ref[...]
    route, xs, nblk = _dispatch(h2_hi, *_routing(logits))
    route_ref[...] = route
    xs_ref[...] = xs
    nblk_ref[...] = jnp.broadcast_to(nblk, (SUBLANES, LANES)).astype(I32)


def _const_spec(shape):
    nd = len(shape)
    return pl.BlockSpec(shape, lambda t: (0,) * nd, pipeline_mode=pl.Buffered(1))


def _mixer(x, hist_pool, hist_conv, past, sb, ls, wts, n_tiles, tile0, shared):
    nb, seq, _ = x.shape
    assert nb % sb == 0 and seq % ls == 0 and sb * ls == TOK_TILE and ls % SUBLANES == 0 and ls >= CONV_HIST
    nj = seq // ls
    n_valid = (nb // sb) * nj
    n_steps = n_valid if shared is not None else n_tiles - tile0
    own = lambda t: jnp.minimum(t, n_valid - 1)
    hist_spec = lambda n, width: pl.BlockSpec((sb, n, width), lambda t: (own(t) // nj, 0, 0))
    in_specs = [pl.BlockSpec((sb, ls, D_MODEL), lambda t: (own(t) // nj, own(t) % nj, 0)),
                hist_spec(POOL_HIST, D_POOL), hist_spec(CONV_HIST, D_CONV)]
    in_specs += [_const_spec(w.shape) for w in wts]
    shared_shapes = (
        jax.ShapeDtypeStruct((n_tiles * TOK_TILE, D_MODEL), F32),
        jax.ShapeDtypeStruct((n_tiles * TOK_TILE, ROUTER_COLS), F32),
        jax.ShapeDtypeStruct((n_tiles, TILE_ROWS, D_MODEL), BF16),
        jax.ShapeDtypeStruct((n_tiles, SUBLANES, LANES), I32),
    )
    out_shape = shared_shapes + (jax.ShapeDtypeStruct((nb, POOL_HIST, D_POOL), F32),
                                 jax.ShapeDtypeStruct((nb, CONV_HIST, D_CONV), F32))
    out_specs = (pl.BlockSpec((TOK_TILE, D_MODEL), lambda t: (tile0 + t, 0)),
                 pl.BlockSpec((TOK_TILE, ROUTER_COLS), lambda t: (tile0 + t, 0)),
                 pl.BlockSpec((None, TILE_ROWS, D_MODEL), lambda t: (tile0 + t, 0, 0)),
                 pl.BlockSpec((None, SUBLANES, LANES), lambda t: (tile0 + t, 0, 0)),
                 hist_spec(POOL_HIST, D_POOL), hist_spec(CONV_HIST, D_CONV))
    args = [x, hist_pool, hist_conv, *wts]
    aliases = {}
    if shared is not None:
        in_specs += [pl.BlockSpec(memory_space=pl.ANY)] * len(shared)
        aliases = {len(args) + k: k for k in range(len(shared))}
        args += list(shared)
    rows = sb * ls
    return pl.pallas_call(
        functools.partial(_mixer_kernel, past, sb, ls, nj, n_valid, n_steps),
        grid=(n_steps,),
        in_specs=in_specs,
        out_specs=out_specs,
        out_shape=out_shape,
        input_output_aliases=aliases,
        scratch_shapes=[pltpu.VMEM((sb, POOL_PAD + ls, D_POOL), F32),
                        pltpu.VMEM((sb, CONV_PAD + ls, D_CONV), F32),
                        pltpu.VMEM((SUBLANES - 1, sb, CONV_PAD + ls, D_CONV), F32),
                        pltpu.VMEM((rows, D_MODEL), BF16),
                        pltpu.VMEM((N_GATE_CHUNKS, rows, GATE_CHUNK), F32),
                        pltpu.VMEM((CONV_CHUNK, D_CONV), F32),
                        pltpu.VMEM((rows, D_CONV), BF16)],
        compiler_params=pltpu.CompilerParams(dimension_semantics=("arbitrary",),
                                             vmem_limit_bytes=VMEM_LIMIT),
        name="mixer",
    )(*args)


def _chunk_tables(nblk, n_chunks_max):
    n_tiles = nblk.shape[0]
    e_ids = jnp.arange(N_EXPERTS, dtype=I32)
    per_e = jnp.sum(nblk, axis=0)
    chunks_e = -(-per_e // CHUNK_BLOCKS)
    cend = jnp.cumsum(chunks_e)
    cstart = cend - chunks_e
    n_chunks = cend[-1]
    c = jnp.arange(n_chunks_max, dtype=I32)
    chunk_e = jnp.minimum(jnp.sum((cend[None, :] <= c[:, None]).astype(I32), axis=1), N_EXPERTS - 1)
    is_e = chunk_e[:, None] == e_ids[None, :]
    of_chunk = lambda v: jnp.sum(jnp.where(is_e, v[None, :], 0), axis=1)
    of_chunk_t = lambda a: jnp.sum(jnp.where(is_e[:, None, :], a[None, :, :], 0), axis=2)
    cum_t = jnp.cumsum(nblk, axis=0)
    bstart = jnp.cumsum(nblk, axis=1) - nblk
    k = ((c - of_chunk(cstart)) * CHUNK_BLOCKS)[:, None] + jnp.arange(CHUNK_BLOCKS, dtype=I32)[None, :]
    tile = jnp.sum((of_chunk_t(cum_t)[:, None, :] <= k[:, :, None]).astype(I32), axis=2)
    is_t = tile[:, :, None] == jnp.arange(n_tiles, dtype=I32)[None, None, :]
    shift = of_chunk_t(bstart - (cum_t - nblk))
    blk = k + jnp.sum(jnp.where(is_t, shift[:, None, :], 0), axis=2)
    valid = (k < of_chunk(per_e)[:, None]) & (c[:, None] < n_chunks)
    table = jnp.where(valid, tile * TILE_BLOCKS + blk, -1)
    return table.reshape(-1).astype(I32), chunk_e.astype(I32), n_chunks.reshape(1).astype(I32)


def _expert_kernel(spare0, tab_ref, ce_ref, nch_ref, x_hbm, wg_ref, wu_ref, wd_ref, y_hbm,
                   xin0, xin1, yout0, yout1, wgu_bf, wd_bf, sem_in, sem_out):
    c = pl.program_id(0)
    nc = pl.num_programs(0)
    nch = nch_ref[0]

    zero_block = spare0 + 2 * CHUNK_BLOCKS

    def src_block(chunk, s):
        t = tab_ref[chunk * CHUNK_BLOCKS + s]
        return jnp.where(t < 0, zero_block, t)

    def dst_block(chunk, s):
        t = tab_ref[jnp.maximum(chunk, 0) * CHUNK_BLOCKS + s]
        return jnp.where((t < 0) | (chunk < 0), spare0 + (chunk % 2) * CHUNK_BLOCKS + s, t)

    def in_copy(block, buf, sl, s):
        return pltpu.make_async_copy(x_hbm.at[block], buf.at[pl.ds(s * BLK, BLK), :], sem_in.at[sl])

    def out_copy(block, buf, sl, s):
        return pltpu.make_async_copy(buf.at[pl.ds(s * BLK, BLK), :], y_hbm.at[block], sem_out.at[sl])

    def step(sl):
        xin, xin_o = (xin0, xin1) if sl == 0 else (xin1, xin0)
        yout, yout_o = (yout0, yout1) if sl == 0 else (yout1, yout0)

        for s in range(CHUNK_BLOCKS):
            in_copy(src_block(c, s), xin, sl, s).wait()

        @pl.when(c >= 1)
        def _():
            for s in range(CHUNK_BLOCKS):
                out_copy(dst_block(c - 2, s), yout, sl, s).wait()

        ab = _dot(xin[...], wgu_bf[...])

        nxt = jnp.minimum(c + 1, nc - 1)
        for s in range(CHUNK_BLOCKS):
            in_copy(jnp.where(c + 1 < nch, src_block(nxt, s), zero_block), xin_o, 1 - sl, s).start()
        for s in range(CHUNK_BLOCKS):
            out_copy(dst_block(c - 1, s), yout_o, 1 - sl, s).start()

        a = ab[:, 0:D_EXPERT]
        b = ab[:, D_EXPERT:2 * D_EXPERT]
        act = ((a * jax.nn.sigmoid(a)) * b).astype(BF16)
        yout[...] = _dot(act, wd_bf[...]).astype(BF16)

        @pl.when(c == nch - 1)
        def _():
            for s in range(CHUNK_BLOCKS):
                out_copy(dst_block(c, s), yout, sl, s).start()
            for s in range(CHUNK_BLOCKS):
                in_copy(zero_block, xin_o, 1 - sl, s).wait()
            for s in range(CHUNK_BLOCKS):
                out_copy(dst_block(c - 1, s), yout_o, 1 - sl, s).wait()
            for s in range(CHUNK_BLOCKS):
                out_copy(dst_block(c, s), yout, sl, s).wait()

    @pl.when(c < nch)
    def _():
        @pl.when(c == 0)
        def _():
            yout0[...] = jnp.zeros(yout0.shape, yout0.dtype)
            yout1[...] = jnp.zeros(yout1.shape, yout1.dtype)
            for s in range(CHUNK_BLOCKS):
                in_copy(src_block(0, s), xin0, 0, s).start()

        @pl.when((c == 0) | (ce_ref[c] != ce_ref[jnp.maximum(c - 1, 0)]))
        def _():
            wgu_bf[:, 0:D_EXPERT] = wg_ref[...].astype(BF16)
            wgu_bf[:, D_EXPERT:2 * D_EXPERT] = wu_ref[...].astype(BF16)
            wd_bf[...] = wd_ref[...].astype(BF16)

        @pl.when(c % 2 == 0)
        def _():
            step(0)

        @pl.when(c % 2 == 1)
        def _():
            step(1)


def _experts(xs_blocks, n_used_blocks, table, chunk_e, n_chunks, w_gate, w_up, w_down):
    assert xs_blocks.shape[0] - n_used_blocks > 2 * CHUNK_BLOCKS
    n_steps = chunk_e.shape[0]
    w_spec = lambda shape: pl.BlockSpec((None,) + shape, lambda c, tab, ce, nch: (ce[c], 0, 0))
    return pl.pallas_call(
        functools.partial(_expert_kernel, n_used_blocks),
        grid_spec=pltpu.PrefetchScalarGridSpec(
            num_scalar_prefetch=3,
            grid=(n_steps,),
            in_specs=[pl.BlockSpec(memory_space=pl.ANY),
                      w_spec((D_MODEL, D_EXPERT)), w_spec((D_MODEL, D_EXPERT)), w_spec((D_EXPERT, D_MODEL))],
            out_specs=pl.BlockSpec(memory_space=pl.ANY),
            scratch_shapes=[pltpu.VMEM((CHUNK_ROWS, D_MODEL), BF16),
                            pltpu.VMEM((CHUNK_ROWS, D_MODEL), BF16),
                            pltpu.VMEM((CHUNK_ROWS, D_MODEL), BF16),
                            pltpu.VMEM((CHUNK_ROWS, D_MODEL), BF16),
                            pltpu.VMEM((D_MODEL, 2 * D_EXPERT), BF16),
                            pltpu.VMEM((D_EXPERT, D_MODEL), BF16),
                            pltpu.SemaphoreType.DMA((2,)),
                            pltpu.SemaphoreType.DMA((2,))]),
        out_shape=jax.ShapeDtypeStruct(xs_blocks.shape, xs_blocks.dtype),
        input_output_aliases={3: 0},
        compiler_params=pltpu.CompilerParams(dimension_semantics=("arbitrary",),
                                             vmem_limit_bytes=VMEM_LIMIT),
        name="experts",
    )(table, chunk_e, n_chunks, xs_blocks, w_gate, w_up, w_down)


def _combine_kernel(n_first, y_ref, route_ref, x1_ref, gfin_ref, o1_ref, o2_ref):
    i = pl.program_id(0)
    route = route_ref[...]
    pos1 = route[:, ROUTE_POS1:ROUTE_POS1 + 1].astype(I32)
    pos2 = route[:, ROUTE_POS2:ROUTE_POS2 + 1].astype(I32)
    w1 = route[:, ROUTE_W1:ROUTE_W1 + 1]
    w2 = route[:, ROUTE_W2:ROUTE_W2 + 1]
    r_idx = lax.broadcasted_iota(I32, (TOK_TILE, TILE_ROWS), 1)
    gather = jnp.where(r_idx == pos1, w1, jnp.where(r_idx == pos2, w2, 0.0)).astype(BF16)
    moe = _dot(gather, y_ref[...].reshape(TILE_ROWS, D_MODEL))
    out = _rmsnorm(x1_ref[...] + moe, gfin_ref[...])

    @pl.when(i < n_first)
    def _():
        o1_ref[...] = out

    @pl.when(i >= n_first)
    def _():
        o2_ref[...] = out


def _combine(y_blocks, route, x1, g_final, n_first, n_tiles):
    tok_spec = lambda width: pl.BlockSpec((TOK_TILE, width), lambda i: (i, 0))
    return pl.pallas_call(
        functools.partial(_combine_kernel, n_first),
        grid=(n_tiles,),
        in_specs=[pl.BlockSpec((TILE_BLOCKS, BLK, D_MODEL), lambda i: (i, 0, 0)),
                  tok_spec(ROUTER_COLS), tok_spec(D_MODEL),
                  pl.BlockSpec((1, D_MODEL), lambda i: (0, 0))],
        out_specs=(pl.BlockSpec((TOK_TILE, D_MODEL), lambda i: (jnp.minimum(i, n_first - 1), 0)),
                   pl.BlockSpec((TOK_TILE, D_MODEL), lambda i: (jnp.maximum(i - n_first, 0), 0))),
        out_shape=(jax.ShapeDtypeStruct((n_first * TOK_TILE, D_MODEL), F32),
                   jax.ShapeDtypeStruct(((n_tiles - n_first) * TOK_TILE, D_MODEL), F32)),
        compiler_params=pltpu.CompilerParams(dimension_semantics=("arbitrary",),
                                             vmem_limit_bytes=VMEM_LIMIT),
        name="combine",
    )(y_blocks, route, x1, g_final)


def kernel(x_prompt, x_sample, state_pool, state_conv, g_mix, w_in, w_pool_grp, pool_scale, w_pool_out, w_dw, b_dw, ln_g, ln_b, w_conv_out, w_out, g_ffn, w_rg, b_rg, w_re, b_re, w_gate, w_up, w_down, g_final):
    assert g_mix.shape[0] == 1
    l = 0
    bp, seq_p, _ = x_prompt.shape
    bs, seq_s, _ = x_sample.shape
    tiles_p = bp * seq_p // TOK_TILE
    tiles_s = bs * seq_s // TOK_TILE
    n_tiles = tiles_p + tiles_s

    w_r = jnp.zeros((D_MODEL, ROUTER_COLS), F32)
    w_r = w_r.at[:, :N_EXPERTS].set(w_re[l]).at[:, N_EXPERTS:N_EXPERTS + N_EXPERT_GROUPS].set(w_rg[l])
    b_r = jnp.zeros((1, ROUTER_COLS), F32)
    b_r = b_r.at[0, :N_EXPERTS].set(b_re[l]).at[0, N_EXPERTS:N_EXPERTS + N_EXPERT_GROUPS].set(b_rg[l])
    w_r_hi = w_r.astype(BF16)
    w_r_lo = (w_r - w_r_hi.astype(F32)).astype(BF16)
    row = lambda a: a.reshape(1, -1)
    c_gate = D_POOL + 2 * D_CONV
    w_in_bf = w_in[l].astype(BF16)
    w_gates = w_in_bf[:, c_gate:].reshape(D_MODEL, N_GATE_CHUNKS, GATE_CHUNK).transpose(1, 0, 2)
    wts = (row(g_mix[l]), w_in_bf[:, :c_gate], w_gates, w_pool_grp[l].astype(BF16), row(pool_scale[l]),
           w_pool_out[l].astype(BF16), w_dw[l], row(b_dw[l]), row(ln_g[l]), row(ln_b[l]),
           w_conv_out[l].astype(BF16), w_out[l].astype(BF16), row(g_ffn[l]), w_r_hi, w_r_lo, b_r)

    zp = jnp.zeros((bp, POOL_HIST, D_POOL), F32)
    zc = jnp.zeros((bp, CONV_HIST, D_CONV), F32)
    *shared, npp, ncp = _mixer(x_prompt, zp, zc, 0, 1, TOK_TILE, wts, n_tiles + 1, 0, None)
    x1, route, xs, nblk, nps, ncs = _mixer(x_sample, state_pool[l], state_conv[l], PAST_LEN, bs, seq_s, wts,
                                           n_tiles + 1, tiles_p, shared)

    n_chunks_max = -(-n_tiles * TILE_BLOCKS // CHUNK_BLOCKS) + N_EXPERTS
    table, chunk_e, n_chunks = _chunk_tables(nblk[:n_tiles, 0, :N_EXPERTS], n_chunks_max)
    y_blocks = _experts(xs.reshape((n_tiles + 1) * TILE_BLOCKS, BLK, D_MODEL), n_tiles * TILE_BLOCKS,
                        table, chunk_e, n_chunks,
                        w_gate[l].reshape(N_EXPERTS, D_MODEL, D_EXPERT),
                        w_up[l].reshape(N_EXPERTS, D_MODEL, D_EXPERT),
                        w_down[l].reshape(N_EXPERTS, D_EXPERT, D_MODEL))
    yp, ys = _combine(y_blocks, route, x1, row(g_final), tiles_p, n_tiles)
    return (yp.reshape(bp, seq_p, D_MODEL), ys.reshape(bs, seq_s, D_MODEL),
            npp[None], ncp[None], nps[None], ncs[None])
```

```python
import functools

import jax
import jax.numpy as jnp
from jax import lax
from jax.experimental import pallas as pl
from jax.experimental.pallas import tpu as pltpu

D_MODEL = 1024
D_POOL = 512
N_POOL_GROUPS = 4
POOL_GROUP = D_POOL // N_POOL_GROUPS
POOL_WINDOWS = (2, 4, 8, 16)
POOL_HIST = max(POOL_WINDOWS) - 1
D_CONV = 512
CONV_WIDTH = 31
CONV_HIST = CONV_WIDTH - 1
N_EXPERT_GROUPS = 4
EXPERTS_PER_GROUP = 8
N_EXPERTS = N_EXPERT_GROUPS * EXPERTS_PER_GROUP
TOP_K = 2
D_EXPERT = 256
RMS_EPS = 1e-6
LN_EPS = 1e-5
PAST_LEN = 1024

LANES = 128
SUBLANES = 8
POOL_PAD = 16
CONV_PAD = 32
CONV_CHUNK = 64
ROUTER_COLS = LANES
VMEM_LIMIT = 56 * 1024 * 1024

TOK_TILE = 512
BLK = 2 * SUBLANES
TILE_BLOCKS = (TOP_K * TOK_TILE + N_EXPERTS * (BLK - 1)) // BLK
TILE_ROWS = TILE_BLOCKS * BLK
CHUNK_BLOCKS = 32
CHUNK_ROWS = CHUNK_BLOCKS * BLK
ROUTE_POS1, ROUTE_POS2, ROUTE_W1, ROUTE_W2 = 0, 1, 2, 3

BF16 = jnp.bfloat16
F32 = jnp.float32
I32 = jnp.int32


def _dot(a, b):
    return jnp.dot(a, b, preferred_element_type=F32)


def _rmsnorm(x, g):
    return x * lax.rsqrt(jnp.mean(x * x, axis=-1, keepdims=True) + RMS_EPS) * g


def _routing(logits):
    lane = lax.broadcasted_iota(I32, logits.shape, 1)
    lane_f = lane.astype(F32)
    neg = jnp.float32(-jnp.inf)
    big = jnp.float32(1e9)
    is_grp = (lane >= N_EXPERTS) & (lane < N_EXPERTS + N_EXPERT_GROUPS)
    glog = jnp.where(is_grp, logits, neg)
    gmax = jnp.max(glog, axis=1, keepdims=True)
    gidx = jnp.min(jnp.where(glog == gmax, lane_f, big), axis=1, keepdims=True) - float(N_EXPERTS)
    gsum = jnp.sum(jnp.where(is_grp, jnp.exp(glog - gmax), 0.0), axis=1, keepdims=True)
    p_sel = 1.0 / gsum
    lane_grp = (lane >> 3).astype(F32)
    is_sel = (lane < N_EXPERTS) & (lane_grp == gidx)
    elog = jnp.where(is_sel, logits, neg)
    v1 = jnp.max(elog, axis=1, keepdims=True)
    i1 = jnp.min(jnp.where(elog == v1, lane_f, big), axis=1, keepdims=True)
    elog2 = jnp.where(lane_f == i1, neg, elog)
    v2 = jnp.max(elog2, axis=1, keepdims=True)
    i2 = jnp.min(jnp.where(elog2 == v2, lane_f, big), axis=1, keepdims=True)
    t = jnp.exp(v2 - v1)
    return i1, i2, p_sel / (1.0 + t), p_sel * (t / (1.0 + t))


def _dispatch(h2, i1, i2, w1, w2):
    rows = h2.shape[0]
    lane = lax.broadcasted_iota(I32, (rows, LANES), 1)
    lane_f = lane.astype(F32)
    hit1 = lane_f == i1
    hit2 = lane_f == i2
    sel = jnp.where(hit1 | hit2, 1.0, 0.0)
    earlier = lax.broadcasted_iota(I32, (rows, rows), 1) < lax.broadcasted_iota(I32, (rows, rows), 0)
    rank = _dot(jnp.where(earlier, 1.0, 0.0).astype(BF16), sel.astype(BF16))
    cnt = jnp.sum(sel, axis=0, keepdims=True)
    nblk = jnp.floor((cnt + float(BLK - 1)) * (1.0 / BLK))
    before = lax.broadcasted_iota(I32, (LANES, LANES), 0) < lax.broadcasted_iota(I32, (LANES, LANES), 1)
    bstart = _dot(jnp.broadcast_to(nblk, (SUBLANES, LANES)).astype(BF16),
                  jnp.where(before, 1.0, 0.0).astype(BF16))[0:1, :]
    pos = float(BLK) * bstart + rank
    pos1 = jnp.sum(jnp.where(hit1, pos, 0.0), axis=1, keepdims=True)
    pos2 = jnp.sum(jnp.where(hit2, pos, 0.0), axis=1, keepdims=True)
    route = jnp.where(lane == ROUTE_POS1, pos1,
                      jnp.where(lane == ROUTE_POS2, pos2,
                                jnp.where(lane == ROUTE_W1, w1, jnp.where(lane == ROUTE_W2, w2, 0.0))))
    route_t = route.T
    p1 = route_t[ROUTE_POS1:ROUTE_POS1 + 1, :].astype(I32)
    p2 = route_t[ROUTE_POS2:ROUTE_POS2 + 1, :].astype(I32)
    r_idx = lax.broadcasted_iota(I32, (TILE_ROWS, rows), 0)
    onehot = jnp.where((r_idx == p1) | (r_idx == p2), 1.0, 0.0).astype(BF16)
    return route, _dot(onehot, h2).astype(BF16), nblk


N_MIXER_IN = 19
N_MIXER_OUT = 6
N_MIXER_SCRATCH = 7
GATE_CHUNK = 2 * LANES
N_GATE_CHUNKS = 2 * D_MODEL // GATE_CHUNK


def _mixer_kernel(past, sb, ls, nj, n_valid, n_steps, *refs):
    t = pl.program_id(0)
    if n_valid == n_steps:
        _mixer_tile(past, sb, ls, t % nj, nj, *refs)
        return
    outs = refs[-(N_MIXER_OUT + N_MIXER_SCRATCH):][:N_MIXER_OUT]

    @pl.when(t < n_valid)
    def _():
        _mixer_tile(past, sb, ls, t % nj, nj, *refs)

    @pl.when(t >= n_valid)
    def _():
        for ref in outs[:4]:
            ref[...] = jnp.zeros(ref.shape, ref.dtype)


def _mixer_tile(past, sb, ls, j, nj, *refs):
    (x_ref, hp_ref, hc_ref, gmix_ref, win_ref, wgate_ref, wgrp_ref, pscale_ref, wpo_ref, wdw_ref, bdw_ref,
     lng_ref, lnb_ref, wco_ref, wout_ref, gffn_ref, wrh_ref, wrl_ref, br_ref) = refs[:N_MIXER_IN]
    x1_ref, route_ref, xs_ref, nblk_ref, npool_ref, nconv_ref = refs[-(N_MIXER_OUT + N_MIXER_SCRATCH):][:N_MIXER_OUT]
    ubuf, vbuf, sbuf, hbuf, gbuf, cbuf, ybuf = refs[-N_MIXER_SCRATCH:]
    rows = sb * ls
    assert rows // CONV_CHUNK == N_GATE_CHUNKS

    @pl.when(j == 0)
    def _():
        ubuf[:, POOL_PAD - POOL_HIST:POOL_PAD, :] = hp_ref[...]
        vbuf[:, CONV_PAD - CONV_HIST:CONV_PAD, :] = hc_ref[...]

    x = x_ref[...].reshape(rows, D_MODEL)
    hbuf[...] = _rmsnorm(x, gmix_ref[...]).astype(BF16)
    u = _dot(hbuf[...], win_ref[:, 0:D_POOL])
    ubuf[:, POOL_PAD:POOL_PAD + ls, :] = u.reshape(sb, ls, D_POOL)
    ga = _dot(hbuf[...], win_ref[:, D_POOL:D_POOL + D_CONV])
    gb = _dot(hbuf[...], win_ref[:, D_POOL + D_CONV:D_POOL + 2 * D_CONV])
    vbuf[:, CONV_PAD:CONV_PAD + ls, :] = (ga * jax.nn.sigmoid(gb)).reshape(sb, ls, D_CONV)
    for i in range(N_GATE_CHUNKS):
        gbuf[i] = _dot(hbuf[...], wgate_ref[i])

    t_idx = lax.broadcasted_iota(I32, (sb, ls, 1), 1)
    frames = (past + 1 + j * ls + t_idx).astype(F32)
    ys = []
    for gi, w in enumerate(POOL_WINDOWS):
        c0, c1 = gi * POOL_GROUP, (gi + 1) * POOL_GROUP
        cur = ubuf[:, POOL_PAD:POOL_PAD + ls, c0:c1]
        s = cur
        for k in range(1, w):
            s = s + ubuf[:, POOL_PAD - k:POOL_PAD - k + ls, c0:c1]
        mean = s / jnp.minimum(jnp.float32(w), frames)
        ys.append(_dot((mean - cur).reshape(rows, POOL_GROUP).astype(BF16), wgrp_ref[gi]))
    yp = (jnp.concatenate(ys, axis=1) * pscale_ref[...]).astype(BF16)
    y_pool = _dot(yp, wpo_ref[...])
    new_pool = ubuf[:, POOL_PAD + ls - POOL_HIST:POOL_PAD + ls, :]
    ubuf[:, POOL_PAD - POOL_HIST:POOL_PAD, :] = new_pool

    first = CONV_PAD - CONV_HIST
    taps = [[o for o in range(first, first + CONV_WIDTH) if o % SUBLANES == r] for r in range(SUBLANES)]
    for r in range(1, SUBLANES):
        span = taps[r][-1] - taps[r][0] + ls
        sbuf[r - 1, :, 0:span, :] = vbuf[:, taps[r][0]:taps[r][0] + span, :]
    nb_c, nt_c = (1, CONV_CHUNK) if ls >= CONV_CHUNK else (CONV_CHUNK // ls, ls)
    n_grp = nt_c // SUBLANES

    def chunk(i, carry):
        if nb_c == 1:
            bsl, t0 = slice(0, 1), i * CONV_CHUNK
        else:
            bsl, t0 = pl.ds(i * nb_c, nb_c), 0
        for c0 in range(0, D_CONV, LANES):
            cs = slice(c0, c0 + LANES)
            accs = [jnp.zeros((nb_c, SUBLANES, LANES), F32) + bdw_ref[:, cs] for _ in range(n_grp)]
            for r in range(SUBLANES):
                ws = [jnp.broadcast_to(wdw_ref[o - first:o - first + 1, cs], (nb_c, SUBLANES, LANES)) for o in taps[r]]
                for q in range(n_grp + len(taps[r]) - 1):
                    if r == 0:
                        val = vbuf[bsl, pl.ds(t0 + taps[0][0] + q * SUBLANES, SUBLANES), cs]
                    else:
                        val = sbuf[r - 1, bsl, pl.ds(t0 + q * SUBLANES, SUBLANES), cs]
                    for a in range(len(taps[r])):
                        if 0 <= q - a < n_grp:
                            accs[q - a] = accs[q - a] + val * ws[a]
            cbuf[:, cs] = jnp.concatenate(accs, axis=1).reshape(CONV_CHUNK, LANES)
        yf = cbuf[...]
        mu = jnp.mean(yf, axis=-1, keepdims=True)
        yc = yf - mu
        var = jnp.mean(yc * yc, axis=-1, keepdims=True)
        yn = yc * lax.rsqrt(var + LN_EPS) * lng_ref[...] + lnb_ref[...]
        ybuf[pl.ds(i * CONV_CHUNK, CONV_CHUNK), :] = (yn * jax.nn.sigmoid(yn)).astype(BF16)
        return carry

    for i in range(N_GATE_CHUNKS):
        chunk(i, 0)
    y_conv = _dot(ybuf[...], wco_ref[...])
    new_conv = vbuf[:, CONV_PAD + ls - CONV_HIST:CONV_PAD + ls, :]
    vbuf[:, CONV_PAD - CONV_HIST:CONV_PAD, :] = new_conv

    @pl.when(j == nj - 1)
    def _():
        npool_ref[...] = new_pool
        nconv_ref[...] = new_conv

    half = N_GATE_CHUNKS // 2
    merged = []
    for c in range(half):
        cs = slice(c * GATE_CHUNK, (c + 1) * GATE_CHUNK)
        merged.append(jax.nn.sigmoid(gbuf[c]) * y_pool[:, cs] + jax.nn.sigmoid(gbuf[half + c]) * y_conv[:, cs])
    x1 = x + _dot(jnp.concatenate(merged, axis=1).astype(BF16), wout_ref[...])
    x1_ref[...] = x1

    h2 = _rmsnorm(x1, gffn_ref[...])
    h2_hi = h2.astype(BF16)
    h2_lo = (h2 - h2_hi.astype(F32)).astype(BF16)
    logits = (_dot(h2_hi, wrh_ref[...]) + (_dot(h2_hi, wrl_ref[...]) + _dot(h2_lo, wrh_ref[...]))) + br_ref[...]
    route, xs, nblk = _dispatch(h2_hi, *_routing(logits))
    route_ref[...] = route
    xs_ref[...] = xs
    nblk_ref[...] = jnp.broadcast_to(nblk, (SUBLANES, LANES)).astype(I32)


def _const_spec(shape):
    nd = len(shape)
    return pl.BlockSpec(shape, lambda t: (0,) * nd, pipeline_mode=pl.Buffered(1))


def _mixer(x, hist_pool, hist_conv, past, sb, ls, wts, n_tiles, tile0, shared):
    nb, seq, _ = x.shape
    assert nb % sb == 0 and seq % ls == 0 and sb * ls == TOK_TILE and ls % SUBLANES == 0 and ls >= CONV_HIST
    nj = seq // ls
    n_valid = (nb // sb) * nj
    n_steps = n_valid if shared is not None else n_tiles - tile0
    own = lambda t: jnp.minimum(t, n_valid - 1)
    hist_spec = lambda n, width: pl.BlockSpec((sb, n, width), lambda t: (own(t) // nj, 0, 0))
    in_specs = [pl.BlockSpec((sb, ls, D_MODEL), lambda t: (own(t) // nj, own(t) % nj, 0)),
                hist_spec(POOL_HIST, D_POOL), hist_spec(CONV_HIST, D_CONV)]
    in_specs += [_const_spec(w.shape) for w in wts]
    shared_shapes = (
        jax.ShapeDtypeStruct((n_tiles * TOK_TILE, D_MODEL), F32),
        jax.ShapeDtypeStruct((n_tiles * TOK_TILE, ROUTER_COLS), F32),
        jax.ShapeDtypeStruct((n_tiles, TILE_ROWS, D_MODEL), BF16),
        jax.ShapeDtypeStruct((n_tiles, SUBLANES, LANES), I32),
    )
    out_shape = shared_shapes + (jax.ShapeDtypeStruct((nb, POOL_HIST, D_POOL), F32),
                                 jax.ShapeDtypeStruct((nb, CONV_HIST, D_CONV), F32))
    out_specs = (pl.BlockSpec((TOK_TILE, D_MODEL), lambda t: (tile0 + t, 0)),
                 pl.BlockSpec((TOK_TILE, ROUTER_COLS), lambda t: (tile0 + t, 0)),
                 pl.BlockSpec((None, TILE_ROWS, D_MODEL), lambda t: (tile0 + t, 0, 0)),
                 pl.BlockSpec((None, SUBLANES, LANES), lambda t: (tile0 + t, 0, 0)),
                 hist_spec(POOL_HIST, D_POOL), hist_spec(CONV_HIST, D_CONV))
    args = [x, hist_pool, hist_conv, *wts]
    aliases = {}
    if shared is not None:
        in_specs += [pl.BlockSpec(memory_space=pl.ANY)] * len(shared)
        aliases = {len(args) + k: k for k in range(len(shared))}
        args += list(shared)
    rows = sb * ls
    return pl.pallas_call(
        functools.partial(_mixer_kernel, past, sb, ls, nj, n_valid, n_steps),
        grid=(n_steps,),
        in_specs=in_specs,
        out_specs=out_specs,
        out_shape=out_shape,
        input_output_aliases=aliases,
        scratch_shapes=[pltpu.VMEM((sb, POOL_PAD + ls, D_POOL), F32),
                        pltpu.VMEM((sb, CONV_PAD + ls, D_CONV), F32),
                        pltpu.VMEM((SUBLANES - 1, sb, CONV_PAD + ls, D_CONV), F32),
                        pltpu.VMEM((rows, D_MODEL), BF16),
                        pltpu.VMEM((N_GATE_CHUNKS, rows, GATE_CHUNK), F32),
                        pltpu.VMEM((CONV_CHUNK, D_CONV), F32),
                        pltpu.VMEM((rows, D_CONV), BF16)],
        compiler_params=pltpu.CompilerParams(dimension_semantics=("arbitrary",),
                                             vmem_limit_bytes=VMEM_LIMIT),
        name="mixer",
    )(*args)


def _chunk_tables(nblk, n_chunks_max):
    n_tiles = nblk.shape[0]
    e_ids = jnp.arange(N_EXPERTS, dtype=I32)
    per_e = jnp.sum(nblk, axis=0)
    chunks_e = -(-per_e // CHUNK_BLOCKS)
    cend = jnp.cumsum(chunks_e)
    cstart = cend - chunks_e
    n_chunks = cend[-1]
    c = jnp.arange(n_chunks_max, dtype=I32)
    chunk_e = jnp.minimum(jnp.sum((cend[None, :] <= c[:, None]).astype(I32), axis=1), N_EXPERTS - 1)
    is_e = chunk_e[:, None] == e_ids[None, :]
    of_chunk = lambda v: jnp.sum(jnp.where(is_e, v[None, :], 0), axis=1)
    of_chunk_t = lambda a: jnp.sum(jnp.where(is_e[:, None, :], a[None, :, :], 0), axis=2)
    cum_t = jnp.cumsum(nblk, axis=0)
    bstart = jnp.cumsum(nblk, axis=1) - nblk
    k = ((c - of_chunk(cstart)) * CHUNK_BLOCKS)[:, None] + jnp.arange(CHUNK_BLOCKS, dtype=I32)[None, :]
    tile = jnp.sum((of_chunk_t(cum_t)[:, None, :] <= k[:, :, None]).astype(I32), axis=2)
    is_t = tile[:, :, None] == jnp.arange(n_tiles, dtype=I32)[None, None, :]
    shift = of_chunk_t(bstart - (cum_t - nblk))
    blk = k + jnp.sum(jnp.where(is_t, shift[:, None, :], 0), axis=2)
    valid = (k < of_chunk(per_e)[:, None]) & (c[:, None] < n_chunks)
    table = jnp.where(valid, tile * TILE_BLOCKS + blk, -1)
    return table.reshape(-1).astype(I32), chunk_e.astype(I32), n_chunks.reshape(1).astype(I32)


def _expert_kernel(spare0, tab_ref, ce_ref, nch_ref, x_hbm, wg_ref, wu_ref, wd_ref, y_hbm,
                   xin0, xin1, yout0, yout1, wgu_bf, wd_bf, sem_in, sem_out):
    c = pl.program_id(0)
    nc = pl.num_programs(0)
    nch = nch_ref[0]

    zero_block = spare0 + 2 * CHUNK_BLOCKS

    def src_block(chunk, s):
        t = tab_ref[chunk * CHUNK_BLOCKS + s]
        return jnp.where(t < 0, zero_block, t)

    def dst_block(chunk, s):
        t = tab_ref[jnp.maximum(chunk, 0) * CHUNK_BLOCKS + s]
        return jnp.where((t < 0) | (chunk < 0), spare0 + (chunk % 2) * CHUNK_BLOCKS + s, t)

    def in_copy(block, buf, sl, s):
        return pltpu.make_async_copy(x_hbm.at[block], buf.at[pl.ds(s * BLK, BLK), :], sem_in.at[sl])

    def out_copy(block, buf, sl, s):
        return pltpu.make_async_copy(buf.at[pl.ds(s * BLK, BLK), :], y_hbm.at[block], sem_out.at[sl])

    def step(sl):
        xin, xin_o = (xin0, xin1) if sl == 0 else (xin1, xin0)
        yout, yout_o = (yout0, yout1) if sl == 0 else (yout1, yout0)

        nxt = jnp.minimum(c + 1, nc - 1)
        for s in range(CHUNK_BLOCKS):
            in_copy(jnp.where(c + 1 < nch, src_block(nxt, s), zero_block), xin_o, 1 - sl, s).start()
        for s in range(CHUNK_BLOCKS):
            in_copy(src_block(c, s), xin, sl, s).wait()

        ab = _dot(xin[...], wgu_bf[...])

        for s in range(CHUNK_BLOCKS):
            out_copy(dst_block(c - 1, s), yout_o, 1 - sl, s).start()

        a = ab[:, 0:D_EXPERT]
        b = ab[:, D_EXPERT:2 * D_EXPERT]
        act = ((a * jax.nn.sigmoid(a)) * b).astype(BF16)

        for s in range(CHUNK_BLOCKS):
            out_copy(dst_block(c - 2, s), yout, sl, s).wait()
        yout[...] = _dot(act, wd_bf[...]).astype(BF16)

        @pl.when(c == nch - 1)
        def _():
            for s in range(CHUNK_BLOCKS):
                out_copy(dst_block(c, s), yout, sl, s).start()
            for s in range(CHUNK_BLOCKS):
                in_copy(zero_block, xin_o, 1 - sl, s).wait()
            for s in range(CHUNK_BLOCKS):
                out_copy(dst_block(c - 1, s), yout_o, 1 - sl, s).wait()
            for s in range(CHUNK_BLOCKS):
                out_copy(dst_block(c, s), yout, sl, s).wait()

    @pl.when(c < nch)
    def _():
        @pl.when(c == 0)
        def _():
            yout0[...] = jnp.zeros(yout0.shape, yout0.dtype)
            yout1[...] = jnp.zeros(yout1.shape, yout1.dtype)
            for s in range(CHUNK_BLOCKS):
                out_copy(dst_block(c - 2, s), yout0, 0, s).start()
            for s in range(CHUNK_BLOCKS):
                in_copy(src_block(0, s), xin0, 0, s).start()

        @pl.when((c == 0) | (ce_ref[c] != ce_ref[jnp.maximum(c - 1, 0)]))
        def _():
            wgu_bf[:, 0:D_EXPERT] = wg_ref[...].astype(BF16)
            wgu_bf[:, D_EXPERT:2 * D_EXPERT] = wu_ref[...].astype(BF16)
            wd_bf[...] = wd_ref[...].astype(BF16)

        @pl.when(c % 2 == 0)
        def _():
            step(0)

        @pl.when(c % 2 == 1)
        def _():
            step(1)


def _experts(xs_blocks, n_used_blocks, table, chunk_e, n_chunks, w_gate, w_up, w_down):
    assert xs_blocks.shape[0] - n_used_blocks > 2 * CHUNK_BLOCKS
    n_steps = chunk_e.shape[0]
    w_spec = lambda shape: pl.BlockSpec((None,) + shape, lambda c, tab, ce, nch: (ce[c], 0, 0))
    return pl.pallas_call(
        functools.partial(_expert_kernel, n_used_blocks),
        grid_spec=pltpu.PrefetchScalarGridSpec(
            num_scalar_prefetch=3,
            grid=(n_steps,),
            in_specs=[pl.BlockSpec(memory_space=pl.ANY),
                      w_spec((D_MODEL, D_EXPERT)), w_spec((D_MODEL, D_EXPERT)), w_spec((D_EXPERT, D_MODEL))],
            out_specs=pl.BlockSpec(memory_space=pl.ANY),
            scratch_shapes=[pltpu.VMEM((CHUNK_ROWS, D_MODEL), BF16),
                            pltpu.VMEM((CHUNK_ROWS, D_MODEL), BF16),
                            pltpu.VMEM((CHUNK_ROWS, D_MODEL), BF16),
                            pltpu.VMEM((CHUNK_ROWS, D_MODEL), BF16),
                            pltpu.VMEM((D_MODEL, 2 * D_EXPERT), BF16),
                            pltpu.VMEM((D_EXPERT, D_MODEL), BF16),
                            pltpu.SemaphoreType.DMA((2,)),
                            pltpu.SemaphoreType.DMA((2,))]),
        out_shape=jax.ShapeDtypeStruct(xs_blocks.shape, xs_blocks.dtype),
        input_output_aliases={3: 0},
        compiler_params=pltpu.CompilerParams(dimension_semantics=("arbitrary",),
                                             vmem_limit_bytes=VMEM_LIMIT),
        name="experts",
    )(table, chunk_e, n_chunks, xs_blocks, w_gate, w_up, w_down)


def _combine_kernel(n_first, y_ref, route_ref, x1_ref, gfin_ref, o1_ref, o2_ref):
    i = pl.program_id(0)
    route = route_ref[...]
    pos1 = route[:, ROUTE_POS1:ROUTE_POS1 + 1].astype(I32)
    pos2 = route[:, ROUTE_POS2:ROUTE_POS2 + 1].astype(I32)
    w1 = route[:, ROUTE_W1:ROUTE_W1 + 1]
    w2 = route[:, ROUTE_W2:ROUTE_W2 + 1]
    r_idx = lax.broadcasted_iota(I32, (TOK_TILE, TILE_ROWS), 1)
    gather = jnp.where(r_idx == pos1, w1, jnp.where(r_idx == pos2, w2, 0.0)).astype(BF16)
    moe = _dot(gather, y_ref[...].reshape(TILE_ROWS, D_MODEL))
    out = _rmsnorm(x1_ref[...] + moe, gfin_ref[...])

    @pl.when(i < n_first)
    def _():
        o1_ref[...] = out

    @pl.when(i >= n_first)
    def _():
        o2_ref[...] = out


def _combine(y_blocks, route, x1, g_final, n_first, n_tiles):
    tok_spec = lambda width: pl.BlockSpec((TOK_TILE, width), lambda i: (i, 0))
    return pl.pallas_call(
        functools.partial(_combine_kernel, n_first),
        grid=(n_tiles,),
        in_specs=[pl.BlockSpec((TILE_BLOCKS, BLK, D_MODEL), lambda i: (i, 0, 0)),
                  tok_spec(ROUTER_COLS), tok_spec(D_MODEL),
                  pl.BlockSpec((1, D_MODEL), lambda i: (0, 0))],
        out_specs=(pl.BlockSpec((TOK_TILE, D_MODEL), lambda i: (jnp.minimum(i, n_first - 1), 0)),
                   pl.BlockSpec((TOK_TILE, D_MODEL), lambda i: (jnp.maximum(i - n_first, 0), 0))),
        out_shape=(jax.ShapeDtypeStruct((n_first * TOK_TILE, D_MODEL), F32),
                   jax.ShapeDtypeStruct(((n_tiles - n_first) * TOK_TILE, D_MODEL), F32)),
        compiler_params=pltpu.CompilerParams(dimension_semantics=("arbitrary",),
                                             vmem_limit_bytes=VMEM_LIMIT),
        name="combine",
    )(y_blocks, route, x1, g_final)


def kernel(x_prompt, x_sample, state_pool, state_conv, g_mix, w_in, w_pool_grp, pool_scale, w_pool_out, w_dw, b_dw, ln_g, ln_b, w_conv_out, w_out, g_ffn, w_rg, b_rg, w_re, b_re, w_gate, w_up, w_down, g_final):
    assert g_mix.shape[0] == 1
    l = 0
    bp, seq_p, _ = x_prompt.shape
    bs, seq_s, _ = x_sample.shape
    tiles_p = bp * seq_p // TOK_TILE
    tiles_s = bs * seq_s // TOK_TILE
    n_tiles = tiles_p + tiles_s

    w_r = jnp.zeros((D_MODEL, ROUTER_COLS), F32)
    w_r = w_r.at[:, :N_EXPERTS].set(w_re[l]).at[:, N_EXPERTS:N_EXPERTS + N_EXPERT_GROUPS].set(w_rg[l])
    b_r = jnp.zeros((1, ROUTER_COLS), F32)
    b_r = b_r.at[0, :N_EXPERTS].set(b_re[l]).at[0, N_EXPERTS:N_EXPERTS + N_EXPERT_GROUPS].set(b_rg[l])
    w_r_hi = w_r.astype(BF16)
    w_r_lo = (w_r - w_r_hi.astype(F32)).astype(BF16)
    row = lambda a: a.reshape(1, -1)
    c_gate = D_POOL + 2 * D_CONV
    w_in_bf = w_in[l].astype(BF16)
    w_gates = w_in_bf[:, c_gate:].reshape(D_MODEL, N_GATE_CHUNKS, GATE_CHUNK).transpose(1, 0, 2)
    wts = (row(g_mix[l]), w_in_bf[:, :c_gate], w_gates, w_pool_grp[l].astype(BF16), row(pool_scale[l]),
           w_pool_out[l].astype(BF16), w_dw[l], row(b_dw[l]), row(ln_g[l]), row(ln_b[l]),
           w_conv_out[l].astype(BF16), w_out[l].astype(BF16), row(g_ffn[l]), w_r_hi, w_r_lo, b_r)

    zp = jnp.zeros((bp, POOL_HIST, D_POOL), F32)
    zc = jnp.zeros((bp, CONV_HIST, D_CONV), F32)
    *shared, npp, ncp = _mixer(x_prompt, zp, zc, 0, 1, TOK_TILE, wts, n_tiles + 1, 0, None)
    x1, route, xs, nblk, nps, ncs = _mixer(x_sample, state_pool[l], state_conv[l], PAST_LEN, bs, seq_s, wts,
                                           n_tiles + 1, tiles_p, shared)

    n_chunks_max = -(-n_tiles * TILE_BLOCKS // CHUNK_BLOCKS) + N_EXPERTS
    table, chunk_e, n_chunks = _chunk_tables(nblk[:n_tiles, 0, :N_EXPERTS], n_chunks_max)
    y_blocks = _experts(xs.reshape((n_tiles + 1) * TILE_BLOCKS, BLK, D_MODEL), n_tiles * TILE_BLOCKS,
                        table, chunk_e, n_chunks,
                        w_gate[l].reshape(N_EXPERTS, D_MODEL, D_EXPERT),
                        w_up[l].reshape(N_EXPERTS, D_MODEL, D_EXPERT),
                        w_down[l].reshape(N_EXPERTS, D_EXPERT, D_MODEL))
    yp, ys = _combine(y_blocks, route, x1, row(g_final), tiles_p, n_tiles)
    return (yp.reshape(bp, seq_p, D_MODEL), ys.reshape(bs, seq_s, D_MODEL),
            npp[None], ncp[None], nps[None], ncs[None])
```

```python
import functools

import jax
import jax.numpy as jnp
from jax import lax
from jax.experimental import pallas as pl
from jax.experimental.pallas import tpu as pltpu

D_MODEL = 1024
D_POOL = 512
N_POOL_GROUPS = 4
POOL_GROUP = D_POOL // N_POOL_GROUPS
POOL_WINDOWS = (2, 4, 8, 16)
POOL_HIST = max(POOL_WINDOWS) - 1
D_CONV = 512
CONV_WIDTH = 31
CONV_HIST = CONV_WIDTH - 1
N_EXPERT_GROUPS = 4
EXPERTS_PER_GROUP = 8
N_EXPERTS = N_EXPERT_GROUPS * EXPERTS_PER_GROUP
TOP_K = 2
D_EXPERT = 256
RMS_EPS = 1e-6
LN_EPS = 1e-5
PAST_LEN = 1024

LANES = 128
SUBLANES = 8
POOL_PAD = 16
CONV_PAD = 32
CONV_CHUNK = 64
ROUTER_COLS = LANES
VMEM_LIMIT = 56 * 1024 * 1024

TOK_TILE = 512
BLK = 2 * SUBLANES
TILE_BLOCKS = (TOP_K * TOK_TILE + N_EXPERTS * (BLK - 1)) // BLK
TILE_ROWS = TILE_BLOCKS * BLK
CHUNK_BLOCKS = 32
CHUNK_ROWS = CHUNK_BLOCKS * BLK
ROUTE_POS1, ROUTE_POS2, ROUTE_W1, ROUTE_W2 = 0, 1, 2, 3

BF16 = jnp.bfloat16
F32 = jnp.float32
I32 = jnp.int32


def _dot(a, b):
    return jnp.dot(a, b, preferred_element_type=F32)


def _rmsnorm(x, g):
    return x * lax.rsqrt(jnp.mean(x * x, axis=-1, keepdims=True) + RMS_EPS) * g


def _routing(logits):
    lane = lax.broadcasted_iota(I32, logits.shape, 1)
    lane_f = lane.astype(F32)
    neg = jnp.float32(-jnp.inf)
    big = jnp.float32(1e9)
    is_grp = (lane >= N_EXPERTS) & (lane < N_EXPERTS + N_EXPERT_GROUPS)
    glog = jnp.where(is_grp, logits, neg)
    gmax = jnp.max(glog, axis=1, keepdims=True)
    gidx = jnp.min(jnp.where(glog == gmax, lane_f, big), axis=1, keepdims=True) - float(N_EXPERTS)
    gsum = jnp.sum(jnp.where(is_grp, jnp.exp(glog - gmax), 0.0), axis=1, keepdims=True)
    p_sel = 1.0 / gsum
    lane_grp = (lane >> 3).astype(F32)
    is_sel = (lane < N_EXPERTS) & (lane_grp == gidx)
    elog = jnp.where(is_sel, logits, neg)
    v1 = jnp.max(elog, axis=1, keepdims=True)
    i1 = jnp.min(jnp.where(elog == v1, lane_f, big), axis=1, keepdims=True)
    elog2 = jnp.where(lane_f == i1, neg, elog)
    v2 = jnp.max(elog2, axis=1, keepdims=True)
    i2 = jnp.min(jnp.where(elog2 == v2, lane_f, big), axis=1, keepdims=True)
    t = jnp.exp(v2 - v1)
    return i1, i2, p_sel / (1.0 + t), p_sel * (t / (1.0 + t))


def _dispatch(h2, i1, i2, w1, w2):
    rows = h2.shape[0]
    lane = lax.broadcasted_iota(I32, (rows, LANES), 1)
    lane_f = lane.astype(F32)
    hit1 = lane_f == i1
    hit2 = lane_f == i2
    sel = jnp.where(hit1 | hit2, 1.0, 0.0)
    earlier = lax.broadcasted_iota(I32, (rows, rows), 1) < lax.broadcasted_iota(I32, (rows, rows), 0)
    rank = _dot(jnp.where(earlier, 1.0, 0.0).astype(BF16), sel.astype(BF16))
    cnt = jnp.sum(sel, axis=0, keepdims=True)
    nblk = jnp.floor((cnt + float(BLK - 1)) * (1.0 / BLK))
    before = lax.broadcasted_iota(I32, (LANES, LANES), 0) < lax.broadcasted_iota(I32, (LANES, LANES), 1)
    bstart = _dot(jnp.broadcast_to(nblk, (SUBLANES, LANES)).astype(BF16),
                  jnp.where(before, 1.0, 0.0).astype(BF16))[0:1, :]
    pos = float(BLK) * bstart + rank
    pos1 = jnp.sum(jnp.where(hit1, pos, 0.0), axis=1, keepdims=True)
    pos2 = jnp.sum(jnp.where(hit2, pos, 0.0), axis=1, keepdims=True)
    route = jnp.where(lane == ROUTE_POS1, pos1,
                      jnp.where(lane == ROUTE_POS2, pos2,
                                jnp.where(lane == ROUTE_W1, w1, jnp.where(lane == ROUTE_W2, w2, 0.0))))
    route_t = route.T
    p1 = route_t[ROUTE_POS1:ROUTE_POS1 + 1, :].astype(I32)
    p2 = route_t[ROUTE_POS2:ROUTE_POS2 + 1, :].astype(I32)
    r_idx = lax.broadcasted_iota(I32, (TILE_ROWS, rows), 0)
    onehot = jnp.where((r_idx == p1) | (r_idx == p2), 1.0, 0.0).astype(BF16)
    return route, _dot(onehot, h2).astype(BF16), nblk


N_MIXER_IN = 18
N_MIXER_OUT = 6
N_MIXER_SCRATCH = 7
GATE_CHUNK = 2 * LANES
N_GATE_CHUNKS = 2 * D_MODEL // GATE_CHUNK


def _mixer_kernel(past, sb, ls, nj, n_valid, n_steps, *refs):
    t = pl.program_id(0)
    if n_valid == n_steps:
        _mixer_tile(past, sb, ls, t % nj, nj, *refs)
        return
    outs = refs[-(N_MIXER_OUT + N_MIXER_SCRATCH):][:N_MIXER_OUT]

    @pl.when(t < n_valid)
    def _():
        _mixer_tile(past, sb, ls, t % nj, nj, *refs)

    @pl.when(t >= n_valid)
    def _():
        for ref in outs[:4]:
            ref[...] = jnp.zeros(ref.shape, ref.dtype)


def _mixer_tile(past, sb, ls, j, nj, *refs):
    (x_ref, hp_ref, hc_ref, gmix_ref, win_ref, wgate_ref, wgrp_ref, pscale_ref, wpo_ref, wdw_ref, bdw_ref,
     lng_ref, lnb_ref, wco_ref, wout_ref, gffn_ref, wrc_ref, br_ref) = refs[:N_MIXER_IN]
    x1_ref, route_ref, xs_ref, nblk_ref, npool_ref, nconv_ref = refs[-(N_MIXER_OUT + N_MIXER_SCRATCH):][:N_MIXER_OUT]
    ubuf, vbuf, sbuf, hbuf, gbuf, cbuf, ybuf = refs[-N_MIXER_SCRATCH:]
    rows = sb * ls
    assert rows // CONV_CHUNK == N_GATE_CHUNKS

    @pl.when(j == 0)
    def _():
        ubuf[:, POOL_PAD - POOL_HIST:POOL_PAD, :] = hp_ref[...]
        vbuf[:, CONV_PAD - CONV_HIST:CONV_PAD, :] = hc_ref[...]

    x = x_ref[...].reshape(rows, D_MODEL)
    hbuf[...] = _rmsnorm(x, gmix_ref[...]).astype(BF16)
    u = _dot(hbuf[...], win_ref[:, 0:D_POOL])
    ubuf[:, POOL_PAD:POOL_PAD + ls, :] = u.reshape(sb, ls, D_POOL)
    ga = _dot(hbuf[...], win_ref[:, D_POOL:D_POOL + D_CONV])
    gb = _dot(hbuf[...], win_ref[:, D_POOL + D_CONV:D_POOL + 2 * D_CONV])
    vbuf[:, CONV_PAD:CONV_PAD + ls, :] = (ga * jax.nn.sigmoid(gb)).reshape(sb, ls, D_CONV)
    for i in range(N_GATE_CHUNKS):
        gbuf[i] = _dot(hbuf[...], wgate_ref[i])

    t_idx = lax.broadcasted_iota(I32, (sb, ls, 1), 1)
    frames = (past + 1 + j * ls + t_idx).astype(F32)
    ys = []
    part, width = ubuf[...], 1
    for gi, w in enumerate(POOL_WINDOWS):
        c0, c1 = gi * POOL_GROUP, (gi + 1) * POOL_GROUP
        cur = ubuf[:, POOL_PAD:POOL_PAD + ls, c0:c1]
        while width < w:
            part = part + pltpu.roll(part, width, 1)
            width *= 2
        s = part[:, POOL_PAD:POOL_PAD + ls, 0:POOL_GROUP]
        part = part[:, :, POOL_GROUP:]
        mean = s / jnp.minimum(jnp.float32(w), frames)
        ys.append(_dot((mean - cur).reshape(rows, POOL_GROUP).astype(BF16), wgrp_ref[gi]))
    yp = (jnp.concatenate(ys, axis=1) * pscale_ref[...]).astype(BF16)
    y_pool = _dot(yp, wpo_ref[...])
    new_pool = ubuf[:, POOL_PAD + ls - POOL_HIST:POOL_PAD + ls, :]
    ubuf[:, POOL_PAD - POOL_HIST:POOL_PAD, :] = new_pool

    first = CONV_PAD - CONV_HIST
    taps = [[o for o in range(first, first + CONV_WIDTH) if o % SUBLANES == r] for r in range(SUBLANES)]
    for r in range(1, SUBLANES):
        span = taps[r][-1] - taps[r][0] + ls
        sbuf[r - 1, :, 0:span, :] = vbuf[:, taps[r][0]:taps[r][0] + span, :]
    nb_c, nt_c = (1, CONV_CHUNK) if ls >= CONV_CHUNK else (CONV_CHUNK // ls, ls)
    n_grp = nt_c // SUBLANES

    def chunk(i, carry):
        if nb_c == 1:
            bsl, t0 = slice(0, 1), i * CONV_CHUNK
        else:
            bsl, t0 = pl.ds(i * nb_c, nb_c), 0
        for c0 in range(0, D_CONV, LANES):
            cs = slice(c0, c0 + LANES)
            accs = [jnp.zeros((nb_c, SUBLANES, LANES), F32) + bdw_ref[:, cs] for _ in range(n_grp)]
            for r in range(SUBLANES):
                ws = [jnp.broadcast_to(wdw_ref[o - first:o - first + 1, cs], (nb_c, SUBLANES, LANES)) for o in taps[r]]
                for q in range(n_grp + len(taps[r]) - 1):
                    if r == 0:
                        val = vbuf[bsl, pl.ds(t0 + taps[0][0] + q * SUBLANES, SUBLANES), cs]
                    else:
                        val = sbuf[r - 1, bsl, pl.ds(t0 + q * SUBLANES, SUBLANES), cs]
                    for a in range(len(taps[r])):
                        if 0 <= q - a < n_grp:
                            accs[q - a] = accs[q - a] + val * ws[a]
            cbuf[:, cs] = jnp.concatenate(accs, axis=1).reshape(CONV_CHUNK, LANES)
        yf = cbuf[...]
        mu = jnp.mean(yf, axis=-1, keepdims=True)
        yc = yf - mu
        var = jnp.mean(yc * yc, axis=-1, keepdims=True)
        yn = yc * lax.rsqrt(var + LN_EPS) * lng_ref[...] + lnb_ref[...]
        ybuf[pl.ds(i * CONV_CHUNK, CONV_CHUNK), :] = (yn * jax.nn.sigmoid(yn)).astype(BF16)
        return carry

    for i in range(N_GATE_CHUNKS):
        chunk(i, 0)
    y_conv = _dot(ybuf[...], wco_ref[...])
    new_conv = vbuf[:, CONV_PAD + ls - CONV_HIST:CONV_PAD + ls, :]
    vbuf[:, CONV_PAD - CONV_HIST:CONV_PAD, :] = new_conv

    @pl.when(j == nj - 1)
    def _():
        npool_ref[...] = new_pool
        nconv_ref[...] = new_conv

    half = N_GATE_CHUNKS // 2
    merged = []
    for c in range(half):
        cs = slice(c * GATE_CHUNK, (c + 1) * GATE_CHUNK)
        merged.append(jax.nn.sigmoid(gbuf[c]) * y_pool[:, cs] + jax.nn.sigmoid(gbuf[half + c]) * y_conv[:, cs])
    x1 = x + _dot(jnp.concatenate(merged, axis=1).astype(BF16), wout_ref[...])
    x1_ref[...] = x1

    h2 = _rmsnorm(x1, gffn_ref[...])
    h2_hi = h2.astype(BF16)
    h2_lo = (h2 - h2_hi.astype(F32)).astype(BF16)
    hi = _dot(h2_hi, wrc_ref[...])
    logits = (hi[:, 0:ROUTER_COLS] + (hi[:, ROUTER_COLS:] + _dot(h2_lo, wrc_ref[:, 0:ROUTER_COLS]))) + br_ref[...]
    route, xs, nblk = _dispatch(h2_hi, *_routing(logits))
    route_ref[...] = route
    xs_ref[...] = xs
    nblk_ref[...] = jnp.broadcast_to(nblk, (SUBLANES, LANES)).astype(I32)


def _const_spec(shape):
    nd = len(shape)
    return pl.BlockSpec(shape, lambda t: (0,) * nd, pipeline_mode=pl.Buffered(1))


def _mixer(x, hist_pool, hist_conv, past, sb, ls, wts, n_tiles, tile0, shared):
    nb, seq, _ = x.shape
    assert nb % sb == 0 and seq % ls == 0 and sb * ls == TOK_TILE and ls % SUBLANES == 0 and ls >= CONV_HIST
    nj = seq // ls
    n_valid = (nb // sb) * nj
    n_steps = n_valid if shared is not None else n_tiles - tile0
    own = lambda t: jnp.minimum(t, n_valid - 1)
    hist_spec = lambda n, width: pl.BlockSpec((sb, n, width), lambda t: (own(t) // nj, 0, 0))
    in_specs = [pl.BlockSpec((sb, ls, D_MODEL), lambda t: (own(t) // nj, own(t) % nj, 0)),
                hist_spec(POOL_HIST, D_POOL), hist_spec(CONV_HIST, D_CONV)]
    in_specs += [_const_spec(w.shape) for w in wts]
    shared_shapes = (
        jax.ShapeDtypeStruct((n_tiles * TOK_TILE, D_MODEL), F32),
        jax.ShapeDtypeStruct((n_tiles * TOK_TILE, ROUTER_COLS), F32),
        jax.ShapeDtypeStruct((n_tiles, TILE_ROWS, D_MODEL), BF16),
        jax.ShapeDtypeStruct((n_tiles, SUBLANES, LANES), I32),
    )
    out_shape = shared_shapes + (jax.ShapeDtypeStruct((nb, POOL_HIST, D_POOL), F32),
                                 jax.ShapeDtypeStruct((nb, CONV_HIST, D_CONV), F32))
    out_specs = (pl.BlockSpec((TOK_TILE, D_MODEL), lambda t: (tile0 + t, 0)),
                 pl.BlockSpec((TOK_TILE, ROUTER_COLS), lambda t: (tile0 + t, 0)),
                 pl.BlockSpec((None, TILE_ROWS, D_MODEL), lambda t: (tile0 + t, 0, 0)),
                 pl.BlockSpec((None, SUBLANES, LANES), lambda t: (tile0 + t, 0, 0)),
                 hist_spec(POOL_HIST, D_POOL), hist_spec(CONV_HIST, D_CONV))
    args = [x, hist_pool, hist_conv, *wts]
    aliases = {}
    if shared is not None:
        in_specs += [pl.BlockSpec(memory_space=pl.ANY)] * len(shared)
        aliases = {len(args) + k: k for k in range(len(shared))}
        args += list(shared)
    rows = sb * ls
    return pl.pallas_call(
        functools.partial(_mixer_kernel, past, sb, ls, nj, n_valid, n_steps),
        grid=(n_steps,),
        in_specs=in_specs,
        out_specs=out_specs,
        out_shape=out_shape,
        input_output_aliases=aliases,
        scratch_shapes=[pltpu.VMEM((sb, POOL_PAD + ls, D_POOL), F32),
                        pltpu.VMEM((sb, CONV_PAD + ls, D_CONV), F32),
                        pltpu.VMEM((SUBLANES - 1, sb, CONV_PAD + ls, D_CONV), F32),
                        pltpu.VMEM((rows, D_MODEL), BF16),
                        pltpu.VMEM((N_GATE_CHUNKS, rows, GATE_CHUNK), F32),
                        pltpu.VMEM((CONV_CHUNK, D_CONV), F32),
                        pltpu.VMEM((rows, D_CONV), BF16)],
        compiler_params=pltpu.CompilerParams(dimension_semantics=("arbitrary",),
                                             vmem_limit_bytes=VMEM_LIMIT),
        name="mixer",
    )(*args)


def _chunk_tables(nblk, n_chunks_max):
    n_tiles = nblk.shape[0]
    e_ids = jnp.arange(N_EXPERTS, dtype=I32)
    per_e = jnp.sum(nblk, axis=0)
    chunks_e = -(-per_e // CHUNK_BLOCKS)
    cend = jnp.cumsum(chunks_e)
    cstart = cend - chunks_e
    n_chunks = cend[-1]
    c = jnp.arange(n_chunks_max, dtype=I32)
    chunk_e = jnp.minimum(jnp.sum((cend[None, :] <= c[:, None]).astype(I32), axis=1), N_EXPERTS - 1)
    is_e = chunk_e[:, None] == e_ids[None, :]
    of_chunk = lambda v: jnp.sum(jnp.where(is_e, v[None, :], 0), axis=1)
    of_chunk_t = lambda a: jnp.sum(jnp.where(is_e[:, None, :], a[None, :, :], 0), axis=2)
    cum_t = jnp.cumsum(nblk, axis=0)
    bstart = jnp.cumsum(nblk, axis=1) - nblk
    k = ((c - of_chunk(cstart)) * CHUNK_BLOCKS)[:, None] + jnp.arange(CHUNK_BLOCKS, dtype=I32)[None, :]
    tile = jnp.sum((of_chunk_t(cum_t)[:, None, :] <= k[:, :, None]).astype(I32), axis=2)
    is_t = tile[:, :, None] == jnp.arange(n_tiles, dtype=I32)[None, None, :]
    shift = of_chunk_t(bstart - (cum_t - nblk))
    blk = k + jnp.sum(jnp.where(is_t, shift[:, None, :], 0), axis=2)
    valid = (k < of_chunk(per_e)[:, None]) & (c[:, None] < n_chunks)
    table = jnp.where(valid, tile * TILE_BLOCKS + blk, -1)
    return table.reshape(-1).astype(I32), chunk_e.astype(I32), n_chunks.reshape(1).astype(I32)


def _expert_kernel(spare0, tab_ref, ce_ref, nch_ref, x_hbm, wg_ref, wu_ref, wd_ref, y_hbm,
                   xin0, xin1, yout0, yout1, wgu_bf, wd_bf, sem_in, sem_out):
    c = pl.program_id(0)
    nc = pl.num_programs(0)
    nch = nch_ref[0]

    zero_block = spare0 + 2 * CHUNK_BLOCKS

    def src_block(chunk, s):
        t = tab_ref[chunk * CHUNK_BLOCKS + s]
        return jnp.where(t < 0, zero_block, t)

    def dst_block(chunk, s):
        t = tab_ref[jnp.maximum(chunk, 0) * CHUNK_BLOCKS + s]
        return jnp.where((t < 0) | (chunk < 0), spare0 + (chunk % 2) * CHUNK_BLOCKS + s, t)

    def in_copy(block, buf, sl, s):
        return pltpu.make_async_copy(x_hbm.at[block], buf.at[pl.ds(s * BLK, BLK), :], sem_in.at[sl])

    def out_copy(block, buf, sl, s):
        return pltpu.make_async_copy(buf.at[pl.ds(s * BLK, BLK), :], y_hbm.at[block], sem_out.at[sl])

    def step(sl):
        xin, xin_o = (xin0, xin1) if sl == 0 else (xin1, xin0)
        yout, yout_o = (yout0, yout1) if sl == 0 else (yout1, yout0)

        nxt = jnp.minimum(c + 1, nc - 1)
        for s in range(CHUNK_BLOCKS):
            in_copy(jnp.where(c + 1 < nch, src_block(nxt, s), zero_block), xin_o, 1 - sl, s).start()
        for s in range(CHUNK_BLOCKS):
            in_copy(src_block(c, s), xin, sl, s).wait()

        ab = _dot(xin[...], wgu_bf[...])

        for s in range(CHUNK_BLOCKS):
            out_copy(dst_block(c - 1, s), yout_o, 1 - sl, s).start()

        a = ab[:, 0:D_EXPERT]
        b = ab[:, D_EXPERT:2 * D_EXPERT]
        act = ((a * jax.nn.sigmoid(a)) * b).astype(BF16)

        for s in range(CHUNK_BLOCKS):
            out_copy(dst_block(c - 2, s), yout, sl, s).wait()
        yout[...] = _dot(act, wd_bf[...]).astype(BF16)

        @pl.when(c == nch - 1)
        def _():
            for s in range(CHUNK_BLOCKS):
                out_copy(dst_block(c, s), yout, sl, s).start()
            for s in range(CHUNK_BLOCKS):
                in_copy(zero_block, xin_o, 1 - sl, s).wait()
            for s in range(CHUNK_BLOCKS):
                out_copy(dst_block(c - 1, s), yout_o, 1 - sl, s).wait()
            for s in range(CHUNK_BLOCKS):
                out_copy(dst_block(c, s), yout, sl, s).wait()

    @pl.when(c < nch)
    def _():
        @pl.when(c == 0)
        def _():
            yout0[...] = jnp.zeros(yout0.shape, yout0.dtype)
            yout1[...] = jnp.zeros(yout1.shape, yout1.dtype)
            for s in range(CHUNK_BLOCKS):
                out_copy(dst_block(c - 2, s), yout0, 0, s).start()
            for s in range(CHUNK_BLOCKS):
                in_copy(src_block(0, s), xin0, 0, s).start()

        @pl.when((c == 0) | (ce_ref[c] != ce_ref[jnp.maximum(c - 1, 0)]))
        def _():
            wgu_bf[:, 0:D_EXPERT] = wg_ref[...].astype(BF16)
            wgu_bf[:, D_EXPERT:2 * D_EXPERT] = wu_ref[...].astype(BF16)
            wd_bf[...] = wd_ref[...].astype(BF16)

        @pl.when(c % 2 == 0)
        def _():
            step(0)

        @pl.when(c % 2 == 1)
        def _():
            step(1)


def _experts(xs_blocks, n_used_blocks, table, chunk_e, n_chunks, w_gate, w_up, w_down):
    assert xs_blocks.shape[0] - n_used_blocks > 2 * CHUNK_BLOCKS
    n_steps = chunk_e.shape[0]
    w_spec = lambda shape: pl.BlockSpec((None,) + shape, lambda c, tab, ce, nch: (ce[c], 0, 0))
    return pl.pallas_call(
        functools.partial(_expert_kernel, n_used_blocks),
        grid_spec=pltpu.PrefetchScalarGridSpec(
            num_scalar_prefetch=3,
            grid=(n_steps,),
            in_specs=[pl.BlockSpec(memory_space=pl.ANY),
                      w_spec((D_MODEL, D_EXPERT)), w_spec((D_MODEL, D_EXPERT)), w_spec((D_EXPERT, D_MODEL))],
            out_specs=pl.BlockSpec(memory_space=pl.ANY),
            scratch_shapes=[pltpu.VMEM((CHUNK_ROWS, D_MODEL), BF16),
                            pltpu.VMEM((CHUNK_ROWS, D_MODEL), BF16),
                            pltpu.VMEM((CHUNK_ROWS, D_MODEL), BF16),
                            pltpu.VMEM((CHUNK_ROWS, D_MODEL), BF16),
                            pltpu.VMEM((D_MODEL, 2 * D_EXPERT), BF16),
                            pltpu.VMEM((D_EXPERT, D_MODEL), BF16),
                            pltpu.SemaphoreType.DMA((2,)),
                            pltpu.SemaphoreType.DMA((2,))]),
        out_shape=jax.ShapeDtypeStruct(xs_blocks.shape, xs_blocks.dtype),
        input_output_aliases={3: 0},
        compiler_params=pltpu.CompilerParams(dimension_semantics=("arbitrary",),
                                             vmem_limit_bytes=VMEM_LIMIT),
        name="experts",
    )(table, chunk_e, n_chunks, xs_blocks, w_gate, w_up, w_down)


def _combine_kernel(n_first, y_ref, route_ref, x1_ref, gfin_ref, o1_ref, o2_ref):
    i = pl.program_id(0)
    route = route_ref[...]
    pos1 = route[:, ROUTE_POS1:ROUTE_POS1 + 1].astype(I32)
    pos2 = route[:, ROUTE_POS2:ROUTE_POS2 + 1].astype(I32)
    w1 = route[:, ROUTE_W1:ROUTE_W1 + 1]
    w2 = route[:, ROUTE_W2:ROUTE_W2 + 1]
    r_idx = lax.broadcasted_iota(I32, (TOK_TILE, TILE_ROWS), 1)
    gather = jnp.where(r_idx == pos1, w1, jnp.where(r_idx == pos2, w2, 0.0)).astype(BF16)
    moe = _dot(gather, y_ref[...].reshape(TILE_ROWS, D_MODEL))
    out = _rmsnorm(x1_ref[...] + moe, gfin_ref[...])

    @pl.when(i < n_first)
    def _():
        o1_ref[...] = out

    @pl.when(i >= n_first)
    def _():
        o2_ref[...] = out


def _combine(y_blocks, route, x1, g_final, n_first, n_tiles):
    tok_spec = lambda width: pl.BlockSpec((TOK_TILE, width), lambda i: (i, 0))
    return pl.pallas_call(
        functools.partial(_combine_kernel, n_first),
        grid=(n_tiles,),
        in_specs=[pl.BlockSpec((TILE_BLOCKS, BLK, D_MODEL), lambda i: (i, 0, 0)),
                  tok_spec(ROUTER_COLS), tok_spec(D_MODEL),
                  pl.BlockSpec((1, D_MODEL), lambda i: (0, 0))],
        out_specs=(pl.BlockSpec((TOK_TILE, D_MODEL), lambda i: (jnp.minimum(i, n_first - 1), 0)),
                   pl.BlockSpec((TOK_TILE, D_MODEL), lambda i: (jnp.maximum(i - n_first, 0), 0))),
        out_shape=(jax.ShapeDtypeStruct((n_first * TOK_TILE, D_MODEL), F32),
                   jax.ShapeDtypeStruct(((n_tiles - n_first) * TOK_TILE, D_MODEL), F32)),
        compiler_params=pltpu.CompilerParams(dimension_semantics=("arbitrary",),
                                             vmem_limit_bytes=VMEM_LIMIT),
        name="combine",
    )(y_blocks, route, x1, g_final)


def kernel(x_prompt, x_sample, state_pool, state_conv, g_mix, w_in, w_pool_grp, pool_scale, w_pool_out, w_dw, b_dw, ln_g, ln_b, w_conv_out, w_out, g_ffn, w_rg, b_rg, w_re, b_re, w_gate, w_up, w_down, g_final):
    assert g_mix.shape[0] == 1
    l = 0
    bp, seq_p, _ = x_prompt.shape
    bs, seq_s, _ = x_sample.shape
    tiles_p = bp * seq_p // TOK_TILE
    tiles_s = bs * seq_s // TOK_TILE
    n_tiles = tiles_p + tiles_s

    w_r = jnp.zeros((D_MODEL, ROUTER_COLS), F32)
    w_r = w_r.at[:, :N_EXPERTS].set(w_re[l]).at[:, N_EXPERTS:N_EXPERTS + N_EXPERT_GROUPS].set(w_rg[l])
    b_r = jnp.zeros((1, ROUTER_COLS), F32)
    b_r = b_r.at[0, :N_EXPERTS].set(b_re[l]).at[0, N_EXPERTS:N_EXPERTS + N_EXPERT_GROUPS].set(b_rg[l])
    w_r_hi = w_r.astype(BF16)
    w_r_cat = jnp.concatenate([w_r_hi, (w_r - w_r_hi.astype(F32)).astype(BF16)], axis=1)
    row = lambda a: a.reshape(1, -1)
    c_gate = D_POOL + 2 * D_CONV
    w_in_bf = w_in[l].astype(BF16)
    w_gates = w_in_bf[:, c_gate:].reshape(D_MODEL, N_GATE_CHUNKS, GATE_CHUNK).transpose(1, 0, 2)
    wts = (row(g_mix[l]), w_in_bf[:, :c_gate], w_gates, w_pool_grp[l].astype(BF16), row(pool_scale[l]),
           w_pool_out[l].astype(BF16), w_dw[l], row(b_dw[l]), row(ln_g[l]), row(ln_b[l]),
           w_conv_out[l].astype(BF16), w_out[l].astype(BF16), row(g_ffn[l]), w_r_cat, b_r)

    zp = jnp.zeros((bp, POOL_HIST, D_POOL), F32)
    zc = jnp.zeros((bp, CONV_HIST, D_CONV), F32)
    *shared, npp, ncp = _mixer(x_prompt, zp, zc, 0, 1, TOK_TILE, wts, n_tiles + 1, 0, None)
    x1, route, xs, nblk, nps, ncs = _mixer(x_sample, state_pool[l], state_conv[l], PAST_LEN, bs, seq_s, wts,
                                           n_tiles + 1, tiles_p, shared)

    n_chunks_max = -(-n_tiles * TILE_BLOCKS // CHUNK_BLOCKS) + N_EXPERTS
    table, chunk_e, n_chunks = _chunk_tables(nblk[:n_tiles, 0, :N_EXPERTS], n_chunks_max)
    y_blocks = _experts(xs.reshape((n_tiles + 1) * TILE_BLOCKS, BLK, D_MODEL), n_tiles * TILE_BLOCKS,
                        table, chunk_e, n_chunks,
                        w_gate[l].reshape(N_EXPERTS, D_MODEL, D_EXPERT),
                        w_up[l].reshape(N_EXPERTS, D_MODEL, D_EXPERT),
                        w_down[l].reshape(N_EXPERTS, D_EXPERT, D_MODEL))
    yp, ys = _combine(y_blocks, route, x1, row(g_final), tiles_p, n_tiles)
    return (yp.reshape(bp, seq_p, D_MODEL), ys.reshape(bs, seq_s, D_MODEL),
            npp[None], ncp[None], nps[None], ncs[None])
```

```python
import functools

import jax
import jax.numpy as jnp
from jax import lax
from jax.experimental import pallas as pl
from jax.experimental.pallas import tpu as pltpu

D_MODEL = 1024
D_POOL = 512
N_POOL_GROUPS = 4
POOL_GROUP = D_POOL // N_POOL_GROUPS
POOL_WINDOWS = (2, 4, 8, 16)
POOL_HIST = max(POOL_WINDOWS) - 1
D_CONV = 512
CONV_WIDTH = 31
CONV_HIST = CONV_WIDTH - 1
N_EXPERT_GROUPS = 4
EXPERTS_PER_GROUP = 8
N_EXPERTS = N_EXPERT_GROUPS * EXPERTS_PER_GROUP
TOP_K = 2
D_EXPERT = 256
RMS_EPS = 1e-6
LN_EPS = 1e-5
PAST_LEN = 1024

LANES = 128
SUBLANES = 8
POOL_PAD = 16
CONV_PAD = 32
CONV_CHUNK = 64
ROUTER_COLS = LANES
VMEM_LIMIT = 56 * 1024 * 1024

TOK_TILE = 512
BLK = 2 * SUBLANES
TILE_BLOCKS = (TOP_K * TOK_TILE + N_EXPERTS * (BLK - 1)) // BLK
TILE_ROWS = TILE_BLOCKS * BLK
CHUNK_BLOCKS = 32
CHUNK_ROWS = CHUNK_BLOCKS * BLK
GATHER_PRIORITY, SCATTER_PRIORITY = 0, 1
ROUTE_POS1, ROUTE_POS2, ROUTE_W1, ROUTE_W2 = 0, 1, 2, 3

BF16 = jnp.bfloat16
F32 = jnp.float32
I32 = jnp.int32


def _dot(a, b):
    return jnp.dot(a, b, preferred_element_type=F32)


def _rmsnorm(x, g):
    return x * lax.rsqrt(jnp.mean(x * x, axis=-1, keepdims=True) + RMS_EPS) * g


def _routing(logits):
    lane = lax.broadcasted_iota(I32, logits.shape, 1)
    lane_f = lane.astype(F32)
    neg = jnp.float32(-jnp.inf)
    big = jnp.float32(1e9)
    is_grp = (lane >= N_EXPERTS) & (lane < N_EXPERTS + N_EXPERT_GROUPS)
    glog = jnp.where(is_grp, logits, neg)
    gmax = jnp.max(glog, axis=1, keepdims=True)
    gidx = jnp.min(jnp.where(glog == gmax, lane_f, big), axis=1, keepdims=True) - float(N_EXPERTS)
    gsum = jnp.sum(jnp.where(is_grp, jnp.exp(glog - gmax), 0.0), axis=1, keepdims=True)
    p_sel = 1.0 / gsum
    lane_grp = (lane >> 3).astype(F32)
    is_sel = (lane < N_EXPERTS) & (lane_grp == gidx)
    elog = jnp.where(is_sel, logits, neg)
    v1 = jnp.max(elog, axis=1, keepdims=True)
    i1 = jnp.min(jnp.where(elog == v1, lane_f, big), axis=1, keepdims=True)
    elog2 = jnp.where(lane_f == i1, neg, elog)
    v2 = jnp.max(elog2, axis=1, keepdims=True)
    i2 = jnp.min(jnp.where(elog2 == v2, lane_f, big), axis=1, keepdims=True)
    t = jnp.exp(v2 - v1)
    return i1, i2, p_sel / (1.0 + t), p_sel * (t / (1.0 + t))


def _dispatch(h2, i1, i2, w1, w2):
    rows = h2.shape[0]
    lane = lax.broadcasted_iota(I32, (rows, LANES), 1)
    lane_f = lane.astype(F32)
    hit1 = lane_f == i1
    hit2 = lane_f == i2
    sel = jnp.where(hit1 | hit2, 1.0, 0.0)
    earlier = lax.broadcasted_iota(I32, (rows, rows), 1) < lax.broadcasted_iota(I32, (rows, rows), 0)
    rank = _dot(jnp.where(earlier, 1.0, 0.0).astype(BF16), sel.astype(BF16))
    cnt = jnp.sum(sel, axis=0, keepdims=True)
    nblk = jnp.floor((cnt + float(BLK - 1)) * (1.0 / BLK))
    before = lax.broadcasted_iota(I32, (LANES, LANES), 0) < lax.broadcasted_iota(I32, (LANES, LANES), 1)
    bstart = _dot(jnp.broadcast_to(nblk, (SUBLANES, LANES)).astype(BF16),
                  jnp.where(before, 1.0, 0.0).astype(BF16))[0:1, :]
    pos = float(BLK) * bstart + rank
    pos1 = jnp.sum(jnp.where(hit1, pos, 0.0), axis=1, keepdims=True)
    pos2 = jnp.sum(jnp.where(hit2, pos, 0.0), axis=1, keepdims=True)
    route = jnp.where(lane == ROUTE_POS1, pos1,
                      jnp.where(lane == ROUTE_POS2, pos2,
                                jnp.where(lane == ROUTE_W1, w1, jnp.where(lane == ROUTE_W2, w2, 0.0))))
    route_t = route.T
    p1 = route_t[ROUTE_POS1:ROUTE_POS1 + 1, :].astype(I32)
    p2 = route_t[ROUTE_POS2:ROUTE_POS2 + 1, :].astype(I32)
    r_idx = lax.broadcasted_iota(I32, (TILE_ROWS, rows), 0)
    onehot = jnp.where((r_idx == p1) | (r_idx == p2), 1.0, 0.0).astype(BF16)
    return route, _dot(onehot, h2).astype(BF16), nblk


N_MIXER_IN = 18
N_MIXER_OUT = 6
N_MIXER_SCRATCH = 7
GATE_CHUNK = 2 * LANES
N_GATE_CHUNKS = 2 * D_MODEL // GATE_CHUNK


def _mixer_kernel(past, sb, ls, nj, n_valid, n_steps, *refs):
    t = pl.program_id(0)
    if n_valid == n_steps:
        _mixer_tile(past, sb, ls, t % nj, nj, *refs)
        return
    outs = refs[-(N_MIXER_OUT + N_MIXER_SCRATCH):][:N_MIXER_OUT]

    @pl.when(t < n_valid)
    def _():
        _mixer_tile(past, sb, ls, t % nj, nj, *refs)

    @pl.when(t >= n_valid)
    def _():
        for ref in outs[:4]:
            ref[...] = jnp.zeros(ref.shape, ref.dtype)


def _mixer_tile(past, sb, ls, j, nj, *refs):
    (x_ref, hp_ref, hc_ref, gmix_ref, win_ref, wgate_ref, wgrp_ref, pscale_ref, wpo_ref, wdw_ref, bdw_ref,
     lng_ref, lnb_ref, wco_ref, wout_ref, gffn_ref, wrc_ref, br_ref) = refs[:N_MIXER_IN]
    x1_ref, route_ref, xs_ref, nblk_ref, npool_ref, nconv_ref = refs[-(N_MIXER_OUT + N_MIXER_SCRATCH):][:N_MIXER_OUT]
    ubuf, vbuf, sbuf, hbuf, gbuf, cbuf, ybuf = refs[-N_MIXER_SCRATCH:]
    rows = sb * ls
    assert rows // CONV_CHUNK == N_GATE_CHUNKS

    @pl.when(j == 0)
    def _():
        ubuf[:, POOL_PAD - POOL_HIST:POOL_PAD, :] = hp_ref[...]
        vbuf[:, CONV_PAD - CONV_HIST:CONV_PAD, :] = hc_ref[...]

    x = x_ref[...].reshape(rows, D_MODEL)
    hbuf[...] = _rmsnorm(x, gmix_ref[...]).astype(BF16)
    u = _dot(hbuf[...], win_ref[:, 0:D_POOL])
    ubuf[:, POOL_PAD:POOL_PAD + ls, :] = u.reshape(sb, ls, D_POOL)
    ga = _dot(hbuf[...], win_ref[:, D_POOL:D_POOL + D_CONV])
    gb = _dot(hbuf[...], win_ref[:, D_POOL + D_CONV:D_POOL + 2 * D_CONV])
    vbuf[:, CONV_PAD:CONV_PAD + ls, :] = (ga * jax.nn.sigmoid(gb)).reshape(sb, ls, D_CONV)
    for i in range(N_GATE_CHUNKS):
        gbuf[i] = _dot(hbuf[...], wgate_ref[i])

    t_idx = lax.broadcasted_iota(I32, (sb, ls, 1), 1)
    frames = (past + 1 + j * ls + t_idx).astype(F32)
    ys = []
    part, width = ubuf[...], 1
    for gi, w in enumerate(POOL_WINDOWS):
        c0, c1 = gi * POOL_GROUP, (gi + 1) * POOL_GROUP
        cur = ubuf[:, POOL_PAD:POOL_PAD + ls, c0:c1]
        while width < w:
            part = part + pltpu.roll(part, width, 1)
            width *= 2
        s = part[:, POOL_PAD:POOL_PAD + ls, 0:POOL_GROUP]
        part = part[:, :, POOL_GROUP:]
        mean = s / jnp.minimum(jnp.float32(w), frames)
        ys.append(_dot((mean - cur).reshape(rows, POOL_GROUP).astype(BF16), wgrp_ref[gi]))
    yp = (jnp.concatenate(ys, axis=1) * pscale_ref[...]).astype(BF16)
    y_pool = _dot(yp, wpo_ref[...])
    new_pool = ubuf[:, POOL_PAD + ls - POOL_HIST:POOL_PAD + ls, :]
    ubuf[:, POOL_PAD - POOL_HIST:POOL_PAD, :] = new_pool

    first = CONV_PAD - CONV_HIST
    taps = [[o for o in range(first, first + CONV_WIDTH) if o % SUBLANES == r] for r in range(SUBLANES)]
    for r in range(1, SUBLANES):
        span = taps[r][-1] - taps[r][0] + ls
        sbuf[r - 1, :, 0:span, :] = vbuf[:, taps[r][0]:taps[r][0] + span, :]
    nb_c, nt_c = (1, CONV_CHUNK) if ls >= CONV_CHUNK else (CONV_CHUNK // ls, ls)
    n_grp = nt_c // SUBLANES

    def chunk(i, carry):
        if nb_c == 1:
            bsl, t0 = slice(0, 1), i * CONV_CHUNK
        else:
            bsl, t0 = pl.ds(i * nb_c, nb_c), 0
        for c0 in range(0, D_CONV, LANES):
            cs = slice(c0, c0 + LANES)
            accs = [jnp.zeros((nb_c, SUBLANES, LANES), F32) + bdw_ref[:, cs] for _ in range(n_grp)]
            for r in range(SUBLANES):
                ws = [jnp.broadcast_to(wdw_ref[o - first:o - first + 1, cs], (nb_c, SUBLANES, LANES)) for o in taps[r]]
                for q in range(n_grp + len(taps[r]) - 1):
                    if r == 0:
                        val = vbuf[bsl, pl.ds(t0 + taps[0][0] + q * SUBLANES, SUBLANES), cs]
                    else:
                        val = sbuf[r - 1, bsl, pl.ds(t0 + q * SUBLANES, SUBLANES), cs]
                    for a in range(len(taps[r])):
                        if 0 <= q - a < n_grp:
                            accs[q - a] = accs[q - a] + val * ws[a]
            cbuf[:, cs] = jnp.concatenate(accs, axis=1).reshape(CONV_CHUNK, LANES)
        yf = cbuf[...]
        mu = jnp.mean(yf, axis=-1, keepdims=True)
        yc = yf - mu
        var = jnp.mean(yc * yc, axis=-1, keepdims=True)
        yn = yc * lax.rsqrt(var + LN_EPS) * lng_ref[...] + lnb_ref[...]
        ybuf[pl.ds(i * CONV_CHUNK, CONV_CHUNK), :] = (yn * jax.nn.sigmoid(yn)).astype(BF16)
        return carry

    for i in range(N_GATE_CHUNKS):
        chunk(i, 0)
    y_conv = _dot(ybuf[...], wco_ref[...])
    new_conv = vbuf[:, CONV_PAD + ls - CONV_HIST:CONV_PAD + ls, :]
    vbuf[:, CONV_PAD - CONV_HIST:CONV_PAD, :] = new_conv

    @pl.when(j == nj - 1)
    def _():
        npool_ref[...] = new_pool
        nconv_ref[...] = new_conv

    half = N_GATE_CHUNKS // 2
    merged = []
    for c in range(half):
        cs = slice(c * GATE_CHUNK, (c + 1) * GATE_CHUNK)
        merged.append(jax.nn.sigmoid(gbuf[c]) * y_pool[:, cs] + jax.nn.sigmoid(gbuf[half + c]) * y_conv[:, cs])
    x1 = x + _dot(jnp.concatenate(merged, axis=1).astype(BF16), wout_ref[...])
    x1_ref[...] = x1

    h2 = _rmsnorm(x1, gffn_ref[...])
    h2_hi = h2.astype(BF16)
    h2_lo = (h2 - h2_hi.astype(F32)).astype(BF16)
    hi = _dot(h2_hi, wrc_ref[...])
    logits = (hi[:, 0:ROUTER_COLS] + (hi[:, ROUTER_COLS:] + _dot(h2_lo, wrc_ref[:, 0:ROUTER_COLS]))) + br_ref[...]
    route, xs, nblk = _dispatch(h2_hi, *_routing(logits))
    route_ref[...] = route
    xs_ref[...] = xs
    nblk_ref[...] = jnp.broadcast_to(nblk, (SUBLANES, LANES)).astype(I32)


def _const_spec(shape):
    nd = len(shape)
    return pl.BlockSpec(shape, lambda t: (0,) * nd, pipeline_mode=pl.Buffered(1))


def _mixer(x, hist_pool, hist_conv, past, sb, ls, wts, n_tiles, tile0, shared):
    nb, seq, _ = x.shape
    assert nb % sb == 0 and seq % ls == 0 and sb * ls == TOK_TILE and ls % SUBLANES == 0 and ls >= CONV_HIST
    nj = seq // ls
    n_valid = (nb // sb) * nj
    n_steps = n_valid if shared is not None else n_tiles - tile0
    own = lambda t: jnp.minimum(t, n_valid - 1)
    hist_spec = lambda n, width: pl.BlockSpec((sb, n, width), lambda t: (own(t) // nj, 0, 0))
    in_specs = [pl.BlockSpec((sb, ls, D_MODEL), lambda t: (own(t) // nj, own(t) % nj, 0)),
                hist_spec(POOL_HIST, D_POOL), hist_spec(CONV_HIST, D_CONV)]
    in_specs += [_const_spec(w.shape) for w in wts]
    shared_shapes = (
        jax.ShapeDtypeStruct((n_tiles * TOK_TILE, D_MODEL), F32),
        jax.ShapeDtypeStruct((n_tiles * TOK_TILE, ROUTER_COLS), F32),
        jax.ShapeDtypeStruct((n_tiles, TILE_ROWS, D_MODEL), BF16),
        jax.ShapeDtypeStruct((n_tiles, SUBLANES, LANES), I32),
    )
    out_shape = shared_shapes + (jax.ShapeDtypeStruct((nb, POOL_HIST, D_POOL), F32),
                                 jax.ShapeDtypeStruct((nb, CONV_HIST, D_CONV), F32))
    out_specs = (pl.BlockSpec((TOK_TILE, D_MODEL), lambda t: (tile0 + t, 0)),
                 pl.BlockSpec((TOK_TILE, ROUTER_COLS), lambda t: (tile0 + t, 0)),
                 pl.BlockSpec((None, TILE_ROWS, D_MODEL), lambda t: (tile0 + t, 0, 0)),
                 pl.BlockSpec((None, SUBLANES, LANES), lambda t: (tile0 + t, 0, 0)),
                 hist_spec(POOL_HIST, D_POOL), hist_spec(CONV_HIST, D_CONV))
    args = [x, hist_pool, hist_conv, *wts]
    aliases = {}
    if shared is not None:
        in_specs += [pl.BlockSpec(memory_space=pl.ANY)] * len(shared)
        aliases = {len(args) + k: k for k in range(len(shared))}
        args += list(shared)
    rows = sb * ls
    return pl.pallas_call(
        functools.partial(_mixer_kernel, past, sb, ls, nj, n_valid, n_steps),
        grid=(n_steps,),
        in_specs=in_specs,
        out_specs=out_specs,
        out_shape=out_shape,
        input_output_aliases=aliases,
        scratch_shapes=[pltpu.VMEM((sb, POOL_PAD + ls, D_POOL), F32),
                        pltpu.VMEM((sb, CONV_PAD + ls, D_CONV), F32),
                        pltpu.VMEM((SUBLANES - 1, sb, CONV_PAD + ls, D_CONV), F32),
                        pltpu.VMEM((rows, D_MODEL), BF16),
                        pltpu.VMEM((N_GATE_CHUNKS, rows, GATE_CHUNK), F32),
                        pltpu.VMEM((CONV_CHUNK, D_CONV), F32),
                        pltpu.VMEM((rows, D_CONV), BF16)],
        compiler_params=pltpu.CompilerParams(dimension_semantics=("arbitrary",),
                                             vmem_limit_bytes=VMEM_LIMIT),
        name="mixer",
    )(*args)


def _chunk_tables(nblk, n_chunks_max):
    n_tiles = nblk.shape[0]
    e_ids = jnp.arange(N_EXPERTS, dtype=I32)
    per_e = jnp.sum(nblk, axis=0)
    chunks_e = -(-per_e // CHUNK_BLOCKS)
    cend = jnp.cumsum(chunks_e)
    cstart = cend - chunks_e
    n_chunks = cend[-1]
    c = jnp.arange(n_chunks_max, dtype=I32)
    chunk_e = jnp.minimum(jnp.sum((cend[None, :] <= c[:, None]).astype(I32), axis=1), N_EXPERTS - 1)
    is_e = chunk_e[:, None] == e_ids[None, :]
    of_chunk = lambda v: jnp.sum(jnp.where(is_e, v[None, :], 0), axis=1)
    of_chunk_t = lambda a: jnp.sum(jnp.where(is_e[:, None, :], a[None, :, :], 0), axis=2)
    cum_t = jnp.cumsum(nblk, axis=0)
    bstart = jnp.cumsum(nblk, axis=1) - nblk
    k = ((c - of_chunk(cstart)) * CHUNK_BLOCKS)[:, None] + jnp.arange(CHUNK_BLOCKS, dtype=I32)[None, :]
    tile = jnp.sum((of_chunk_t(cum_t)[:, None, :] <= k[:, :, None]).astype(I32), axis=2)
    is_t = tile[:, :, None] == jnp.arange(n_tiles, dtype=I32)[None, None, :]
    shift = of_chunk_t(bstart - (cum_t - nblk))
    blk = k + jnp.sum(jnp.where(is_t, shift[:, None, :], 0), axis=2)
    valid = (k < of_chunk(per_e)[:, None]) & (c[:, None] < n_chunks)
    table = jnp.where(valid, tile * TILE_BLOCKS + blk, -1)
    return table.reshape(-1).astype(I32), chunk_e.astype(I32), n_chunks.reshape(1).astype(I32)


def _expert_kernel(spare0, tab_ref, ce_ref, nch_ref, x_hbm, wg_ref, wu_ref, wd_ref, y_hbm,
                   xin0, xin1, yout0, yout1, wgu_bf, wd_bf, sem_in, sem_out):
    c = pl.program_id(0)
    nc = pl.num_programs(0)
    nch = nch_ref[0]

    zero_block = spare0 + 2 * CHUNK_BLOCKS

    def src_block(chunk, s):
        t = tab_ref[chunk * CHUNK_BLOCKS + s]
        return jnp.where(t < 0, zero_block, t)

    def dst_block(chunk, s):
        t = tab_ref[jnp.maximum(chunk, 0) * CHUNK_BLOCKS + s]
        return jnp.where((t < 0) | (chunk < 0), spare0 + (chunk % 2) * CHUNK_BLOCKS + s, t)

    def in_copy(block, buf, sl, s):
        return pltpu.make_async_copy(x_hbm.at[block], buf.at[pl.ds(s * BLK, BLK), :], sem_in.at[sl])

    def out_copy(block, buf, sl, s):
        return pltpu.make_async_copy(buf.at[pl.ds(s * BLK, BLK), :], y_hbm.at[block], sem_out.at[sl])

    def step(sl):
        xin, xin_o = (xin0, xin1) if sl == 0 else (xin1, xin0)
        yout, yout_o = (yout0, yout1) if sl == 0 else (yout1, yout0)

        nxt = jnp.minimum(c + 1, nc - 1)
        for s in range(CHUNK_BLOCKS):
            in_copy(jnp.where(c + 1 < nch, src_block(nxt, s), zero_block), xin_o, 1 - sl, s).start(GATHER_PRIORITY)
        for s in range(CHUNK_BLOCKS):
            in_copy(src_block(c, s), xin, sl, s).wait()

        ab = _dot(xin[...], wgu_bf[...])

        for s in range(CHUNK_BLOCKS):
            out_copy(dst_block(c - 1, s), yout_o, 1 - sl, s).start(SCATTER_PRIORITY)

        a = ab[:, 0:D_EXPERT]
        b = ab[:, D_EXPERT:2 * D_EXPERT]
        act = ((a * jax.nn.sigmoid(a)) * b).astype(BF16)

        for s in range(CHUNK_BLOCKS):
            out_copy(dst_block(c - 2, s), yout, sl, s).wait()
        yout[...] = _dot(act, wd_bf[...]).astype(BF16)

        @pl.when(c == nch - 1)
        def _():
            for s in range(CHUNK_BLOCKS):
                out_copy(dst_block(c, s), yout, sl, s).start(SCATTER_PRIORITY)
            for s in range(CHUNK_BLOCKS):
                in_copy(zero_block, xin_o, 1 - sl, s).wait()
            for s in range(CHUNK_BLOCKS):
                out_copy(dst_block(c - 1, s), yout_o, 1 - sl, s).wait()
            for s in range(CHUNK_BLOCKS):
                out_copy(dst_block(c, s), yout, sl, s).wait()

    @pl.when(c < nch)
    def _():
        @pl.when(c == 0)
        def _():
            yout0[...] = jnp.zeros(yout0.shape, yout0.dtype)
            yout1[...] = jnp.zeros(yout1.shape, yout1.dtype)
            for s in range(CHUNK_BLOCKS):
                out_copy(dst_block(c - 2, s), yout0, 0, s).start(SCATTER_PRIORITY)
            for s in range(CHUNK_BLOCKS):
                in_copy(src_block(0, s), xin0, 0, s).start(GATHER_PRIORITY)

        @pl.when((c == 0) | (ce_ref[c] != ce_ref[jnp.maximum(c - 1, 0)]))
        def _():
            wgu_bf[:, 0:D_EXPERT] = wg_ref[...].astype(BF16)
            wgu_bf[:, D_EXPERT:2 * D_EXPERT] = wu_ref[...].astype(BF16)
            wd_bf[...] = wd_ref[...].astype(BF16)

        @pl.when(c % 2 == 0)
        def _():
            step(0)

        @pl.when(c % 2 == 1)
        def _():
            step(1)


def _experts(xs_blocks, n_used_blocks, table, chunk_e, n_chunks, w_gate, w_up, w_down):
    assert xs_blocks.shape[0] - n_used_blocks > 2 * CHUNK_BLOCKS
    n_steps = chunk_e.shape[0]
    w_spec = lambda shape: pl.BlockSpec((None,) + shape, lambda c, tab, ce, nch: (ce[c], 0, 0))
    return pl.pallas_call(
        functools.partial(_expert_kernel, n_used_blocks),
        grid_spec=pltpu.PrefetchScalarGridSpec(
            num_scalar_prefetch=3,
            grid=(n_steps,),
            in_specs=[pl.BlockSpec(memory_space=pl.ANY),
                      w_spec((D_MODEL, D_EXPERT)), w_spec((D_MODEL, D_EXPERT)), w_spec((D_EXPERT, D_MODEL))],
            out_specs=pl.BlockSpec(memory_space=pl.ANY),
            scratch_shapes=[pltpu.VMEM((CHUNK_ROWS, D_MODEL), BF16),
                            pltpu.VMEM((CHUNK_ROWS, D_MODEL), BF16),
                            pltpu.VMEM((CHUNK_ROWS, D_MODEL), BF16),
                            pltpu.VMEM((CHUNK_ROWS, D_MODEL), BF16),
                            pltpu.VMEM((D_MODEL, 2 * D_EXPERT), BF16),
                            pltpu.VMEM((D_EXPERT, D_MODEL), BF16),
                            pltpu.SemaphoreType.DMA((2,)),
                            pltpu.SemaphoreType.DMA((2,))]),
        out_shape=jax.ShapeDtypeStruct(xs_blocks.shape, xs_blocks.dtype),
        input_output_aliases={3: 0},
        compiler_params=pltpu.CompilerParams(dimension_semantics=("arbitrary",),
                                             vmem_limit_bytes=VMEM_LIMIT),
        name="experts",
    )(table, chunk_e, n_chunks, xs_blocks, w_gate, w_up, w_down)


def _combine_kernel(n_first, y_ref, route_ref, x1_ref, gfin_ref, o1_ref, o2_ref):
    i = pl.program_id(0)
    route = route_ref[...]
    pos1 = route[:, ROUTE_POS1:ROUTE_POS1 + 1].astype(I32)
    pos2 = route[:, ROUTE_POS2:ROUTE_POS2 + 1].astype(I32)
    w1 = route[:, ROUTE_W1:ROUTE_W1 + 1]
    w2 = route[:, ROUTE_W2:ROUTE_W2 + 1]
    r_idx = lax.broadcasted_iota(I32, (TOK_TILE, TILE_ROWS), 1)
    gather = jnp.where(r_idx == pos1, w1, jnp.where(r_idx == pos2, w2, 0.0)).astype(BF16)
    moe = _dot(gather, y_ref[...].reshape(TILE_ROWS, D_MODEL))
    out = _rmsnorm(x1_ref[...] + moe, gfin_ref[...])

    @pl.when(i < n_first)
    def _():
        o1_ref[...] = out

    @pl.when(i >= n_first)
    def _():
        o2_ref[...] = out


def _combine(y_blocks, route, x1, g_final, n_first, n_tiles):
    tok_spec = lambda width: pl.BlockSpec((TOK_TILE, width), lambda i: (i, 0))
    return pl.pallas_call(
        functools.partial(_combine_kernel, n_first),
        grid=(n_tiles,),
        in_specs=[pl.BlockSpec((TILE_BLOCKS, BLK, D_MODEL), lambda i: (i, 0, 0)),
                  tok_spec(ROUTER_COLS), tok_spec(D_MODEL),
                  pl.BlockSpec((1, D_MODEL), lambda i: (0, 0))],
        out_specs=(pl.BlockSpec((TOK_TILE, D_MODEL), lambda i: (jnp.minimum(i, n_first - 1), 0)),
                   pl.BlockSpec((TOK_TILE, D_MODEL), lambda i: (jnp.maximum(i - n_first, 0), 0))),
        out_shape=(jax.ShapeDtypeStruct((n_first * TOK_TILE, D_MODEL), F32),
                   jax.ShapeDtypeStruct(((n_tiles - n_first) * TOK_TILE, D_MODEL), F32)),
        compiler_params=pltpu.CompilerParams(dimension_semantics=("arbitrary",),
                                             vmem_limit_bytes=VMEM_LIMIT),
        name="combine",
    )(y_blocks, route, x1, g_final)


def kernel(x_prompt, x_sample, state_pool, state_conv, g_mix, w_in, w_pool_grp, pool_scale, w_pool_out, w_dw, b_dw, ln_g, ln_b, w_conv_out, w_out, g_ffn, w_rg, b_rg, w_re, b_re, w_gate, w_up, w_down, g_final):
    assert g_mix.shape[0] == 1
    l = 0
    bp, seq_p, _ = x_prompt.shape
    bs, seq_s, _ = x_sample.shape
    tiles_p = bp * seq_p // TOK_TILE
    tiles_s = bs * seq_s // TOK_TILE
    n_tiles = tiles_p + tiles_s

    w_r = jnp.zeros((D_MODEL, ROUTER_COLS), F32)
    w_r = w_r.at[:, :N_EXPERTS].set(w_re[l]).at[:, N_EXPERTS:N_EXPERTS + N_EXPERT_GROUPS].set(w_rg[l])
    b_r = jnp.zeros((1, ROUTER_COLS), F32)
    b_r = b_r.at[0, :N_EXPERTS].set(b_re[l]).at[0, N_EXPERTS:N_EXPERTS + N_EXPERT_GROUPS].set(b_rg[l])
    w_r_hi = w_r.astype(BF16)
    w_r_cat = jnp.concatenate([w_r_hi, (w_r - w_r_hi.astype(F32)).astype(BF16)], axis=1)
    row = lambda a: a.reshape(1, -1)
    c_gate = D_POOL + 2 * D_CONV
    w_in_bf = w_in[l].astype(BF16)
    w_gates = w_in_bf[:, c_gate:].reshape(D_MODEL, N_GATE_CHUNKS, GATE_CHUNK).transpose(1, 0, 2)
    wts = (row(g_mix[l]), w_in_bf[:, :c_gate], w_gates, w_pool_grp[l].astype(BF16), row(pool_scale[l]),
           w_pool_out[l].astype(BF16), w_dw[l], row(b_dw[l]), row(ln_g[l]), row(ln_b[l]),
           w_conv_out[l].astype(BF16), w_out[l].astype(BF16), row(g_ffn[l]), w_r_cat, b_r)

    zp = jnp.zeros((bp, POOL_HIST, D_POOL), F32)
    zc = jnp.zeros((bp, CONV_HIST, D_CONV), F32)
    *shared, npp, ncp = _mixer(x_prompt, zp, zc, 0, 1, TOK_TILE, wts, n_tiles + 1, 0, None)
    x1, route, xs, nblk, nps, ncs = _mixer(x_sample, state_pool[l], state_conv[l], PAST_LEN, bs, seq_s, wts,
                                           n_tiles + 1, tiles_p, shared)

    n_chunks_max = -(-n_tiles * TILE_BLOCKS // CHUNK_BLOCKS) + N_EXPERTS
    table, chunk_e, n_chunks = _chunk_tables(nblk[:n_tiles, 0, :N_EXPERTS], n_chunks_max)
    y_blocks = _experts(xs.reshape((n_tiles + 1) * TILE_BLOCKS, BLK, D_MODEL), n_tiles * TILE_BLOCKS,
                        table, chunk_e, n_chunks,
                        w_gate[l].reshape(N_EXPERTS, D_MODEL, D_EXPERT),
                        w_up[l].reshape(N_EXPERTS, D_MODEL, D_EXPERT),
                        w_down[l].reshape(N_EXPERTS, D_EXPERT, D_MODEL))
    yp, ys = _combine(y_blocks, route, x1, row(g_final), tiles_p, n_tiles)
    return (yp.reshape(bp, seq_p, D_MODEL), ys.reshape(bs, seq_s, D_MODEL),
            npp[None], ncp[None], nps[None], ncs[None])
```

```python
import functools

import jax
import jax.numpy as jnp
from jax import lax
from jax.experimental import pallas as pl
from jax.experimental.pallas import tpu as pltpu

D_MODEL = 1024
D_POOL = 512
N_POOL_GROUPS = 4
POOL_GROUP = D_POOL // N_POOL_GROUPS
POOL_WINDOWS = (2, 4, 8, 16)
POOL_HIST = max(POOL_WINDOWS) - 1
D_CONV = 512
CONV_WIDTH = 31
CONV_HIST = CONV_WIDTH - 1
N_EXPERT_GROUPS = 4
EXPERTS_PER_GROUP = 8
N_EXPERTS = N_EXPERT_GROUPS * EXPERTS_PER_GROUP
TOP_K = 2
D_EXPERT = 256
RMS_EPS = 1e-6
LN_EPS = 1e-5
PAST_LEN = 1024

LANES = 128
SUBLANES = 8
POOL_PAD = 16
CONV_PAD = 32
CONV_CHUNK = 64
ROUTER_COLS = LANES
VMEM_LIMIT = 56 * 1024 * 1024

TOK_TILE = 512
BLK = 2 * SUBLANES
TILE_BLOCKS = (TOP_K * TOK_TILE + N_EXPERTS * (BLK - 1)) // BLK
TILE_ROWS = TILE_BLOCKS * BLK
CHUNK_BLOCKS = 32
CHUNK_ROWS = CHUNK_BLOCKS * BLK
GATHER_PRIORITY, SCATTER_PRIORITY = 0, 1
ROUTE_POS1, ROUTE_POS2, ROUTE_W1, ROUTE_W2 = 0, 1, 2, 3

BF16 = jnp.bfloat16
F32 = jnp.float32
I32 = jnp.int32


def _dot(a, b):
    return jnp.dot(a, b, preferred_element_type=F32)


def _rmsnorm(x, g):
    return x * lax.rsqrt(jnp.mean(x * x, axis=-1, keepdims=True) + RMS_EPS) * g


def _routing(logits):
    lane = lax.broadcasted_iota(I32, logits.shape, 1)
    lane_f = lane.astype(F32)
    neg = jnp.float32(-jnp.inf)
    big = jnp.float32(1e9)
    is_grp = (lane >= N_EXPERTS) & (lane < N_EXPERTS + N_EXPERT_GROUPS)
    glog = jnp.where(is_grp, logits, neg)
    gmax = jnp.max(glog, axis=1, keepdims=True)
    gidx = jnp.min(jnp.where(glog == gmax, lane_f, big), axis=1, keepdims=True) - float(N_EXPERTS)
    gsum = jnp.sum(jnp.where(is_grp, jnp.exp(glog - gmax), 0.0), axis=1, keepdims=True)
    p_sel = 1.0 / gsum
    lane_grp = (lane >> 3).astype(F32)
    is_sel = (lane < N_EXPERTS) & (lane_grp == gidx)
    elog = jnp.where(is_sel, logits, neg)
    v1 = jnp.max(elog, axis=1, keepdims=True)
    i1 = jnp.min(jnp.where(elog == v1, lane_f, big), axis=1, keepdims=True)
    elog2 = jnp.where(lane_f == i1, neg, elog)
    v2 = jnp.max(elog2, axis=1, keepdims=True)
    i2 = jnp.min(jnp.where(elog2 == v2, lane_f, big), axis=1, keepdims=True)
    t = jnp.exp(v2 - v1)
    return i1, i2, p_sel / (1.0 + t), p_sel * (t / (1.0 + t))


def _dispatch(h2, i1, i2, w1, w2):
    rows = h2.shape[0]
    lane = lax.broadcasted_iota(I32, (rows, LANES), 1)
    lane_f = lane.astype(F32)
    hit1 = lane_f == i1
    hit2 = lane_f == i2
    sel = jnp.where(hit1 | hit2, 1.0, 0.0)
    earlier = lax.broadcasted_iota(I32, (rows, rows), 1) < lax.broadcasted_iota(I32, (rows, rows), 0)
    rank = _dot(jnp.where(earlier, 1.0, 0.0).astype(BF16), sel.astype(BF16))
    cnt = jnp.sum(sel, axis=0, keepdims=True)
    nblk = jnp.floor((cnt + float(BLK - 1)) * (1.0 / BLK))
    before = lax.broadcasted_iota(I32, (LANES, LANES), 0) < lax.broadcasted_iota(I32, (LANES, LANES), 1)
    bstart = _dot(jnp.broadcast_to(nblk, (SUBLANES, LANES)).astype(BF16),
                  jnp.where(before, 1.0, 0.0).astype(BF16))[0:1, :]
    pos = float(BLK) * bstart + rank
    pos1 = jnp.sum(jnp.where(hit1, pos, 0.0), axis=1, keepdims=True)
    pos2 = jnp.sum(jnp.where(hit2, pos, 0.0), axis=1, keepdims=True)
    route = jnp.where(lane == ROUTE_POS1, pos1,
                      jnp.where(lane == ROUTE_POS2, pos2,
                                jnp.where(lane == ROUTE_W1, w1, jnp.where(lane == ROUTE_W2, w2, 0.0))))
    route_t = route.T
    p1 = route_t[ROUTE_POS1:ROUTE_POS1 + 1, :].astype(I32)
    p2 = route_t[ROUTE_POS2:ROUTE_POS2 + 1, :].astype(I32)
    r_idx = lax.broadcasted_iota(I32, (TILE_ROWS, rows), 0)
    onehot = jnp.where((r_idx == p1) | (r_idx == p2), 1.0, 0.0).astype(BF16)
    return route, _dot(onehot, h2).astype(BF16), nblk


N_MIXER_IN = 17
N_MIXER_OUT = 6
N_MIXER_SCRATCH = 6


def _mixer_kernel(past, sb, ls, nj, n_valid, n_steps, *refs):
    t = pl.program_id(0)
    if n_valid == n_steps:
        _mixer_tile(past, sb, ls, t % nj, nj, *refs)
        return
    outs = refs[-(N_MIXER_OUT + N_MIXER_SCRATCH):][:N_MIXER_OUT]

    @pl.when(t < n_valid)
    def _():
        _mixer_tile(past, sb, ls, t % nj, nj, *refs)

    @pl.when(t >= n_valid)
    def _():
        for ref in outs[:4]:
            ref[...] = jnp.zeros(ref.shape, ref.dtype)


def _mixer_tile(past, sb, ls, j, nj, *refs):
    (x_ref, hp_ref, hc_ref, gmix_ref, win_ref, wgrp_ref, pscale_ref, wpo_ref, wdw_ref, bdw_ref,
     lng_ref, lnb_ref, wco_ref, wout_ref, gffn_ref, wrc_ref, br_ref) = refs[:N_MIXER_IN]
    x1_ref, route_ref, xs_ref, nblk_ref, npool_ref, nconv_ref = refs[-(N_MIXER_OUT + N_MIXER_SCRATCH):][:N_MIXER_OUT]
    ubuf, vbuf, sbuf, hbuf, cbuf, ybuf = refs[-N_MIXER_SCRATCH:]
    rows = sb * ls

    @pl.when(j == 0)
    def _():
        ubuf[:, POOL_PAD - POOL_HIST:POOL_PAD, :] = hp_ref[...]
        vbuf[:, CONV_PAD - CONV_HIST:CONV_PAD, :] = hc_ref[...]

    x = x_ref[...].reshape(rows, D_MODEL)
    hbuf[...] = _rmsnorm(x, gmix_ref[...]).astype(BF16)
    u = _dot(hbuf[...], win_ref[:, 0:D_POOL])
    ubuf[:, POOL_PAD:POOL_PAD + ls, :] = u.reshape(sb, ls, D_POOL)
    ga = _dot(hbuf[...], win_ref[:, D_POOL:D_POOL + D_CONV])
    gb = _dot(hbuf[...], win_ref[:, D_POOL + D_CONV:D_POOL + 2 * D_CONV])
    vbuf[:, CONV_PAD:CONV_PAD + ls, :] = (ga * jax.nn.sigmoid(gb)).reshape(sb, ls, D_CONV)
    c_gp = D_POOL + 2 * D_CONV
    gate_pool = _dot(hbuf[...], win_ref[:, c_gp:c_gp + D_MODEL])
    gate_conv = _dot(hbuf[...], win_ref[:, c_gp + D_MODEL:c_gp + 2 * D_MODEL])

    t_idx = lax.broadcasted_iota(I32, (sb, ls, 1), 1)
    frames = (past + 1 + j * ls + t_idx).astype(F32)
    ys = []
    part, width = ubuf[...], 1
    for gi, w in enumerate(POOL_WINDOWS):
        c0, c1 = gi * POOL_GROUP, (gi + 1) * POOL_GROUP
        cur = ubuf[:, POOL_PAD:POOL_PAD + ls, c0:c1]
        while width < w:
            part = part + pltpu.roll(part, width, 1)
            width *= 2
        s = part[:, POOL_PAD:POOL_PAD + ls, 0:POOL_GROUP]
        part = part[:, :, POOL_GROUP:]
        mean = s / jnp.minimum(jnp.float32(w), frames)
        ys.append(_dot((mean - cur).reshape(rows, POOL_GROUP).astype(BF16), wgrp_ref[gi]))
    yp = (jnp.concatenate(ys, axis=1) * pscale_ref[...]).astype(BF16)
    y_pool = _dot(yp, wpo_ref[...])
    new_pool = ubuf[:, POOL_PAD + ls - POOL_HIST:POOL_PAD + ls, :]
    ubuf[:, POOL_PAD - POOL_HIST:POOL_PAD, :] = new_pool

    first = CONV_PAD - CONV_HIST
    taps = [[o for o in range(first, first + CONV_WIDTH) if o % SUBLANES == r] for r in range(SUBLANES)]
    for r in range(1, SUBLANES):
        span = taps[r][-1] - taps[r][0] + ls
        sbuf[r - 1, :, 0:span, :] = vbuf[:, taps[r][0]:taps[r][0] + span, :]
    nb_c, nt_c = (1, CONV_CHUNK) if ls >= CONV_CHUNK else (CONV_CHUNK // ls, ls)
    n_grp = nt_c // SUBLANES

    def chunk(i, carry):
        if nb_c == 1:
            bsl, t0 = slice(0, 1), i * CONV_CHUNK
        else:
            bsl, t0 = pl.ds(i * nb_c, nb_c), 0
        for c0 in range(0, D_CONV, LANES):
            cs = slice(c0, c0 + LANES)
            accs = [jnp.zeros((nb_c, SUBLANES, LANES), F32) + bdw_ref[:, cs] for _ in range(n_grp)]
            for r in range(SUBLANES):
                ws = [jnp.broadcast_to(wdw_ref[o - first:o - first + 1, cs], (nb_c, SUBLANES, LANES)) for o in taps[r]]
                for q in range(n_grp + len(taps[r]) - 1):
                    if r == 0:
                        val = vbuf[bsl, pl.ds(t0 + taps[0][0] + q * SUBLANES, SUBLANES), cs]
                    else:
                        val = sbuf[r - 1, bsl, pl.ds(t0 + q * SUBLANES, SUBLANES), cs]
                    for a in range(len(taps[r])):
                        if 0 <= q - a < n_grp:
                            accs[q - a] = accs[q - a] + val * ws[a]
            cbuf[:, cs] = jnp.concatenate(accs, axis=1).reshape(CONV_CHUNK, LANES)
        yf = cbuf[...]
        mu = jnp.mean(yf, axis=-1, keepdims=True)
        yc = yf - mu
        var = jnp.mean(yc * yc, axis=-1, keepdims=True)
        yn = yc * lax.rsqrt(var + LN_EPS) * lng_ref[...] + lnb_ref[...]
        ybuf[pl.ds(i * CONV_CHUNK, CONV_CHUNK), :] = (yn * jax.nn.sigmoid(yn)).astype(BF16)
        return carry

    for i in range(rows // CONV_CHUNK):
        chunk(i, 0)
    y_conv = _dot(ybuf[...], wco_ref[...])
    new_conv = vbuf[:, CONV_PAD + ls - CONV_HIST:CONV_PAD + ls, :]
    vbuf[:, CONV_PAD - CONV_HIST:CONV_PAD, :] = new_conv

    @pl.when(j == nj - 1)
    def _():
        npool_ref[...] = new_pool
        nconv_ref[...] = new_conv

    merged = jax.nn.sigmoid(gate_pool) * y_pool + jax.nn.sigmoid(gate_conv) * y_conv
    x1 = x + _dot(merged.astype(BF16), wout_ref[...])
    x1_ref[...] = x1

    h2 = _rmsnorm(x1, gffn_ref[...])
    h2_hi = h2.astype(BF16)
    h2_lo = (h2 - h2_hi.astype(F32)).astype(BF16)
    hi = _dot(h2_hi, wrc_ref[...])
    logits = (hi[:, 0:ROUTER_COLS] + (hi[:, ROUTER_COLS:] + _dot(h2_lo, wrc_ref[:, 0:ROUTER_COLS]))) + br_ref[...]
    route, xs, nblk = _dispatch(h2_hi, *_routing(logits))
    route_ref[...] = route
    xs_ref[...] = xs
    nblk_ref[...] = jnp.broadcast_to(nblk, (SUBLANES, LANES)).astype(I32)


def _const_spec(shape):
    nd = len(shape)
    return pl.BlockSpec(shape, lambda t: (0,) * nd, pipeline_mode=pl.Buffered(1))


def _mixer(x, hist_pool, hist_conv, past, sb, ls, wts, n_tiles, tile0, shared):
    nb, seq, _ = x.shape
    assert nb % sb == 0 and seq % ls == 0 and sb * ls == TOK_TILE and ls % SUBLANES == 0 and ls >= CONV_HIST
    nj = seq // ls
    n_valid = (nb // sb) * nj
    n_steps = n_valid if shared is not None else n_tiles - tile0
    own = lambda t: jnp.minimum(t, n_valid - 1)
    hist_spec = lambda n, width: pl.BlockSpec((sb, n, width), lambda t: (own(t) // nj, 0, 0))
    in_specs = [pl.BlockSpec((sb, ls, D_MODEL), lambda t: (own(t) // nj, own(t) % nj, 0)),
                hist_spec(POOL_HIST, D_POOL), hist_spec(CONV_HIST, D_CONV)]
    in_specs += [_const_spec(w.shape) for w in wts]
    shared_shapes = (
        jax.ShapeDtypeStruct((n_tiles * TOK_TILE, D_MODEL), F32),
        jax.ShapeDtypeStruct((n_tiles * TOK_TILE, ROUTER_COLS), F32),
        jax.ShapeDtypeStruct((n_tiles, TILE_ROWS, D_MODEL), BF16),
        jax.ShapeDtypeStruct((n_tiles, SUBLANES, LANES), I32),
    )
    out_shape = shared_shapes + (jax.ShapeDtypeStruct((nb, POOL_HIST, D_POOL), F32),
                                 jax.ShapeDtypeStruct((nb, CONV_HIST, D_CONV), F32))
    out_specs = (pl.BlockSpec((TOK_TILE, D_MODEL), lambda t: (tile0 + t, 0)),
                 pl.BlockSpec((TOK_TILE, ROUTER_COLS), lambda t: (tile0 + t, 0)),
                 pl.BlockSpec((None, TILE_ROWS, D_MODEL), lambda t: (tile0 + t, 0, 0)),
                 pl.BlockSpec((None, SUBLANES, LANES), lambda t: (tile0 + t, 0, 0)),
                 hist_spec(POOL_HIST, D_POOL), hist_spec(CONV_HIST, D_CONV))
    args = [x, hist_pool, hist_conv, *wts]
    aliases = {}
    if shared is not None:
        in_specs += [pl.BlockSpec(memory_space=pl.ANY)] * len(shared)
        aliases = {len(args) + k: k for k in range(len(shared))}
        args += list(shared)
    rows = sb * ls
    return pl.pallas_call(
        functools.partial(_mixer_kernel, past, sb, ls, nj, n_valid, n_steps),
        grid=(n_steps,),
        in_specs=in_specs,
        out_specs=out_specs,
        out_shape=out_shape,
        input_output_aliases=aliases,
        scratch_shapes=[pltpu.VMEM((sb, POOL_PAD + ls, D_POOL), F32),
                        pltpu.VMEM((sb, CONV_PAD + ls, D_CONV), F32),
                        pltpu.VMEM((SUBLANES - 1, sb, CONV_PAD + ls, D_CONV), F32),
                        pltpu.VMEM((rows, D_MODEL), BF16),
                        pltpu.VMEM((CONV_CHUNK, D_CONV), F32),
                        pltpu.VMEM((rows, D_CONV), BF16)],
        compiler_params=pltpu.CompilerParams(dimension_semantics=("arbitrary",),
                                             vmem_limit_bytes=VMEM_LIMIT),
        name="mixer",
    )(*args)


def _chunk_tables(nblk, n_chunks_max):
    n_tiles = nblk.shape[0]
    e_ids = jnp.arange(N_EXPERTS, dtype=I32)
    per_e = jnp.sum(nblk, axis=0)
    chunks_e = -(-per_e // CHUNK_BLOCKS)
    cend = jnp.cumsum(chunks_e)
    cstart = cend - chunks_e
    n_chunks = cend[-1]
    c = jnp.arange(n_chunks_max, dtype=I32)
    chunk_e = jnp.minimum(jnp.sum((cend[None, :] <= c[:, None]).astype(I32), axis=1), N_EXPERTS - 1)
    is_e = chunk_e[:, None] == e_ids[None, :]
    of_chunk = lambda v: jnp.sum(jnp.where(is_e, v[None, :], 0), axis=1)
    of_chunk_t = lambda a: jnp.sum(jnp.where(is_e[:, None, :], a[None, :, :], 0), axis=2)
    cum_t = jnp.cumsum(nblk, axis=0)
    bstart = jnp.cumsum(nblk, axis=1) - nblk
    k = ((c - of_chunk(cstart)) * CHUNK_BLOCKS)[:, None] + jnp.arange(CHUNK_BLOCKS, dtype=I32)[None, :]
    tile = jnp.sum((of_chunk_t(cum_t)[:, None, :] <= k[:, :, None]).astype(I32), axis=2)
    is_t = tile[:, :, None] == jnp.arange(n_tiles, dtype=I32)[None, None, :]
    shift = of_chunk_t(bstart - (cum_t - nblk))
    blk = k + jnp.sum(jnp.where(is_t, shift[:, None, :], 0), axis=2)
    valid = (k < of_chunk(per_e)[:, None]) & (c[:, None] < n_chunks)
    table = jnp.where(valid, tile * TILE_BLOCKS + blk, -1)
    return table.reshape(-1).astype(I32), chunk_e.astype(I32), n_chunks.reshape(1).astype(I32)


def _expert_kernel(spare0, tab_ref, ce_ref, nch_ref, x_hbm, wg_ref, wu_ref, wd_ref, y_hbm,
                   xin0, xin1, yout0, yout1, wgu_bf, wd_bf, sem_in, sem_out):
    c = pl.program_id(0)
    nc = pl.num_programs(0)
    nch = nch_ref[0]

    zero_block = spare0 + 2 * CHUNK_BLOCKS

    def src_block(chunk, s):
        t = tab_ref[chunk * CHUNK_BLOCKS + s]
        return jnp.where(t < 0, zero_block, t)

    def dst_block(chunk, s):
        t = tab_ref[jnp.maximum(chunk, 0) * CHUNK_BLOCKS + s]
        return jnp.where((t < 0) | (chunk < 0), spare0 + (chunk % 2) * CHUNK_BLOCKS + s, t)

    def in_copy(block, buf, sl, s):
        return pltpu.make_async_copy(x_hbm.at[block], buf.at[pl.ds(s * BLK, BLK), :], sem_in.at[sl])

    def out_copy(block, buf, sl, s):
        return pltpu.make_async_copy(buf.at[pl.ds(s * BLK, BLK), :], y_hbm.at[block], sem_out.at[sl])

    def step(sl):
        xin, xin_o = (xin0, xin1) if sl == 0 else (xin1, xin0)
        yout, yout_o = (yout0, yout1) if sl == 0 else (yout1, yout0)

        nxt = jnp.minimum(c + 1, nc - 1)
        for s in range(CHUNK_BLOCKS):
            in_copy(jnp.where(c + 1 < nch, src_block(nxt, s), zero_block), xin_o, 1 - sl, s).start(GATHER_PRIORITY)
        for s in range(CHUNK_BLOCKS):
            in_copy(src_block(c, s), xin, sl, s).wait()

        ab = _dot(xin[...], wgu_bf[...])

        for s in range(CHUNK_BLOCKS):
            out_copy(dst_block(c - 1, s), yout_o, 1 - sl, s).start(SCATTER_PRIORITY)

        a = ab[:, 0:D_EXPERT]
        b = ab[:, D_EXPERT:2 * D_EXPERT]
        act = ((a * jax.nn.sigmoid(a)) * b).astype(BF16)

        for s in range(CHUNK_BLOCKS):
            out_copy(dst_block(c - 2, s), yout, sl, s).wait()
        yout[...] = _dot(act, wd_bf[...]).astype(BF16)

        @pl.when(c == nch - 1)
        def _():
            for s in range(CHUNK_BLOCKS):
                out_copy(dst_block(c, s), yout, sl, s).start(SCATTER_PRIORITY)
            for s in range(CHUNK_BLOCKS):
                in_copy(zero_block, xin_o, 1 - sl, s).wait()
            for s in range(CHUNK_BLOCKS):
                out_copy(dst_block(c - 1, s), yout_o, 1 - sl, s).wait()
            for s in range(CHUNK_BLOCKS):
                out_copy(dst_block(c, s), yout, sl, s).wait()

    @pl.when(c < nch)
    def _():
        @pl.when(c == 0)
        def _():
            yout0[...] = jnp.zeros(yout0.shape, yout0.dtype)
            yout1[...] = jnp.zeros(yout1.shape, yout1.dtype)
            for s in range(CHUNK_BLOCKS):
                out_copy(dst_block(c - 2, s), yout0, 0, s).start(SCATTER_PRIORITY)
            for s in range(CHUNK_BLOCKS):
                in_copy(src_block(0, s), xin0, 0, s).start(GATHER_PRIORITY)

        @pl.when((c == 0) | (ce_ref[c] != ce_ref[jnp.maximum(c - 1, 0)]))
        def _():
            wgu_bf[:, 0:D_EXPERT] = wg_ref[...].astype(BF16)
            wgu_bf[:, D_EXPERT:2 * D_EXPERT] = wu_ref[...].astype(BF16)
            wd_bf[...] = wd_ref[...].astype(BF16)

        @pl.when(c % 2 == 0)
        def _():
            step(0)

        @pl.when(c % 2 == 1)
        def _():
            step(1)


def _experts(xs_blocks, n_used_blocks, table, chunk_e, n_chunks, w_gate, w_up, w_down):
    assert xs_blocks.shape[0] - n_used_blocks > 2 * CHUNK_BLOCKS
    n_steps = chunk_e.shape[0]
    w_spec = lambda shape: pl.BlockSpec((None,) + shape, lambda c, tab, ce, nch: (ce[c], 0, 0))
    return pl.pallas_call(
        functools.partial(_expert_kernel, n_used_blocks),
        grid_spec=pltpu.PrefetchScalarGridSpec(
            num_scalar_prefetch=3,
            grid=(n_steps,),
            in_specs=[pl.BlockSpec(memory_space=pl.ANY),
                      w_spec((D_MODEL, D_EXPERT)), w_spec((D_MODEL, D_EXPERT)), w_spec((D_EXPERT, D_MODEL))],
            out_specs=pl.BlockSpec(memory_space=pl.ANY),
            scratch_shapes=[pltpu.VMEM((CHUNK_ROWS, D_MODEL), BF16),
                            pltpu.VMEM((CHUNK_ROWS, D_MODEL), BF16),
                            pltpu.VMEM((CHUNK_ROWS, D_MODEL), BF16),
                            pltpu.VMEM((CHUNK_ROWS, D_MODEL), BF16),
                            pltpu.VMEM((D_MODEL, 2 * D_EXPERT), BF16),
                            pltpu.VMEM((D_EXPERT, D_MODEL), BF16),
                            pltpu.SemaphoreType.DMA((2,)),
                            pltpu.SemaphoreType.DMA((2,))]),
        out_shape=jax.ShapeDtypeStruct(xs_blocks.shape, xs_blocks.dtype),
        input_output_aliases={3: 0},
        compiler_params=pltpu.CompilerParams(dimension_semantics=("arbitrary",),
                                             vmem_limit_bytes=VMEM_LIMIT),
        name="experts",
    )(table, chunk_e, n_chunks, xs_blocks, w_gate, w_up, w_down)


def _combine_kernel(n_first, y_ref, route_ref, x1_ref, gfin_ref, o1_ref, o2_ref):
    i = pl.program_id(0)
    route = route_ref[...]
    pos1 = route[:, ROUTE_POS1:ROUTE_POS1 + 1].astype(I32)
    pos2 = route[:, ROUTE_POS2:ROUTE_POS2 + 1].astype(I32)
    w1 = route[:, ROUTE_W1:ROUTE_W1 + 1]
    w2 = route[:, ROUTE_W2:ROUTE_W2 + 1]
    r_idx = lax.broadcasted_iota(I32, (TOK_TILE, TILE_ROWS), 1)
    gather = jnp.where(r_idx == pos1, w1, jnp.where(r_idx == pos2, w2, 0.0)).astype(BF16)
    moe = _dot(gather, y_ref[...].reshape(TILE_ROWS, D_MODEL))
    out = _rmsnorm(x1_ref[...] + moe, gfin_ref[...])

    @pl.when(i < n_first)
    def _():
        o1_ref[...] = out

    @pl.when(i >= n_first)
    def _():
        o2_ref[...] = out


def _combine(y_blocks, route, x1, g_final, n_first, n_tiles):
    tok_spec = lambda width: pl.BlockSpec((TOK_TILE, width), lambda i: (i, 0))
    return pl.pallas_call(
        functools.partial(_combine_kernel, n_first),
        grid=(n_tiles,),
        in_specs=[pl.BlockSpec((TILE_BLOCKS, BLK, D_MODEL), lambda i: (i, 0, 0)),
                  tok_spec(ROUTER_COLS), tok_spec(D_MODEL),
                  pl.BlockSpec((1, D_MODEL), lambda i: (0, 0))],
        out_specs=(pl.BlockSpec((TOK_TILE, D_MODEL), lambda i: (jnp.minimum(i, n_first - 1), 0)),
                   pl.BlockSpec((TOK_TILE, D_MODEL), lambda i: (jnp.maximum(i - n_first, 0), 0))),
        out_shape=(jax.ShapeDtypeStruct((n_first * TOK_TILE, D_MODEL), F32),
                   jax.ShapeDtypeStruct(((n_tiles - n_first) * TOK_TILE, D_MODEL), F32)),
        compiler_params=pltpu.CompilerParams(dimension_semantics=("arbitrary",),
                                             vmem_limit_bytes=VMEM_LIMIT),
        name="combine",
    )(y_blocks, route, x1, g_final)


def kernel(x_prompt, x_sample, state_pool, state_conv, g_mix, w_in, w_pool_grp, pool_scale, w_pool_out, w_dw, b_dw, ln_g, ln_b, w_conv_out, w_out, g_ffn, w_rg, b_rg, w_re, b_re, w_gate, w_up, w_down, g_final):
    assert g_mix.shape[0] == 1
    l = 0
    bp, seq_p, _ = x_prompt.shape
    bs, seq_s, _ = x_sample.shape
    tiles_p = bp * seq_p // TOK_TILE
    tiles_s = bs * seq_s // TOK_TILE
    n_tiles = tiles_p + tiles_s

    w_r = jnp.zeros((D_MODEL, ROUTER_COLS), F32)
    w_r = w_r.at[:, :N_EXPERTS].set(w_re[l]).at[:, N_EXPERTS:N_EXPERTS + N_EXPERT_GROUPS].set(w_rg[l])
    b_r = jnp.zeros((1, ROUTER_COLS), F32)
    b_r = b_r.at[0, :N_EXPERTS].set(b_re[l]).at[0, N_EXPERTS:N_EXPERTS + N_EXPERT_GROUPS].set(b_rg[l])
    w_r_hi = w_r.astype(BF16)
    w_r_cat = jnp.concatenate([w_r_hi, (w_r - w_r_hi.astype(F32)).astype(BF16)], axis=1)
    row = lambda a: a.reshape(1, -1)
    wts = (row(g_mix[l]), w_in[l].astype(BF16), w_pool_grp[l].astype(BF16), row(pool_scale[l]),
           w_pool_out[l].astype(BF16), w_dw[l], row(b_dw[l]), row(ln_g[l]), row(ln_b[l]),
           w_conv_out[l].astype(BF16), w_out[l].astype(BF16), row(g_ffn[l]), w_r_cat, b_r)

    zp = jnp.zeros((bp, POOL_HIST, D_POOL), F32)
    zc = jnp.zeros((bp, CONV_HIST, D_CONV), F32)
    *shared, npp, ncp = _mixer(x_prompt, zp, zc, 0, 1, TOK_TILE, wts, n_tiles + 1, 0, None)
    x1, route, xs, nblk, nps, ncs = _mixer(x_sample, state_pool[l], state_conv[l], PAST_LEN, bs, seq_s, wts,
                                           n_tiles + 1, tiles_p, shared)

    n_chunks_max = -(-n_tiles * TILE_BLOCKS // CHUNK_BLOCKS) + N_EXPERTS
    table, chunk_e, n_chunks = _chunk_tables(nblk[:n_tiles, 0, :N_EXPERTS], n_chunks_max)
    y_blocks = _experts(xs.reshape((n_tiles + 1) * TILE_BLOCKS, BLK, D_MODEL), n_tiles * TILE_BLOCKS,
                        table, chunk_e, n_chunks,
                        w_gate[l].reshape(N_EXPERTS, D_MODEL, D_EXPERT),
                        w_up[l].reshape(N_EXPERTS, D_MODEL, D_EXPERT),
                        w_down[l].reshape(N_EXPERTS, D_EXPERT, D_MODEL))
    yp, ys = _combine(y_blocks, route, x1, row(g_final), tiles_p, n_tiles)
    return (yp.reshape(bp, seq_p, D_MODEL), ys.reshape(bs, seq_s, D_MODEL),
            npp[None], ncp[None], nps[None], ncs[None])
```

```python
import functools

import jax
import jax.numpy as jnp
from jax import lax
from jax.experimental import pallas as pl
from jax.experimental.pallas import tpu as pltpu

D_MODEL = 1024
D_POOL = 512
N_POOL_GROUPS = 4
POOL_GROUP = D_POOL // N_POOL_GROUPS
POOL_WINDOWS = (2, 4, 8, 16)
POOL_HIST = max(POOL_WINDOWS) - 1
D_CONV = 512
CONV_WIDTH = 31
CONV_HIST = CONV_WIDTH - 1
N_EXPERT_GROUPS = 4
EXPERTS_PER_GROUP = 8
N_EXPERTS = N_EXPERT_GROUPS * EXPERTS_PER_GROUP
TOP_K = 2
D_EXPERT = 256
RMS_EPS = 1e-6
LN_EPS = 1e-5
PAST_LEN = 1024

LANES = 128
SUBLANES = 8
POOL_PAD = 16
CONV_PAD = 32
CONV_CHUNK = 64
ROUTER_COLS = LANES
VMEM_LIMIT = 56 * 1024 * 1024

TOK_TILE = 512
BLK = 2 * SUBLANES
TILE_BLOCKS = (TOP_K * TOK_TILE + N_EXPERTS * (BLK - 1)) // BLK
TILE_ROWS = TILE_BLOCKS * BLK
CHUNK_BLOCKS = 64
CHUNK_ROWS = CHUNK_BLOCKS * BLK
GATHER_PRIORITY, SCATTER_PRIORITY = 0, 1
ROUTE_POS1, ROUTE_POS2, ROUTE_W1, ROUTE_W2 = 0, 1, 2, 3

BF16 = jnp.bfloat16
F32 = jnp.float32
I32 = jnp.int32


def _dot(a, b):
    return jnp.dot(a, b, preferred_element_type=F32)


def _rmsnorm(x, g):
    return x * lax.rsqrt(jnp.mean(x * x, axis=-1, keepdims=True) + RMS_EPS) * g


def _routing(logits):
    lane = lax.broadcasted_iota(I32, logits.shape, 1)
    lane_f = lane.astype(F32)
    neg = jnp.float32(-jnp.inf)
    big = jnp.float32(1e9)
    is_grp = (lane >= N_EXPERTS) & (lane < N_EXPERTS + N_EXPERT_GROUPS)
    glog = jnp.where(is_grp, logits, neg)
    gmax = jnp.max(glog, axis=1, keepdims=True)
    gidx = jnp.min(jnp.where(glog == gmax, lane_f, big), axis=1, keepdims=True) - float(N_EXPERTS)
    gsum = jnp.sum(jnp.where(is_grp, jnp.exp(glog - gmax), 0.0), axis=1, keepdims=True)
    p_sel = 1.0 / gsum
    lane_grp = (lane >> 3).astype(F32)
    is_sel = (lane < N_EXPERTS) & (lane_grp == gidx)
    elog = jnp.where(is_sel, logits, neg)
    v1 = jnp.max(elog, axis=1, keepdims=True)
    i1 = jnp.min(jnp.where(elog == v1, lane_f, big), axis=1, keepdims=True)
    elog2 = jnp.where(lane_f == i1, neg, elog)
    v2 = jnp.max(elog2, axis=1, keepdims=True)
    i2 = jnp.min(jnp.where(elog2 == v2, lane_f, big), axis=1, keepdims=True)
    t = jnp.exp(v2 - v1)
    return i1, i2, p_sel / (1.0 + t), p_sel * (t / (1.0 + t))


def _dispatch(h2, i1, i2, w1, w2):
    rows = h2.shape[0]
    lane = lax.broadcasted_iota(I32, (rows, LANES), 1)
    lane_f = lane.astype(F32)
    hit1 = lane_f == i1
    hit2 = lane_f == i2
    sel = jnp.where(hit1 | hit2, 1.0, 0.0)
    earlier = lax.broadcasted_iota(I32, (rows, rows), 1) < lax.broadcasted_iota(I32, (rows, rows), 0)
    rank = _dot(jnp.where(earlier, 1.0, 0.0).astype(BF16), sel.astype(BF16))
    cnt = jnp.sum(sel, axis=0, keepdims=True)
    nblk = jnp.floor((cnt + float(BLK - 1)) * (1.0 / BLK))
    before = lax.broadcasted_iota(I32, (LANES, LANES), 0) < lax.broadcasted_iota(I32, (LANES, LANES), 1)
    bstart = _dot(jnp.broadcast_to(nblk, (SUBLANES, LANES)).astype(BF16),
                  jnp.where(before, 1.0, 0.0).astype(BF16))[0:1, :]
    pos = float(BLK) * bstart + rank
    pos1 = jnp.sum(jnp.where(hit1, pos, 0.0), axis=1, keepdims=True)
    pos2 = jnp.sum(jnp.where(hit2, pos, 0.0), axis=1, keepdims=True)
    route = jnp.where(lane == ROUTE_POS1, pos1,
                      jnp.where(lane == ROUTE_POS2, pos2,
                                jnp.where(lane == ROUTE_W1, w1, jnp.where(lane == ROUTE_W2, w2, 0.0))))
    route_t = route.T
    p1 = route_t[ROUTE_POS1:ROUTE_POS1 + 1, :].astype(I32)
    p2 = route_t[ROUTE_POS2:ROUTE_POS2 + 1, :].astype(I32)
    r_idx = lax.broadcasted_iota(I32, (TILE_ROWS, rows), 0)
    onehot = jnp.where((r_idx == p1) | (r_idx == p2), 1.0, 0.0).astype(BF16)
    return route, _dot(onehot, h2).astype(BF16), nblk


N_MIXER_IN = 17
N_MIXER_OUT = 6
N_MIXER_SCRATCH = 6


def _mixer_kernel(past, sb, ls, nj, n_valid, n_steps, *refs):
    t = pl.program_id(0)
    if n_valid == n_steps:
        _mixer_tile(past, sb, ls, t % nj, nj, *refs)
        return
    outs = refs[-(N_MIXER_OUT + N_MIXER_SCRATCH):][:N_MIXER_OUT]

    @pl.when(t < n_valid)
    def _():
        _mixer_tile(past, sb, ls, t % nj, nj, *refs)

    @pl.when(t >= n_valid)
    def _():
        for ref in outs[:4]:
            ref[...] = jnp.zeros(ref.shape, ref.dtype)


def _mixer_tile(past, sb, ls, j, nj, *refs):
    (x_ref, hp_ref, hc_ref, gmix_ref, win_ref, wgrp_ref, pscale_ref, wpo_ref, wdw_ref, bdw_ref,
     lng_ref, lnb_ref, wco_ref, wout_ref, gffn_ref, wrc_ref, br_ref) = refs[:N_MIXER_IN]
    x1_ref, route_ref, xs_ref, nblk_ref, npool_ref, nconv_ref = refs[-(N_MIXER_OUT + N_MIXER_SCRATCH):][:N_MIXER_OUT]
    ubuf, vbuf, sbuf, hbuf, cbuf, ybuf = refs[-N_MIXER_SCRATCH:]
    rows = sb * ls

    @pl.when(j == 0)
    def _():
        ubuf[:, POOL_PAD - POOL_HIST:POOL_PAD, :] = hp_ref[...]
        vbuf[:, CONV_PAD - CONV_HIST:CONV_PAD, :] = hc_ref[...]

    x = x_ref[...].reshape(rows, D_MODEL)
    hbuf[...] = _rmsnorm(x, gmix_ref[...]).astype(BF16)
    u = _dot(hbuf[...], win_ref[:, 0:D_POOL])
    ubuf[:, POOL_PAD:POOL_PAD + ls, :] = u.reshape(sb, ls, D_POOL)
    ga = _dot(hbuf[...], win_ref[:, D_POOL:D_POOL + D_CONV])
    gb = _dot(hbuf[...], win_ref[:, D_POOL + D_CONV:D_POOL + 2 * D_CONV])
    vbuf[:, CONV_PAD:CONV_PAD + ls, :] = (ga * jax.nn.sigmoid(gb)).reshape(sb, ls, D_CONV)
    c_gp = D_POOL + 2 * D_CONV
    gate_pool = _dot(hbuf[...], win_ref[:, c_gp:c_gp + D_MODEL])
    gate_conv = _dot(hbuf[...], win_ref[:, c_gp + D_MODEL:c_gp + 2 * D_MODEL])

    t_idx = lax.broadcasted_iota(I32, (sb, ls, 1), 1)
    frames = (past + 1 + j * ls + t_idx).astype(F32)
    ys = []
    part, width = ubuf[...], 1
    for gi, w in enumerate(POOL_WINDOWS):
        c0, c1 = gi * POOL_GROUP, (gi + 1) * POOL_GROUP
        cur = ubuf[:, POOL_PAD:POOL_PAD + ls, c0:c1]
        while width < w:
            part = part + pltpu.roll(part, width, 1)
            width *= 2
        s = part[:, POOL_PAD:POOL_PAD + ls, 0:POOL_GROUP]
        part = part[:, :, POOL_GROUP:]
        mean = s / jnp.minimum(jnp.float32(w), frames)
        ys.append(_dot((mean - cur).reshape(rows, POOL_GROUP).astype(BF16), wgrp_ref[gi]))
    yp = (jnp.concatenate(ys, axis=1) * pscale_ref[...]).astype(BF16)
    y_pool = _dot(yp, wpo_ref[...])
    new_pool = ubuf[:, POOL_PAD + ls - POOL_HIST:POOL_PAD + ls, :]
    ubuf[:, POOL_PAD - POOL_HIST:POOL_PAD, :] = new_pool

    first = CONV_PAD - CONV_HIST
    taps = [[o for o in range(first, first + CONV_WIDTH) if o % SUBLANES == r] for r in range(SUBLANES)]
    for r in range(1, SUBLANES):
        span = taps[r][-1] - taps[r][0] + ls
        sbuf[r - 1, :, 0:span, :] = vbuf[:, taps[r][0]:taps[r][0] + span, :]
    nb_c, nt_c = (1, CONV_CHUNK) if ls >= CONV_CHUNK else (CONV_CHUNK // ls, ls)
    n_grp = nt_c // SUBLANES

    def chunk(i, carry):
        if nb_c == 1:
            bsl, t0 = slice(0, 1), i * CONV_CHUNK
        else:
            bsl, t0 = pl.ds(i * nb_c, nb_c), 0
        for c0 in range(0, D_CONV, LANES):
            cs = slice(c0, c0 + LANES)
            accs = [jnp.zeros((nb_c, SUBLANES, LANES), F32) + bdw_ref[:, cs] for _ in range(n_grp)]
            for r in range(SUBLANES):
                ws = [jnp.broadcast_to(wdw_ref[o - first:o - first + 1, cs], (nb_c, SUBLANES, LANES)) for o in taps[r]]
                for q in range(n_grp + len(taps[r]) - 1):
                    if r == 0:
                        val = vbuf[bsl, pl.ds(t0 + taps[0][0] + q * SUBLANES, SUBLANES), cs]
                    else:
                        val = sbuf[r - 1, bsl, pl.ds(t0 + q * SUBLANES, SUBLANES), cs]
                    for a in range(len(taps[r])):
                        if 0 <= q - a < n_grp:
                            accs[q - a] = accs[q - a] + val * ws[a]
            cbuf[:, cs] = jnp.concatenate(accs, axis=1).reshape(CONV_CHUNK, LANES)
        yf = cbuf[...]
        mu = jnp.mean(yf, axis=-1, keepdims=True)
        yc = yf - mu
        var = jnp.mean(yc * yc, axis=-1, keepdims=True)
        yn = yc * lax.rsqrt(var + LN_EPS) * lng_ref[...] + lnb_ref[...]
        ybuf[pl.ds(i * CONV_CHUNK, CONV_CHUNK), :] = (yn * jax.nn.sigmoid(yn)).astype(BF16)
        return carry

    for i in range(rows // CONV_CHUNK):
        chunk(i, 0)
    y_conv = _dot(ybuf[...], wco_ref[...])
    new_conv = vbuf[:, CONV_PAD + ls - CONV_HIST:CONV_PAD + ls, :]
    vbuf[:, CONV_PAD - CONV_HIST:CONV_PAD, :] = new_conv

    @pl.when(j == nj - 1)
    def _():
        npool_ref[...] = new_pool
        nconv_ref[...] = new_conv

    merged = jax.nn.sigmoid(gate_pool) * y_pool + jax.nn.sigmoid(gate_conv) * y_conv
    x1 = x + _dot(merged.astype(BF16), wout_ref[...])
    x1_ref[...] = x1

    h2 = _rmsnorm(x1, gffn_ref[...])
    h2_hi = h2.astype(BF16)
    h2_lo = (h2 - h2_hi.astype(F32)).astype(BF16)
    hi = _dot(h2_hi, wrc_ref[...])
    logits = (hi[:, 0:ROUTER_COLS] + (hi[:, ROUTER_COLS:] + _dot(h2_lo, wrc_ref[:, 0:ROUTER_COLS]))) + br_ref[...]
    route, xs, nblk = _dispatch(h2_hi, *_routing(logits))
    route_ref[...] = route
    xs_ref[...] = xs
    nblk_ref[...] = jnp.broadcast_to(nblk, (SUBLANES, LANES)).astype(I32)


def _const_spec(shape):
    nd = len(shape)
    return pl.BlockSpec(shape, lambda t: (0,) * nd, pipeline_mode=pl.Buffered(1))


def _mixer(x, hist_pool, hist_conv, past, sb, ls, wts, n_tiles, tile0, shared):
    nb, seq, _ = x.shape
    assert nb % sb == 0 and seq % ls == 0 and sb * ls == TOK_TILE and ls % SUBLANES == 0 and ls >= CONV_HIST
    nj = seq // ls
    n_valid = (nb // sb) * nj
    n_steps = n_valid if shared is not None else n_tiles - tile0
    own = lambda t: jnp.minimum(t, n_valid - 1)
    hist_spec = lambda n, width: pl.BlockSpec((sb, n, width), lambda t: (own(t) // nj, 0, 0))
    in_specs = [pl.BlockSpec((sb, ls, D_MODEL), lambda t: (own(t) // nj, own(t) % nj, 0)),
                hist_spec(POOL_HIST, D_POOL), hist_spec(CONV_HIST, D_CONV)]
    in_specs += [_const_spec(w.shape) for w in wts]
    shared_shapes = (
        jax.ShapeDtypeStruct((n_tiles * TOK_TILE, D_MODEL), F32),
        jax.ShapeDtypeStruct((n_tiles * TOK_TILE, ROUTER_COLS), F32),
        jax.ShapeDtypeStruct((n_tiles, TILE_ROWS, D_MODEL), BF16),
        jax.ShapeDtypeStruct((n_tiles, SUBLANES, LANES), I32),
    )
    out_shape = shared_shapes + (jax.ShapeDtypeStruct((nb, POOL_HIST, D_POOL), F32),
                                 jax.ShapeDtypeStruct((nb, CONV_HIST, D_CONV), F32))
    out_specs = (pl.BlockSpec((TOK_TILE, D_MODEL), lambda t: (tile0 + t, 0)),
                 pl.BlockSpec((TOK_TILE, ROUTER_COLS), lambda t: (tile0 + t, 0)),
                 pl.BlockSpec((None, TILE_ROWS, D_MODEL), lambda t: (tile0 + t, 0, 0)),
                 pl.BlockSpec((None, SUBLANES, LANES), lambda t: (tile0 + t, 0, 0)),
                 hist_spec(POOL_HIST, D_POOL), hist_spec(CONV_HIST, D_CONV))
    args = [x, hist_pool, hist_conv, *wts]
    aliases = {}
    if shared is not None:
        in_specs += [pl.BlockSpec(memory_space=pl.ANY)] * len(shared)
        aliases = {len(args) + k: k for k in range(len(shared))}
        args += list(shared)
    rows = sb * ls
    return pl.pallas_call(
        functools.partial(_mixer_kernel, past, sb, ls, nj, n_valid, n_steps),
        grid=(n_steps,),
        in_specs=in_specs,
        out_specs=out_specs,
        out_shape=out_shape,
        input_output_aliases=aliases,
        scratch_shapes=[pltpu.VMEM((sb, POOL_PAD + ls, D_POOL), F32),
                        pltpu.VMEM((sb, CONV_PAD + ls, D_CONV), F32),
                        pltpu.VMEM((SUBLANES - 1, sb, CONV_PAD + ls, D_CONV), F32),
                        pltpu.VMEM((rows, D_MODEL), BF16),
                        pltpu.VMEM((CONV_CHUNK, D_CONV), F32),
                        pltpu.VMEM((rows, D_CONV), BF16)],
        compiler_params=pltpu.CompilerParams(dimension_semantics=("arbitrary",),
                                             vmem_limit_bytes=VMEM_LIMIT),
        name="mixer",
    )(*args)


def _chunk_tables(nblk, n_chunks_max):
    n_tiles = nblk.shape[0]
    e_ids = jnp.arange(N_EXPERTS, dtype=I32)
    per_e = jnp.sum(nblk, axis=0)
    chunks_e = -(-per_e // CHUNK_BLOCKS)
    cend = jnp.cumsum(chunks_e)
    cstart = cend - chunks_e
    n_chunks = cend[-1]
    c = jnp.arange(n_chunks_max, dtype=I32)
    chunk_e = jnp.minimum(jnp.sum((cend[None, :] <= c[:, None]).astype(I32), axis=1), N_EXPERTS - 1)
    is_e = chunk_e[:, None] == e_ids[None, :]
    of_chunk = lambda v: jnp.sum(jnp.where(is_e, v[None, :], 0), axis=1)
    of_chunk_t = lambda a: jnp.sum(jnp.where(is_e[:, None, :], a[None, :, :], 0), axis=2)
    cum_t = jnp.cumsum(nblk, axis=0)
    bstart = jnp.cumsum(nblk, axis=1) - nblk
    k = ((c - of_chunk(cstart)) * CHUNK_BLOCKS)[:, None] + jnp.arange(CHUNK_BLOCKS, dtype=I32)[None, :]
    tile = jnp.sum((of_chunk_t(cum_t)[:, None, :] <= k[:, :, None]).astype(I32), axis=2)
    is_t = tile[:, :, None] == jnp.arange(n_tiles, dtype=I32)[None, None, :]
    shift = of_chunk_t(bstart - (cum_t - nblk))
    blk = k + jnp.sum(jnp.where(is_t, shift[:, None, :], 0), axis=2)
    valid = (k < of_chunk(per_e)[:, None]) & (c[:, None] < n_chunks)
    table = jnp.where(valid, tile * TILE_BLOCKS + blk, -1)
    return table.reshape(-1).astype(I32), chunk_e.astype(I32), n_chunks.reshape(1).astype(I32)


def _expert_kernel(spare0, tab_ref, ce_ref, nch_ref, x_hbm, wg_ref, wu_ref, wd_ref, y_hbm,
                   xin0, xin1, yout0, yout1, wgu_bf, wd_bf, sem_in, sem_out):
    c = pl.program_id(0)
    nc = pl.num_programs(0)
    nch = nch_ref[0]

    zero_block = spare0 + 2 * CHUNK_BLOCKS

    def src_block(chunk, s):
        t = tab_ref[chunk * CHUNK_BLOCKS + s]
        return jnp.where(t < 0, zero_block, t)

    def dst_block(chunk, s):
        t = tab_ref[jnp.maximum(chunk, 0) * CHUNK_BLOCKS + s]
        return jnp.where((t < 0) | (chunk < 0), spare0 + (chunk % 2) * CHUNK_BLOCKS + s, t)

    def in_copy(block, buf, sl, s):
        return pltpu.make_async_copy(x_hbm.at[block], buf.at[pl.ds(s * BLK, BLK), :], sem_in.at[sl])

    def out_copy(block, buf, sl, s):
        return pltpu.make_async_copy(buf.at[pl.ds(s * BLK, BLK), :], y_hbm.at[block], sem_out.at[sl])

    def step(sl):
        xin, xin_o = (xin0, xin1) if sl == 0 else (xin1, xin0)
        yout, yout_o = (yout0, yout1) if sl == 0 else (yout1, yout0)

        nxt = jnp.minimum(c + 1, nc - 1)
        for s in range(CHUNK_BLOCKS):
            in_copy(jnp.where(c + 1 < nch, src_block(nxt, s), zero_block), xin_o, 1 - sl, s).start(GATHER_PRIORITY)
        for s in range(CHUNK_BLOCKS):
            in_copy(src_block(c, s), xin, sl, s).wait()

        ab = _dot(xin[...], wgu_bf[...])

        for s in range(CHUNK_BLOCKS):
            out_copy(dst_block(c - 1, s), yout_o, 1 - sl, s).start(SCATTER_PRIORITY)

        a = ab[:, 0:D_EXPERT]
        b = ab[:, D_EXPERT:2 * D_EXPERT]
        act = ((a * jax.nn.sigmoid(a)) * b).astype(BF16)

        for s in range(CHUNK_BLOCKS):
            out_copy(dst_block(c - 2, s), yout, sl, s).wait()
        yout[...] = _dot(act, wd_bf[...]).astype(BF16)

        @pl.when(c == nch - 1)
        def _():
            for s in range(CHUNK_BLOCKS):
                out_copy(dst_block(c, s), yout, sl, s).start(SCATTER_PRIORITY)
            for s in range(CHUNK_BLOCKS):
                in_copy(zero_block, xin_o, 1 - sl, s).wait()
            for s in range(CHUNK_BLOCKS):
                out_copy(dst_block(c - 1, s), yout_o, 1 - sl, s).wait()
            for s in range(CHUNK_BLOCKS):
                out_copy(dst_block(c, s), yout, sl, s).wait()

    @pl.when(c < nch)
    def _():
        @pl.when(c == 0)
        def _():
            yout0[...] = jnp.zeros(yout0.shape, yout0.dtype)
            yout1[...] = jnp.zeros(yout1.shape, yout1.dtype)
            for s in range(CHUNK_BLOCKS):
                out_copy(dst_block(c - 2, s), yout0, 0, s).start(SCATTER_PRIORITY)
            for s in range(CHUNK_BLOCKS):
                in_copy(src_block(0, s), xin0, 0, s).start(GATHER_PRIORITY)

        @pl.when((c == 0) | (ce_ref[c] != ce_ref[jnp.maximum(c - 1, 0)]))
        def _():
            wgu_bf[:, 0:D_EXPERT] = wg_ref[...].astype(BF16)
            wgu_bf[:, D_EXPERT:2 * D_EXPERT] = wu_ref[...].astype(BF16)
            wd_bf[...] = wd_ref[...].astype(BF16)

        @pl.when(c % 2 == 0)
        def _():
            step(0)

        @pl.when(c % 2 == 1)
        def _():
            step(1)


def _experts(xs_blocks, n_used_blocks, table, chunk_e, n_chunks, w_gate, w_up, w_down):
    assert xs_blocks.shape[0] - n_used_blocks > 2 * CHUNK_BLOCKS
    n_steps = chunk_e.shape[0]
    w_spec = lambda shape: pl.BlockSpec((None,) + shape, lambda c, tab, ce, nch: (ce[c], 0, 0))
    return pl.pallas_call(
        functools.partial(_expert_kernel, n_used_blocks),
        grid_spec=pltpu.PrefetchScalarGridSpec(
            num_scalar_prefetch=3,
            grid=(n_steps,),
            in_specs=[pl.BlockSpec(memory_space=pl.ANY),
                      w_spec((D_MODEL, D_EXPERT)), w_spec((D_MODEL, D_EXPERT)), w_spec((D_EXPERT, D_MODEL))],
            out_specs=pl.BlockSpec(memory_space=pl.ANY),
            scratch_shapes=[pltpu.VMEM((CHUNK_ROWS, D_MODEL), BF16),
                            pltpu.VMEM((CHUNK_ROWS, D_MODEL), BF16),
                            pltpu.VMEM((CHUNK_ROWS, D_MODEL), BF16),
                            pltpu.VMEM((CHUNK_ROWS, D_MODEL), BF16),
                            pltpu.VMEM((D_MODEL, 2 * D_EXPERT), BF16),
                            pltpu.VMEM((D_EXPERT, D_MODEL), BF16),
                            pltpu.SemaphoreType.DMA((2,)),
                            pltpu.SemaphoreType.DMA((2,))]),
        out_shape=jax.ShapeDtypeStruct(xs_blocks.shape, xs_blocks.dtype),
        input_output_aliases={3: 0},
        compiler_params=pltpu.CompilerParams(dimension_semantics=("arbitrary",),
                                             vmem_limit_bytes=VMEM_LIMIT),
        name="experts",
    )(table, chunk_e, n_chunks, xs_blocks, w_gate, w_up, w_down)


def _combine_kernel(n_first, y_ref, route_ref, x1_ref, gfin_ref, o1_ref, o2_ref):
    i = pl.program_id(0)
    route = route_ref[...]
    pos1 = route[:, ROUTE_POS1:ROUTE_POS1 + 1].astype(I32)
    pos2 = route[:, ROUTE_POS2:ROUTE_POS2 + 1].astype(I32)
    w1 = route[:, ROUTE_W1:ROUTE_W1 + 1]
    w2 = route[:, ROUTE_W2:ROUTE_W2 + 1]
    r_idx = lax.broadcasted_iota(I32, (TOK_TILE, TILE_ROWS), 1)
    gather = jnp.where(r_idx == pos1, w1, jnp.where(r_idx == pos2, w2, 0.0)).astype(BF16)
    moe = _dot(gather, y_ref[...].reshape(TILE_ROWS, D_MODEL))
    out = _rmsnorm(x1_ref[...] + moe, gfin_ref[...])

    @pl.when(i < n_first)
    def _():
        o1_ref[...] = out

    @pl.when(i >= n_first)
    def _():
        o2_ref[...] = out


def _combine(y_blocks, route, x1, g_final, n_first, n_tiles):
    tok_spec = lambda width: pl.BlockSpec((TOK_TILE, width), lambda i: (i, 0))
    return pl.pallas_call(
        functools.partial(_combine_kernel, n_first),
        grid=(n_tiles,),
        in_specs=[pl.BlockSpec((TILE_BLOCKS, BLK, D_MODEL), lambda i: (i, 0, 0)),
                  tok_spec(ROUTER_COLS), tok_spec(D_MODEL),
                  pl.BlockSpec((1, D_MODEL), lambda i: (0, 0))],
        out_specs=(pl.BlockSpec((TOK_TILE, D_MODEL), lambda i: (jnp.minimum(i, n_first - 1), 0)),
                   pl.BlockSpec((TOK_TILE, D_MODEL), lambda i: (jnp.maximum(i - n_first, 0), 0))),
        out_shape=(jax.ShapeDtypeStruct((n_first * TOK_TILE, D_MODEL), F32),
                   jax.ShapeDtypeStruct(((n_tiles - n_first) * TOK_TILE, D_MODEL), F32)),
        compiler_params=pltpu.CompilerParams(dimension_semantics=("arbitrary",),
                                             vmem_limit_bytes=VMEM_LIMIT),
        name="combine",
    )(y_blocks, route, x1, g_final)


def kernel(x_prompt, x_sample, state_pool, state_conv, g_mix, w_in, w_pool_grp, pool_scale, w_pool_out, w_dw, b_dw, ln_g, ln_b, w_conv_out, w_out, g_ffn, w_rg, b_rg, w_re, b_re, w_gate, w_up, w_down, g_final):
    assert g_mix.shape[0] == 1
    l = 0
    bp, seq_p, _ = x_prompt.shape
    bs, seq_s, _ = x_sample.shape
    tiles_p = bp * seq_p // TOK_TILE
    tiles_s = bs * seq_s // TOK_TILE
    n_tiles = tiles_p + tiles_s

    w_r = jnp.zeros((D_MODEL, ROUTER_COLS), F32)
    w_r = w_r.at[:, :N_EXPERTS].set(w_re[l]).at[:, N_EXPERTS:N_EXPERTS + N_EXPERT_GROUPS].set(w_rg[l])
    b_r = jnp.zeros((1, ROUTER_COLS), F32)
    b_r = b_r.at[0, :N_EXPERTS].set(b_re[l]).at[0, N_EXPERTS:N_EXPERTS + N_EXPERT_GROUPS].set(b_rg[l])
    w_r_hi = w_r.astype(BF16)
    w_r_cat = jnp.concatenate([w_r_hi, (w_r - w_r_hi.astype(F32)).astype(BF16)], axis=1)
    row = lambda a: a.reshape(1, -1)
    wts = (row(g_mix[l]), w_in[l].astype(BF16), w_pool_grp[l].astype(BF16), row(pool_scale[l]),
           w_pool_out[l].astype(BF16), w_dw[l], row(b_dw[l]), row(ln_g[l]), row(ln_b[l]),
           w_conv_out[l].astype(BF16), w_out[l].astype(BF16), row(g_ffn[l]), w_r_cat, b_r)

    zp = jnp.zeros((bp, POOL_HIST, D_POOL), F32)
    zc = jnp.zeros((bp, CONV_HIST, D_CONV), F32)
    n_all = n_tiles + -(-(2 * CHUNK_BLOCKS + 1) // TILE_BLOCKS)
    *shared, npp, ncp = _mixer(x_prompt, zp, zc, 0, 1, TOK_TILE, wts, n_all, 0, None)
    x1, route, xs, nblk, nps, ncs = _mixer(x_sample, state_pool[l], state_conv[l], PAST_LEN, bs, seq_s, wts,
                                           n_all, tiles_p, shared)

    n_chunks_max = -(-n_tiles * TILE_BLOCKS // CHUNK_BLOCKS) + N_EXPERTS
    table, chunk_e, n_chunks = _chunk_tables(nblk[:n_tiles, 0, :N_EXPERTS], n_chunks_max)
    y_blocks = _experts(xs.reshape(n_all * TILE_BLOCKS, BLK, D_MODEL), n_tiles * TILE_BLOCKS,
                        table, chunk_e, n_chunks,
                        w_gate[l].reshape(N_EXPERTS, D_MODEL, D_EXPERT),
                        w_up[l].reshape(N_EXPERTS, D_MODEL, D_EXPERT),
                        w_down[l].reshape(N_EXPERTS, D_EXPERT, D_MODEL))
    yp, ys = _combine(y_blocks, route, x1, row(g_final), tiles_p, n_tiles)
    return (yp.reshape(bp, seq_p, D_MODEL), ys.reshape(bs, seq_s, D_MODEL),
            npp[None], ncp[None], nps[None], ncs[None])
```

```python
import functools

import jax
import jax.numpy as jnp
from jax import lax
from jax.experimental import pallas as pl
from jax.experimental.pallas import tpu as pltpu

D_MODEL = 1024
D_POOL = 512
N_POOL_GROUPS = 4
POOL_GROUP = D_POOL // N_POOL_GROUPS
POOL_WINDOWS = (2, 4, 8, 16)
POOL_HIST = max(POOL_WINDOWS) - 1
D_CONV = 512
CONV_WIDTH = 31
CONV_HIST = CONV_WIDTH - 1
N_EXPERT_GROUPS = 4
EXPERTS_PER_GROUP = 8
N_EXPERTS = N_EXPERT_GROUPS * EXPERTS_PER_GROUP
TOP_K = 2
D_EXPERT = 256
RMS_EPS = 1e-6
LN_EPS = 1e-5
PAST_LEN = 1024

LANES = 128
SUBLANES = 8
POOL_PAD = 16
CONV_PAD = 32
CONV_CHUNK = 64
ROUTER_COLS = LANES
VMEM_LIMIT = 56 * 1024 * 1024

TOK_TILE = 512
BLK = 2 * SUBLANES
TILE_BLOCKS = (TOP_K * TOK_TILE + N_EXPERTS * (BLK - 1)) // BLK
TILE_ROWS = TILE_BLOCKS * BLK
CHUNK_BLOCKS = 32
CHUNK_ROWS = CHUNK_BLOCKS * BLK
ROUTE_POS1, ROUTE_POS2, ROUTE_W1, ROUTE_W2 = 0, 1, 2, 3

BF16 = jnp.bfloat16
F32 = jnp.float32
I32 = jnp.int32


def _dot(a, b):
    return jnp.dot(a, b, preferred_element_type=F32)


def _rmsnorm(x, g):
    return x * lax.rsqrt(jnp.mean(x * x, axis=-1, keepdims=True) + RMS_EPS) * g


def _routing(logits):
    lane = lax.broadcasted_iota(I32, logits.shape, 1)
    lane_f = lane.astype(F32)
    neg = jnp.float32(-jnp.inf)
    big = jnp.float32(1e9)
    is_grp = (lane >= N_EXPERTS) & (lane < N_EXPERTS + N_EXPERT_GROUPS)
    glog = jnp.where(is_grp, logits, neg)
    gmax = jnp.max(glog, axis=1, keepdims=True)
    gidx = jnp.min(jnp.where(glog == gmax, lane_f, big), axis=1, keepdims=True) - float(N_EXPERTS)
    gsum = jnp.sum(jnp.where(is_grp, jnp.exp(glog - gmax), 0.0), axis=1, keepdims=True)
    p_sel = 1.0 / gsum
    lane_grp = (lane >> 3).astype(F32)
    is_sel = (lane < N_EXPERTS) & (lane_grp == gidx)
    elog = jnp.where(is_sel, logits, neg)
    v1 = jnp.max(elog, axis=1, keepdims=True)
    i1 = jnp.min(jnp.where(elog == v1, lane_f, big), axis=1, keepdims=True)
    elog2 = jnp.where(lane_f == i1, neg, elog)
    v2 = jnp.max(elog2, axis=1, keepdims=True)
    i2 = jnp.min(jnp.where(elog2 == v2, lane_f, big), axis=1, keepdims=True)
    t = jnp.exp(v2 - v1)
    return i1, i2, p_sel / (1.0 + t), p_sel * (t / (1.0 + t))


def _dispatch(h2, i1, i2, w1, w2):
    rows = h2.shape[0]
    lane = lax.broadcasted_iota(I32, (rows, LANES), 1)
    lane_f = lane.astype(F32)
    hit1 = lane_f == i1
    hit2 = lane_f == i2
    sel = jnp.where(hit1 | hit2, 1.0, 0.0)
    earlier = lax.broadcasted_iota(I32, (rows, rows), 1) < lax.broadcasted_iota(I32, (rows, rows), 0)
    rank = _dot(jnp.where(earlier, 1.0, 0.0).astype(BF16), sel.astype(BF16))
    cnt = jnp.sum(sel, axis=0, keepdims=True)
    nblk = jnp.floor((cnt + float(BLK - 1)) * (1.0 / BLK))
    before = lax.broadcasted_iota(I32, (LANES, LANES), 0) < lax.broadcasted_iota(I32, (LANES, LANES), 1)
    bstart = _dot(jnp.broadcast_to(nblk, (SUBLANES, LANES)).astype(BF16),
                  jnp.where(before, 1.0, 0.0).astype(BF16))[0:1, :]
    pos = float(BLK) * bstart + rank
    pos1 = jnp.sum(jnp.where(hit1, pos, 0.0), axis=1, keepdims=True)
    pos2 = jnp.sum(jnp.where(hit2, pos, 0.0), axis=1, keepdims=True)
    route = jnp.where(lane == ROUTE_POS1, pos1,
                      jnp.where(lane == ROUTE_POS2, pos2,
                                jnp.where(lane == ROUTE_W1, w1, jnp.where(lane == ROUTE_W2, w2, 0.0))))
    route_t = route.T
    p1 = route_t[ROUTE_POS1:ROUTE_POS1 + 1, :].astype(I32)
    p2 = route_t[ROUTE_POS2:ROUTE_POS2 + 1, :].astype(I32)
    r_idx = lax.broadcasted_iota(I32, (TILE_ROWS, rows), 0)
    onehot = jnp.where((r_idx == p1) | (r_idx == p2), 1.0, 0.0).astype(BF16)
    return route, _dot(onehot, h2).astype(BF16), nblk


N_MIXER_IN = 17
N_MIXER_OUT = 6
N_MIXER_SCRATCH = 5


def _mixer_kernel(past, sb, ls, nj, n_valid, n_steps, *refs):
    t = pl.program_id(0)
    if n_valid == n_steps:
        _mixer_tile(past, sb, ls, t % nj, nj, *refs)
        return
    outs = refs[-(N_MIXER_OUT + N_MIXER_SCRATCH):][:N_MIXER_OUT]

    @pl.when(t < n_valid)
    def _():
        _mixer_tile(past, sb, ls, t % nj, nj, *refs)

    @pl.when(t >= n_valid)
    def _():
        for ref in outs[:4]:
            ref[...] = jnp.zeros(ref.shape, ref.dtype)


def _mixer_tile(past, sb, ls, j, nj, *refs):
    (x_ref, hp_ref, hc_ref, gmix_ref, win_ref, wgrp_ref, pscale_ref, wpo_ref, wdw_ref, bdw_ref,
     lng_ref, lnb_ref, wco_ref, wout_ref, gffn_ref, wrc_ref, br_ref) = refs[:N_MIXER_IN]
    x1_ref, route_ref, xs_ref, nblk_ref, npool_ref, nconv_ref = refs[-(N_MIXER_OUT + N_MIXER_SCRATCH):][:N_MIXER_OUT]
    ubuf, vbuf, sbuf, cbuf, ybuf = refs[-N_MIXER_SCRATCH:]
    rows = sb * ls

    @pl.when(j == 0)
    def _():
        ubuf[:, POOL_PAD - POOL_HIST:POOL_PAD, :] = hp_ref[...]
        vbuf[:, CONV_PAD - CONV_HIST:CONV_PAD, :] = hc_ref[...]

    x = x_ref[...].reshape(rows, D_MODEL)
    h = _rmsnorm(x, gmix_ref[...]).astype(BF16)
    u = _dot(h, win_ref[:, 0:D_POOL])
    ubuf[:, POOL_PAD:POOL_PAD + ls, :] = u.reshape(sb, ls, D_POOL)
    ga = _dot(h, win_ref[:, D_POOL:D_POOL + D_CONV])
    gb = _dot(h, win_ref[:, D_POOL + D_CONV:D_POOL + 2 * D_CONV])
    vbuf[:, CONV_PAD:CONV_PAD + ls, :] = (ga * jax.nn.sigmoid(gb)).reshape(sb, ls, D_CONV)
    c_gp = D_POOL + 2 * D_CONV
    gate_pool = _dot(h, win_ref[:, c_gp:c_gp + D_MODEL])
    gate_conv = _dot(h, win_ref[:, c_gp + D_MODEL:c_gp + 2 * D_MODEL])

    t_idx = lax.broadcasted_iota(I32, (sb, ls, 1), 1)
    frames = (past + 1 + j * ls + t_idx).astype(F32)
    ys = []
    part, width = ubuf[...], 1
    for gi, w in enumerate(POOL_WINDOWS):
        c0, c1 = gi * POOL_GROUP, (gi + 1) * POOL_GROUP
        cur = ubuf[:, POOL_PAD:POOL_PAD + ls, c0:c1]
        while width < w:
            part = part + pltpu.roll(part, width, 1)
            width *= 2
        s = part[:, POOL_PAD:POOL_PAD + ls, 0:POOL_GROUP]
        part = part[:, :, POOL_GROUP:]
        mean = s / jnp.minimum(jnp.float32(w), frames)
        ys.append(_dot((mean - cur).reshape(rows, POOL_GROUP).astype(BF16), wgrp_ref[gi]))
    yp = (jnp.concatenate(ys, axis=1) * pscale_ref[...]).astype(BF16)
    y_pool = _dot(yp, wpo_ref[...])
    new_pool = ubuf[:, POOL_PAD + ls - POOL_HIST:POOL_PAD + ls, :]
    ubuf[:, POOL_PAD - POOL_HIST:POOL_PAD, :] = new_pool

    first = CONV_PAD - CONV_HIST
    taps = [[o for o in range(first, first + CONV_WIDTH) if o % SUBLANES == r] for r in range(SUBLANES)]
    for r in range(1, SUBLANES):
        span = taps[r][-1] - taps[r][0] + ls
        sbuf[r - 1, :, 0:span, :] = vbuf[:, taps[r][0]:taps[r][0] + span, :]
    nb_c, nt_c = (1, CONV_CHUNK) if ls >= CONV_CHUNK else (CONV_CHUNK // ls, ls)
    n_grp = nt_c // SUBLANES
    for i in range(rows // CONV_CHUNK):
        b0, t0 = (0, i * nt_c) if nb_c == 1 else (i * nb_c, 0)
        for c0 in range(0, D_CONV, LANES):
            cs = slice(c0, c0 + LANES)
            accs = [jnp.zeros((nb_c, SUBLANES, LANES), F32) + bdw_ref[:, cs] for _ in range(n_grp)]
            for r in range(SUBLANES):
                ws = [wdw_ref[o - first:o - first + 1, cs] for o in taps[r]]
                for q in range(n_grp + len(taps[r]) - 1):
                    lo = t0 + q * SUBLANES
                    if r == 0:
                        val = vbuf[b0:b0 + nb_c, taps[0][0] + lo:taps[0][0] + lo + SUBLANES, cs]
                    else:
                        val = sbuf[r - 1, b0:b0 + nb_c, lo:lo + SUBLANES, cs]
                    for a in range(len(taps[r])):
                        if 0 <= q - a < n_grp:
                            accs[q - a] = accs[q - a] + val * ws[a]
            cbuf[:, cs] = jnp.concatenate(accs, axis=1).reshape(CONV_CHUNK, LANES)
        yf = cbuf[...]
        mu = jnp.mean(yf, axis=-1, keepdims=True)
        yc = yf - mu
        var = jnp.mean(yc * yc, axis=-1, keepdims=True)
        yn = yc * lax.rsqrt(var + LN_EPS) * lng_ref[...] + lnb_ref[...]
        ybuf[i * CONV_CHUNK:(i + 1) * CONV_CHUNK, :] = (yn * jax.nn.sigmoid(yn)).astype(BF16)
    y_conv = _dot(ybuf[...], wco_ref[...])
    new_conv = vbuf[:, CONV_PAD + ls - CONV_HIST:CONV_PAD + ls, :]
    vbuf[:, CONV_PAD - CONV_HIST:CONV_PAD, :] = new_conv

    @pl.when(j == nj - 1)
    def _():
        npool_ref[...] = new_pool
        nconv_ref[...] = new_conv

    merged = jax.nn.sigmoid(gate_pool) * y_pool + jax.nn.sigmoid(gate_conv) * y_conv
    x1 = x + _dot(merged.astype(BF16), wout_ref[...])
    x1_ref[...] = x1

    h2 = _rmsnorm(x1, gffn_ref[...])
    h2_hi = h2.astype(BF16)
    h2_lo = (h2 - h2_hi.astype(F32)).astype(BF16)
    hi = _dot(h2_hi, wrc_ref[...])
    logits = (hi[:, 0:ROUTER_COLS] + (hi[:, ROUTER_COLS:] + _dot(h2_lo, wrc_ref[:, 0:ROUTER_COLS]))) + br_ref[...]
    route, xs, nblk = _dispatch(h2_hi, *_routing(logits))
    route_ref[...] = route
    xs_ref[...] = xs
    nblk_ref[...] = jnp.broadcast_to(nblk, (SUBLANES, LANES)).astype(I32)


def _const_spec(shape):
    nd = len(shape)
    return pl.BlockSpec(shape, lambda t: (0,) * nd, pipeline_mode=pl.Buffered(1))


def _mixer(x, hist_pool, hist_conv, past, sb, ls, wts, n_tiles, tile0, shared):
    nb, seq, _ = x.shape
    assert nb % sb == 0 and seq % ls == 0 and sb * ls == TOK_TILE and ls % SUBLANES == 0 and ls >= CONV_HIST
    nj = seq // ls
    n_valid = (nb // sb) * nj
    n_steps = n_valid if shared is not None else n_tiles - tile0
    own = lambda t: jnp.minimum(t, n_valid - 1)
    hist_spec = lambda n, width: pl.BlockSpec((sb, n, width), lambda t: (own(t) // nj, 0, 0))
    in_specs = [pl.BlockSpec((sb, ls, D_MODEL), lambda t: (own(t) // nj, own(t) % nj, 0)),
                hist_spec(POOL_HIST, D_POOL), hist_spec(CONV_HIST, D_CONV)]
    in_specs += [_const_spec(w.shape) for w in wts]
    shared_shapes = (
        jax.ShapeDtypeStruct((n_tiles * TOK_TILE, D_MODEL), F32),
        jax.ShapeDtypeStruct((n_tiles * TOK_TILE, ROUTER_COLS), F32),
        jax.ShapeDtypeStruct((n_tiles, TILE_ROWS, D_MODEL), BF16),
        jax.ShapeDtypeStruct((n_tiles, SUBLANES, LANES), I32),
    )
    out_shape = shared_shapes + (jax.ShapeDtypeStruct((nb, POOL_HIST, D_POOL), F32),
                                 jax.ShapeDtypeStruct((nb, CONV_HIST, D_CONV), F32))
    out_specs = (pl.BlockSpec((TOK_TILE, D_MODEL), lambda t: (tile0 + t, 0)),
                 pl.BlockSpec((TOK_TILE, ROUTER_COLS), lambda t: (tile0 + t, 0)),
                 pl.BlockSpec((None, TILE_ROWS, D_MODEL), lambda t: (tile0 + t, 0, 0)),
                 pl.BlockSpec((None, SUBLANES, LANES), lambda t: (tile0 + t, 0, 0)),
                 hist_spec(POOL_HIST, D_POOL), hist_spec(CONV_HIST, D_CONV))
    args = [x, hist_pool, hist_conv, *wts]
    aliases = {}
    if shared is not None:
        in_specs += [pl.BlockSpec(memory_space=pl.ANY)] * len(shared)
        aliases = {len(args) + k: k for k in range(len(shared))}
        args += list(shared)
    rows = sb * ls
    return pl.pallas_call(
        functools.partial(_mixer_kernel, past, sb, ls, nj, n_valid, n_steps),
        grid=(n_steps,),
        in_specs=in_specs,
        out_specs=out_specs,
        out_shape=out_shape,
        input_output_aliases=aliases,
        scratch_shapes=[pltpu.VMEM((sb, POOL_PAD + ls, D_POOL), F32),
                        pltpu.VMEM((sb, CONV_PAD + ls, D_CONV), F32),
                        pltpu.VMEM((SUBLANES - 1, sb, CONV_PAD + ls, D_CONV), F32),
                        pltpu.VMEM((CONV_CHUNK, D_CONV), F32),
                        pltpu.VMEM((rows, D_CONV), BF16)],
        compiler_params=pltpu.CompilerParams(dimension_semantics=("arbitrary",),
                                             vmem_limit_bytes=VMEM_LIMIT),
        name="mixer",
    )(*args)


def _chunk_tables(nblk, n_chunks_max):
    n_tiles = nblk.shape[0]
    e_ids = jnp.arange(N_EXPERTS, dtype=I32)
    per_e = jnp.sum(nblk, axis=0)
    chunks_e = -(-per_e // CHUNK_BLOCKS)
    cend = jnp.cumsum(chunks_e)
    cstart = cend - chunks_e
    n_chunks = cend[-1]
    c = jnp.arange(n_chunks_max, dtype=I32)
    chunk_e = jnp.minimum(jnp.sum((cend[None, :] <= c[:, None]).astype(I32), axis=1), N_EXPERTS - 1)
    is_e = chunk_e[:, None] == e_ids[None, :]
    of_chunk = lambda v: jnp.sum(jnp.where(is_e, v[None, :], 0), axis=1)
    of_chunk_t = lambda a: jnp.sum(jnp.where(is_e[:, None, :], a[None, :, :], 0), axis=2)
    cum_t = jnp.cumsum(nblk, axis=0)
    bstart = jnp.cumsum(nblk, axis=1) - nblk
    k = ((c - of_chunk(cstart)) * CHUNK_BLOCKS)[:, None] + jnp.arange(CHUNK_BLOCKS, dtype=I32)[None, :]
    tile = jnp.sum((of_chunk_t(cum_t)[:, None, :] <= k[:, :, None]).astype(I32), axis=2)
    is_t = tile[:, :, None] == jnp.arange(n_tiles, dtype=I32)[None, None, :]
    shift = of_chunk_t(bstart - (cum_t - nblk))
    blk = k + jnp.sum(jnp.where(is_t, shift[:, None, :], 0), axis=2)
    valid = (k < of_chunk(per_e)[:, None]) & (c[:, None] < n_chunks)
    table = jnp.where(valid, tile * TILE_BLOCKS + blk, -1)
    return table.reshape(-1).astype(I32), chunk_e.astype(I32), n_chunks.reshape(1).astype(I32)


def _expert_kernel(spare0, tab_ref, ce_ref, nch_ref, x_hbm, wg_ref, wu_ref, wd_ref, y_hbm,
                   xin0, xin1, yout0, yout1, wgu_bf, wd_bf, sem_in, sem_out):
    c = pl.program_id(0)
    nc = pl.num_programs(0)
    nch = nch_ref[0]

    zero_block = spare0 + 2 * CHUNK_BLOCKS

    def src_block(chunk, s):
        t = tab_ref[chunk * CHUNK_BLOCKS + s]
        return jnp.where(t < 0, zero_block, t)

    def dst_block(chunk, s):
        t = tab_ref[jnp.maximum(chunk, 0) * CHUNK_BLOCKS + s]
        return jnp.where((t < 0) | (chunk < 0), spare0 + (chunk % 2) * CHUNK_BLOCKS + s, t)

    def in_copy(block, buf, sl, s):
        return pltpu.make_async_copy(x_hbm.at[block], buf.at[pl.ds(s * BLK, BLK), :], sem_in.at[sl])

    def out_copy(block, buf, sl, s):
        return pltpu.make_async_copy(buf.at[pl.ds(s * BLK, BLK), :], y_hbm.at[block], sem_out.at[sl])

    def step(sl):
        xin, xin_o = (xin0, xin1) if sl == 0 else (xin1, xin0)
        yout, yout_o = (yout0, yout1) if sl == 0 else (yout1, yout0)

        nxt = jnp.minimum(c + 1, nc - 1)
        for s in range(CHUNK_BLOCKS):
            in_copy(jnp.where(c + 1 < nch, src_block(nxt, s), zero_block), xin_o, 1 - sl, s).start()
        for s in range(CHUNK_BLOCKS):
            in_copy(src_block(c, s), xin, sl, s).wait()

        ab = _dot(xin[...], wgu_bf[...])

        for s in range(CHUNK_BLOCKS):
            out_copy(dst_block(c - 1, s), yout_o, 1 - sl, s).start()

        a = ab[:, 0:D_EXPERT]
        b = ab[:, D_EXPERT:2 * D_EXPERT]
        act = ((a * jax.nn.sigmoid(a)) * b).astype(BF16)

        for s in range(CHUNK_BLOCKS):
            out_copy(dst_block(c - 2, s), yout, sl, s).wait()
        yout[...] = _dot(act, wd_bf[...]).astype(BF16)

        @pl.when(c == nch - 1)
        def _():
            for s in range(CHUNK_BLOCKS):
                out_copy(dst_block(c, s), yout, sl, s).start()
            for s in range(CHUNK_BLOCKS):
                in_copy(zero_block, xin_o, 1 - sl, s).wait()
            for s in range(CHUNK_BLOCKS):
                out_copy(dst_block(c - 1, s), yout_o, 1 - sl, s).wait()
            for s in range(CHUNK_BLOCKS):
                out_copy(dst_block(c, s), yout, sl, s).wait()

    @pl.when(c < nch)
    def _():
        @pl.when(c == 0)
        def _():
            yout0[...] = jnp.zeros(yout0.shape, yout0.dtype)
            yout1[...] = jnp.zeros(yout1.shape, yout1.dtype)
            for s in range(CHUNK_BLOCKS):
                out_copy(dst_block(c - 2, s), yout0, 0, s).start()
            for s in range(CHUNK_BLOCKS):
                in_copy(src_block(0, s), xin0, 0, s).start()

        @pl.when((c == 0) | (ce_ref[c] != ce_ref[jnp.maximum(c - 1, 0)]))
        def _():
            wgu_bf[:, 0:D_EXPERT] = wg_ref[...].astype(BF16)
            wgu_bf[:, D_EXPERT:2 * D_EXPERT] = wu_ref[...].astype(BF16)
            wd_bf[...] = wd_ref[...].astype(BF16)

        @pl.when(c % 2 == 0)
        def _():
            step(0)

        @pl.when(c % 2 == 1)
        def _():
            step(1)


def _experts(xs_blocks, n_used_blocks, table, chunk_e, n_chunks, w_gate, w_up, w_down):
    assert xs_blocks.shape[0] - n_used_blocks > 2 * CHUNK_BLOCKS
    n_steps = chunk_e.shape[0]
    w_spec = lambda shape: pl.BlockSpec((None,) + shape, lambda c, tab, ce, nch: (ce[c], 0, 0))
    return pl.pallas_call(
        functools.partial(_expert_kernel, n_used_blocks),
        grid_spec=pltpu.PrefetchScalarGridSpec(
            num_scalar_prefetch=3,
            grid=(n_steps,),
            in_specs=[pl.BlockSpec(memory_space=pl.ANY),
                      w_spec((D_MODEL, D_EXPERT)), w_spec((D_MODEL, D_EXPERT)), w_spec((D_EXPERT, D_MODEL))],
            out_specs=pl.BlockSpec(memory_space=pl.ANY),
            scratch_shapes=[pltpu.VMEM((CHUNK_ROWS, D_MODEL), BF16),
                            pltpu.VMEM((CHUNK_ROWS, D_MODEL), BF16),
                            pltpu.VMEM((CHUNK_ROWS, D_MODEL), BF16),
                            pltpu.VMEM((CHUNK_ROWS, D_MODEL), BF16),
                            pltpu.VMEM((D_MODEL, 2 * D_EXPERT), BF16),
                            pltpu.VMEM((D_EXPERT, D_MODEL), BF16),
                            pltpu.SemaphoreType.DMA((2,)),
                            pltpu.SemaphoreType.DMA((2,))]),
        out_shape=jax.ShapeDtypeStruct(xs_blocks.shape, xs_blocks.dtype),
        input_output_aliases={3: 0},
        compiler_params=pltpu.CompilerParams(dimension_semantics=("arbitrary",),
                                             vmem_limit_bytes=VMEM_LIMIT),
        name="experts",
    )(table, chunk_e, n_chunks, xs_blocks, w_gate, w_up, w_down)


def _combine_kernel(n_first, y_ref, route_ref, x1_ref, gfin_ref, o1_ref, o2_ref):
    i = pl.program_id(0)
    route = route_ref[...]
    pos1 = route[:, ROUTE_POS1:ROUTE_POS1 + 1].astype(I32)
    pos2 = route[:, ROUTE_POS2:ROUTE_POS2 + 1].astype(I32)
    w1 = route[:, ROUTE_W1:ROUTE_W1 + 1]
    w2 = route[:, ROUTE_W2:ROUTE_W2 + 1]
    r_idx = lax.broadcasted_iota(I32, (TOK_TILE, TILE_ROWS), 1)
    gather = jnp.where(r_idx == pos1, w1, jnp.where(r_idx == pos2, w2, 0.0)).astype(BF16)
    moe = _dot(gather, y_ref[...].reshape(TILE_ROWS, D_MODEL))
    out = _rmsnorm(x1_ref[...] + moe, gfin_ref[...])

    @pl.when(i < n_first)
    def _():
        o1_ref[...] = out

    @pl.when(i >= n_first)
    def _():
        o2_ref[...] = out


def _combine(y_blocks, route, x1, g_final, n_first, n_tiles):
    tok_spec = lambda width: pl.BlockSpec((TOK_TILE, width), lambda i: (i, 0))
    return pl.pallas_call(
        functools.partial(_combine_kernel, n_first),
        grid=(n_tiles,),
        in_specs=[pl.BlockSpec((TILE_BLOCKS, BLK, D_MODEL), lambda i: (i, 0, 0)),
                  tok_spec(ROUTER_COLS), tok_spec(D_MODEL),
                  pl.BlockSpec((1, D_MODEL), lambda i: (0, 0))],
        out_specs=(pl.BlockSpec((TOK_TILE, D_MODEL), lambda i: (jnp.minimum(i, n_first - 1), 0)),
                   pl.BlockSpec((TOK_TILE, D_MODEL), lambda i: (jnp.maximum(i - n_first, 0), 0))),
        out_shape=(jax.ShapeDtypeStruct((n_first * TOK_TILE, D_MODEL), F32),
                   jax.ShapeDtypeStruct(((n_tiles - n_first) * TOK_TILE, D_MODEL), F32)),
        compiler_params=pltpu.CompilerParams(dimension_semantics=("arbitrary",),
                                             vmem_limit_bytes=VMEM_LIMIT),
        name="combine",
    )(y_blocks, route, x1, g_final)


def kernel(x_prompt, x_sample, state_pool, state_conv, g_mix, w_in, w_pool_grp, pool_scale, w_pool_out, w_dw, b_dw, ln_g, ln_b, w_conv_out, w_out, g_ffn, w_rg, b_rg, w_re, b_re, w_gate, w_up, w_down, g_final):
    assert g_mix.shape[0] == 1
    l = 0
    bp, seq_p, _ = x_prompt.shape
    bs, seq_s, _ = x_sample.shape
    tiles_p = bp * seq_p // TOK_TILE
    tiles_s = bs * seq_s // TOK_TILE
    n_tiles = tiles_p + tiles_s

    n_pad = ROUTER_COLS - N_EXPERTS - N_EXPERT_GROUPS
    w_r = jnp.concatenate([w_re[l], w_rg[l], jnp.zeros((D_MODEL, n_pad), F32)], axis=1)
    b_r = jnp.concatenate([b_re[l], b_rg[l], jnp.zeros((n_pad,), F32)]).reshape(1, ROUTER_COLS)
    w_r_hi = w_r.astype(BF16)
    w_r_cat = jnp.concatenate([w_r_hi, (w_r - w_r_hi.astype(F32)).astype(BF16)], axis=1)
    row = lambda a: a.reshape(1, -1)
    wts = (row(g_mix[l]), w_in[l].astype(BF16), w_pool_grp[l].astype(BF16), row(pool_scale[l]),
           w_pool_out[l].astype(BF16), w_dw[l], row(b_dw[l]), row(ln_g[l]), row(ln_b[l]),
           w_conv_out[l].astype(BF16), w_out[l].astype(BF16), row(g_ffn[l]), w_r_cat, b_r)

    zp = jnp.zeros((bp, POOL_HIST, D_POOL), F32)
    zc = jnp.zeros((bp, CONV_HIST, D_CONV), F32)
    *shared, npp, ncp = _mixer(x_prompt, zp, zc, 0, 1, TOK_TILE, wts, n_tiles + 1, 0, None)
    x1, route, xs, nblk, nps, ncs = _mixer(x_sample, state_pool[l], state_conv[l], PAST_LEN, bs, seq_s, wts,
                                           n_tiles + 1, tiles_p, shared)

    n_chunks_max = -(-n_tiles * TILE_BLOCKS // CHUNK_BLOCKS) + N_EXPERTS
    table, chunk_e, n_chunks = _chunk_tables(nblk[:n_tiles, 0, :N_EXPERTS], n_chunks_max)
    y_blocks = _experts(xs.reshape((n_tiles + 1) * TILE_BLOCKS, BLK, D_MODEL), n_tiles * TILE_BLOCKS,
                        table, chunk_e, n_chunks,
                        w_gate[l].reshape(N_EXPERTS, D_MODEL, D_EXPERT),
                        w_up[l].reshape(N_EXPERTS, D_MODEL, D_EXPERT),
                        w_down[l].reshape(N_EXPERTS, D_EXPERT, D_MODEL))
    yp, ys = _combine(y_blocks, route, x1, row(g_final), tiles_p, n_tiles)
    return (yp.reshape(bp, seq_p, D_MODEL), ys.reshape(bs, seq_s, D_MODEL),
            npp[None], ncp[None], nps[None], ncs[None])
```

```python
import functools

import jax
import jax.numpy as jnp
from jax import lax
from jax.experimental import pallas as pl
from jax.experimental.pallas import tpu as pltpu

D_MODEL = 1024
D_POOL = 512
N_POOL_GROUPS = 4
POOL_GROUP = D_POOL // N_POOL_GROUPS
POOL_WINDOWS = (2, 4, 8, 16)
POOL_HIST = max(POOL_WINDOWS) - 1
D_CONV = 512
CONV_WIDTH = 31
CONV_HIST = CONV_WIDTH - 1
N_EXPERT_GROUPS = 4
EXPERTS_PER_GROUP = 8
N_EXPERTS = N_EXPERT_GROUPS * EXPERTS_PER_GROUP
TOP_K = 2
D_EXPERT = 256
RMS_EPS = 1e-6
LN_EPS = 1e-5
PAST_LEN = 1024

LANES = 128
SUBLANES = 8
POOL_PAD = 16
CONV_PAD = 32
CONV_CHUNK = 64
ROUTER_COLS = LANES
VMEM_LIMIT = 56 * 1024 * 1024

TOK_TILE = 512
BLK = 2 * SUBLANES
TILE_BLOCKS = (TOP_K * TOK_TILE + N_EXPERTS * (BLK - 1)) // BLK
TILE_ROWS = TILE_BLOCKS * BLK
CHUNK_BLOCKS = 32
CHUNK_ROWS = CHUNK_BLOCKS * BLK
ROUTE_POS1, ROUTE_POS2, ROUTE_W1, ROUTE_W2 = 0, 1, 2, 3
COMBINE_TILES = 2

BF16 = jnp.bfloat16
F32 = jnp.float32
I32 = jnp.int32


def _dot(a, b):
    return jnp.dot(a, b, preferred_element_type=F32)


def _rmsnorm(x, g):
    return x * lax.rsqrt(jnp.mean(x * x, axis=-1, keepdims=True) + RMS_EPS) * g


def _routing(logits):
    lane = lax.broadcasted_iota(I32, logits.shape, 1)
    lane_f = lane.astype(F32)
    neg = jnp.float32(-jnp.inf)
    big = jnp.float32(1e9)
    is_grp = (lane >= N_EXPERTS) & (lane < N_EXPERTS + N_EXPERT_GROUPS)
    glog = jnp.where(is_grp, logits, neg)
    gmax = jnp.max(glog, axis=1, keepdims=True)
    gidx = jnp.min(jnp.where(glog == gmax, lane_f, big), axis=1, keepdims=True) - float(N_EXPERTS)
    gsum = jnp.sum(jnp.where(is_grp, jnp.exp(glog - gmax), 0.0), axis=1, keepdims=True)
    p_sel = 1.0 / gsum
    lane_grp = (lane >> 3).astype(F32)
    is_sel = (lane < N_EXPERTS) & (lane_grp == gidx)
    elog = jnp.where(is_sel, logits, neg)
    v1 = jnp.max(elog, axis=1, keepdims=True)
    i1 = jnp.min(jnp.where(elog == v1, lane_f, big), axis=1, keepdims=True)
    elog2 = jnp.where(lane_f == i1, neg, elog)
    v2 = jnp.max(elog2, axis=1, keepdims=True)
    i2 = jnp.min(jnp.where(elog2 == v2, lane_f, big), axis=1, keepdims=True)
    t = jnp.exp(v2 - v1)
    return i1, i2, p_sel / (1.0 + t), p_sel * (t / (1.0 + t))


def _dispatch(h2, i1, i2, w1, w2):
    rows = h2.shape[0]
    lane = lax.broadcasted_iota(I32, (rows, LANES), 1)
    lane_f = lane.astype(F32)
    hit1 = lane_f == i1
    hit2 = lane_f == i2
    sel = jnp.where(hit1 | hit2, 1.0, 0.0)
    earlier = lax.broadcasted_iota(I32, (rows, rows), 1) < lax.broadcasted_iota(I32, (rows, rows), 0)
    rank = _dot(jnp.where(earlier, 1.0, 0.0).astype(BF16), sel.astype(BF16))
    cnt = jnp.sum(sel, axis=0, keepdims=True)
    nblk = jnp.floor((cnt + float(BLK - 1)) * (1.0 / BLK))
    before = lax.broadcasted_iota(I32, (LANES, LANES), 0) < lax.broadcasted_iota(I32, (LANES, LANES), 1)
    bstart = _dot(jnp.broadcast_to(nblk, (SUBLANES, LANES)).astype(BF16),
                  jnp.where(before, 1.0, 0.0).astype(BF16))[0:1, :]
    pos = float(BLK) * bstart + rank
    pos1 = jnp.sum(jnp.where(hit1, pos, 0.0), axis=1, keepdims=True)
    pos2 = jnp.sum(jnp.where(hit2, pos, 0.0), axis=1, keepdims=True)
    route = jnp.where(lane == ROUTE_POS1, pos1,
                      jnp.where(lane == ROUTE_POS2, pos2,
                                jnp.where(lane == ROUTE_W1, w1, jnp.where(lane == ROUTE_W2, w2, 0.0))))
    route_t = route.T
    p1 = route_t[ROUTE_POS1:ROUTE_POS1 + 1, :].astype(I32)
    p2 = route_t[ROUTE_POS2:ROUTE_POS2 + 1, :].astype(I32)
    r_idx = lax.broadcasted_iota(I32, (TILE_ROWS, rows), 0)
    onehot = jnp.where((r_idx == p1) | (r_idx == p2), 1.0, 0.0).astype(BF16)
    return route, _dot(onehot, h2).astype(BF16), nblk


N_MIXER_IN = 17
N_MIXER_OUT = 6
N_MIXER_SCRATCH = 5


def _mixer_kernel(past, sb, ls, nj, n_valid, n_steps, *refs):
    t = pl.program_id(0)
    if n_valid == n_steps:
        _mixer_tile(past, sb, ls, t % nj, nj, *refs)
        return
    outs = refs[-(N_MIXER_OUT + N_MIXER_SCRATCH):][:N_MIXER_OUT]

    @pl.when(t < n_valid)
    def _():
        _mixer_tile(past, sb, ls, t % nj, nj, *refs)

    @pl.when(t >= n_valid)
    def _():
        for ref in outs[:4]:
            ref[...] = jnp.zeros(ref.shape, ref.dtype)


def _mixer_tile(past, sb, ls, j, nj, *refs):
    (x_ref, hp_ref, hc_ref, gmix_ref, win_ref, wgrp_ref, pscale_ref, wpo_ref, wdw_ref, bdw_ref,
     lng_ref, lnb_ref, wco_ref, wout_ref, gffn_ref, wrc_ref, br_ref) = refs[:N_MIXER_IN]
    x1_ref, route_ref, xs_ref, nblk_ref, npool_ref, nconv_ref = refs[-(N_MIXER_OUT + N_MIXER_SCRATCH):][:N_MIXER_OUT]
    ubuf, vbuf, sbuf, cbuf, ybuf = refs[-N_MIXER_SCRATCH:]
    rows = sb * ls

    @pl.when(j == 0)
    def _():
        ubuf[:, POOL_PAD - POOL_HIST:POOL_PAD, :] = hp_ref[...]
        vbuf[:, CONV_PAD - CONV_HIST:CONV_PAD, :] = hc_ref[...]

    x = x_ref[...].reshape(rows, D_MODEL)
    h = _rmsnorm(x, gmix_ref[...]).astype(BF16)
    u = _dot(h, win_ref[:, 0:D_POOL])
    ubuf[:, POOL_PAD:POOL_PAD + ls, :] = u.reshape(sb, ls, D_POOL)
    ga = _dot(h, win_ref[:, D_POOL:D_POOL + D_CONV])
    gb = _dot(h, win_ref[:, D_POOL + D_CONV:D_POOL + 2 * D_CONV])
    vbuf[:, CONV_PAD:CONV_PAD + ls, :] = (ga * jax.nn.sigmoid(gb)).reshape(sb, ls, D_CONV)
    c_gp = D_POOL + 2 * D_CONV
    gate_pool = _dot(h, win_ref[:, c_gp:c_gp + D_MODEL])
    gate_conv = _dot(h, win_ref[:, c_gp + D_MODEL:c_gp + 2 * D_MODEL])

    t_idx = lax.broadcasted_iota(I32, (sb, ls, 1), 1)
    frames = (past + 1 + j * ls + t_idx).astype(F32)
    ys = []
    part, width = ubuf[...], 1
    for gi, w in enumerate(POOL_WINDOWS):
        c0, c1 = gi * POOL_GROUP, (gi + 1) * POOL_GROUP
        cur = ubuf[:, POOL_PAD:POOL_PAD + ls, c0:c1]
        while width < w:
            part = part + pltpu.roll(part, width, 1)
            width *= 2
        s = part[:, POOL_PAD:POOL_PAD + ls, 0:POOL_GROUP]
        part = part[:, :, POOL_GROUP:]
        mean = s / jnp.minimum(jnp.float32(w), frames)
        ys.append(_dot((mean - cur).reshape(rows, POOL_GROUP).astype(BF16), wgrp_ref[gi]))
    yp = (jnp.concatenate(ys, axis=1) * pscale_ref[...]).astype(BF16)
    y_pool = _dot(yp, wpo_ref[...])
    new_pool = ubuf[:, POOL_PAD + ls - POOL_HIST:POOL_PAD + ls, :]
    ubuf[:, POOL_PAD - POOL_HIST:POOL_PAD, :] = new_pool

    first = CONV_PAD - CONV_HIST
    taps = [[o for o in range(first, first + CONV_WIDTH) if o % SUBLANES == r] for r in range(SUBLANES)]
    for r in range(1, SUBLANES):
        span = taps[r][-1] - taps[r][0] + ls
        sbuf[r - 1, :, 0:span, :] = vbuf[:, taps[r][0]:taps[r][0] + span, :]
    nb_c, nt_c = (1, CONV_CHUNK) if ls >= CONV_CHUNK else (CONV_CHUNK // ls, ls)
    n_grp = nt_c // SUBLANES
    for i in range(rows // CONV_CHUNK):
        b0, t0 = (0, i * nt_c) if nb_c == 1 else (i * nb_c, 0)
        for c0 in range(0, D_CONV, LANES):
            cs = slice(c0, c0 + LANES)
            accs = [jnp.zeros((nb_c, SUBLANES, LANES), F32) + bdw_ref[:, cs] for _ in range(n_grp)]
            for r in range(SUBLANES):
                ws = [wdw_ref[o - first:o - first + 1, cs] for o in taps[r]]
                for q in range(n_grp + len(taps[r]) - 1):
                    lo = t0 + q * SUBLANES
                    if r == 0:
                        val = vbuf[b0:b0 + nb_c, taps[0][0] + lo:taps[0][0] + lo + SUBLANES, cs]
                    else:
                        val = sbuf[r - 1, b0:b0 + nb_c, lo:lo + SUBLANES, cs]
                    for a in range(len(taps[r])):
                        if 0 <= q - a < n_grp:
                            accs[q - a] = accs[q - a] + val * ws[a]
            cbuf[:, cs] = jnp.concatenate(accs, axis=1).reshape(CONV_CHUNK, LANES)
        yf = cbuf[...]
        mu = jnp.mean(yf, axis=-1, keepdims=True)
        yc = yf - mu
        var = jnp.mean(yc * yc, axis=-1, keepdims=True)
        yn = yc * lax.rsqrt(var + LN_EPS) * lng_ref[...] + lnb_ref[...]
        ybuf[i * CONV_CHUNK:(i + 1) * CONV_CHUNK, :] = (yn * jax.nn.sigmoid(yn)).astype(BF16)
    y_conv = _dot(ybuf[...], wco_ref[...])
    new_conv = vbuf[:, CONV_PAD + ls - CONV_HIST:CONV_PAD + ls, :]
    vbuf[:, CONV_PAD - CONV_HIST:CONV_PAD, :] = new_conv

    @pl.when(j == nj - 1)
    def _():
        npool_ref[...] = new_pool
        nconv_ref[...] = new_conv

    merged = jax.nn.sigmoid(gate_pool) * y_pool + jax.nn.sigmoid(gate_conv) * y_conv
    x1 = x + _dot(merged.astype(BF16), wout_ref[...])
    x1_ref[...] = x1

    h2 = _rmsnorm(x1, gffn_ref[...])
    h2_hi = h2.astype(BF16)
    h2_lo = (h2 - h2_hi.astype(F32)).astype(BF16)
    hi = _dot(h2_hi, wrc_ref[...])
    logits = (hi[:, 0:ROUTER_COLS] + (hi[:, ROUTER_COLS:] + _dot(h2_lo, wrc_ref[:, 0:ROUTER_COLS]))) + br_ref[...]
    route, xs, nblk = _dispatch(h2_hi, *_routing(logits))
    route_ref[...] = route
    xs_ref[...] = xs
    nblk_ref[...] = jnp.broadcast_to(nblk, (SUBLANES, LANES)).astype(I32)


def _const_spec(shape):
    nd = len(shape)
    return pl.BlockSpec(shape, lambda t: (0,) * nd, pipeline_mode=pl.Buffered(1))


def _mixer(x, hist_pool, hist_conv, past, sb, ls, wts, n_tiles, tile0, shared):
    nb, seq, _ = x.shape
    assert nb % sb == 0 and seq % ls == 0 and sb * ls == TOK_TILE and ls % SUBLANES == 0 and ls >= CONV_HIST
    nj = seq // ls
    n_valid = (nb // sb) * nj
    n_steps = n_valid if shared is not None else n_tiles - tile0
    own = lambda t: jnp.minimum(t, n_valid - 1)
    hist_spec = lambda n, width: pl.BlockSpec((sb, n, width), lambda t: (own(t) // nj, 0, 0))
    in_specs = [pl.BlockSpec((sb, ls, D_MODEL), lambda t: (own(t) // nj, own(t) % nj, 0)),
                hist_spec(POOL_HIST, D_POOL), hist_spec(CONV_HIST, D_CONV)]
    in_specs += [_const_spec(w.shape) for w in wts]
    shared_shapes = (
        jax.ShapeDtypeStruct((n_tiles * TOK_TILE, D_MODEL), F32),
        jax.ShapeDtypeStruct((n_tiles * TOK_TILE, ROUTER_COLS), F32),
        jax.ShapeDtypeStruct((n_tiles, TILE_ROWS, D_MODEL), BF16),
        jax.ShapeDtypeStruct((n_tiles, SUBLANES, LANES), I32),
    )
    out_shape = shared_shapes + (jax.ShapeDtypeStruct((nb, POOL_HIST, D_POOL), F32),
                                 jax.ShapeDtypeStruct((nb, CONV_HIST, D_CONV), F32))
    out_specs = (pl.BlockSpec((TOK_TILE, D_MODEL), lambda t: (tile0 + t, 0)),
                 pl.BlockSpec((TOK_TILE, ROUTER_COLS), lambda t: (tile0 + t, 0)),
                 pl.BlockSpec((None, TILE_ROWS, D_MODEL), lambda t: (tile0 + t, 0, 0)),
                 pl.BlockSpec((None, SUBLANES, LANES), lambda t: (tile0 + t, 0, 0)),
                 hist_spec(POOL_HIST, D_POOL), hist_spec(CONV_HIST, D_CONV))
    args = [x, hist_pool, hist_conv, *wts]
    aliases = {}
    if shared is not None:
        in_specs += [pl.BlockSpec(memory_space=pl.ANY)] * len(shared)
        aliases = {len(args) + k: k for k in range(len(shared))}
        args += list(shared)
    rows = sb * ls
    return pl.pallas_call(
        functools.partial(_mixer_kernel, past, sb, ls, nj, n_valid, n_steps),
        grid=(n_steps,),
        in_specs=in_specs,
        out_specs=out_specs,
        out_shape=out_shape,
        input_output_aliases=aliases,
        scratch_shapes=[pltpu.VMEM((sb, POOL_PAD + ls, D_POOL), F32),
                        pltpu.VMEM((sb, CONV_PAD + ls, D_CONV), F32),
                        pltpu.VMEM((SUBLANES - 1, sb, CONV_PAD + ls, D_CONV), F32),
                        pltpu.VMEM((CONV_CHUNK, D_CONV), F32),
                        pltpu.VMEM((rows, D_CONV), BF16)],
        compiler_params=pltpu.CompilerParams(dimension_semantics=("arbitrary",),
                                             vmem_limit_bytes=VMEM_LIMIT),
        name="mixer",
    )(*args)


def _chunk_tables(nblk, n_chunks_max):
    n_tiles = nblk.shape[0]
    e_ids = jnp.arange(N_EXPERTS, dtype=I32)
    per_e = jnp.sum(nblk, axis=0)
    chunks_e = -(-per_e // CHUNK_BLOCKS)
    cend = jnp.cumsum(chunks_e)
    cstart = cend - chunks_e
    n_chunks = cend[-1]
    c = jnp.arange(n_chunks_max, dtype=I32)
    chunk_e = jnp.minimum(jnp.sum((cend[None, :] <= c[:, None]).astype(I32), axis=1), N_EXPERTS - 1)
    is_e = chunk_e[:, None] == e_ids[None, :]
    of_chunk = lambda v: jnp.sum(jnp.where(is_e, v[None, :], 0), axis=1)
    of_chunk_t = lambda a: jnp.sum(jnp.where(is_e[:, None, :], a[None, :, :], 0), axis=2)
    cum_t = jnp.cumsum(nblk, axis=0)
    bstart = jnp.cumsum(nblk, axis=1) - nblk
    k = ((c - of_chunk(cstart)) * CHUNK_BLOCKS)[:, None] + jnp.arange(CHUNK_BLOCKS, dtype=I32)[None, :]
    tile = jnp.sum((of_chunk_t(cum_t)[:, None, :] <= k[:, :, None]).astype(I32), axis=2)
    is_t = tile[:, :, None] == jnp.arange(n_tiles, dtype=I32)[None, None, :]
    shift = of_chunk_t(bstart - (cum_t - nblk))
    blk = k + jnp.sum(jnp.where(is_t, shift[:, None, :], 0), axis=2)
    valid = (k < of_chunk(per_e)[:, None]) & (c[:, None] < n_chunks)
    table = jnp.where(valid, tile * TILE_BLOCKS + blk, -1)
    return table.reshape(-1).astype(I32), chunk_e.astype(I32), n_chunks.reshape(1).astype(I32)


def _expert_kernel(spare0, tab_ref, ce_ref, nch_ref, x_hbm, wg_ref, wu_ref, wd_ref, y_hbm,
                   xin0, xin1, yout0, yout1, wgu_bf, wd_bf, sem_in, sem_out):
    c = pl.program_id(0)
    nc = pl.num_programs(0)
    nch = nch_ref[0]

    zero_block = spare0 + 2 * CHUNK_BLOCKS

    def src_block(chunk, s):
        t = tab_ref[chunk * CHUNK_BLOCKS + s]
        return jnp.where(t < 0, zero_block, t)

    def dst_block(chunk, s):
        t = tab_ref[jnp.maximum(chunk, 0) * CHUNK_BLOCKS + s]
        return jnp.where((t < 0) | (chunk < 0), spare0 + (chunk % 2) * CHUNK_BLOCKS + s, t)

    def in_copy(block, buf, sl, s):
        return pltpu.make_async_copy(x_hbm.at[block], buf.at[pl.ds(s * BLK, BLK), :], sem_in.at[sl])

    def out_copy(block, buf, sl, s):
        return pltpu.make_async_copy(buf.at[pl.ds(s * BLK, BLK), :], y_hbm.at[block], sem_out.at[sl])

    def step(sl):
        xin, xin_o = (xin0, xin1) if sl == 0 else (xin1, xin0)
        yout, yout_o = (yout0, yout1) if sl == 0 else (yout1, yout0)

        nxt = jnp.minimum(c + 1, nc - 1)
        for s in range(CHUNK_BLOCKS):
            in_copy(jnp.where(c + 1 < nch, src_block(nxt, s), zero_block), xin_o, 1 - sl, s).start()
        for s in range(CHUNK_BLOCKS):
            in_copy(src_block(c, s), xin, sl, s).wait()

        ab = _dot(xin[...], wgu_bf[...])

        for s in range(CHUNK_BLOCKS):
            out_copy(dst_block(c - 1, s), yout_o, 1 - sl, s).start()

        a = ab[:, 0:D_EXPERT]
        b = ab[:, D_EXPERT:2 * D_EXPERT]
        act = ((a * jax.nn.sigmoid(a)) * b).astype(BF16)

        for s in range(CHUNK_BLOCKS):
            out_copy(dst_block(c - 2, s), yout, sl, s).wait()
        yout[...] = _dot(act, wd_bf[...]).astype(BF16)

        @pl.when(c == nch - 1)
        def _():
            for s in range(CHUNK_BLOCKS):
                out_copy(dst_block(c, s), yout, sl, s).start()
            for s in range(CHUNK_BLOCKS):
                in_copy(zero_block, xin_o, 1 - sl, s).wait()
            for s in range(CHUNK_BLOCKS):
                out_copy(dst_block(c - 1, s), yout_o, 1 - sl, s).wait()
            for s in range(CHUNK_BLOCKS):
                out_copy(dst_block(c, s), yout, sl, s).wait()

    @pl.when(c < nch)
    def _():
        @pl.when(c == 0)
        def _():
            yout0[...] = jnp.zeros(yout0.shape, yout0.dtype)
            yout1[...] = jnp.zeros(yout1.shape, yout1.dtype)
            for s in range(CHUNK_BLOCKS):
                out_copy(dst_block(c - 2, s), yout0, 0, s).start()
            for s in range(CHUNK_BLOCKS):
                in_copy(src_block(0, s), xin0, 0, s).start()

        @pl.when((c == 0) | (ce_ref[c] != ce_ref[jnp.maximum(c - 1, 0)]))
        def _():
            wgu_bf[:, 0:D_EXPERT] = wg_ref[...].astype(BF16)
            wgu_bf[:, D_EXPERT:2 * D_EXPERT] = wu_ref[...].astype(BF16)
            wd_bf[...] = wd_ref[...].astype(BF16)

        @pl.when(c % 2 == 0)
        def _():
            step(0)

        @pl.when(c % 2 == 1)
        def _():
            step(1)


def _experts(xs_blocks, n_used_blocks, table, chunk_e, n_chunks, w_gate, w_up, w_down):
    assert xs_blocks.shape[0] - n_used_blocks > 2 * CHUNK_BLOCKS
    n_steps = chunk_e.shape[0]
    w_spec = lambda shape: pl.BlockSpec((None,) + shape, lambda c, tab, ce, nch: (ce[c], 0, 0))
    return pl.pallas_call(
        functools.partial(_expert_kernel, n_used_blocks),
        grid_spec=pltpu.PrefetchScalarGridSpec(
            num_scalar_prefetch=3,
            grid=(n_steps,),
            in_specs=[pl.BlockSpec(memory_space=pl.ANY),
                      w_spec((D_MODEL, D_EXPERT)), w_spec((D_MODEL, D_EXPERT)), w_spec((D_EXPERT, D_MODEL))],
            out_specs=pl.BlockSpec(memory_space=pl.ANY),
            scratch_shapes=[pltpu.VMEM((CHUNK_ROWS, D_MODEL), BF16),
                            pltpu.VMEM((CHUNK_ROWS, D_MODEL), BF16),
                            pltpu.VMEM((CHUNK_ROWS, D_MODEL), BF16),
                            pltpu.VMEM((CHUNK_ROWS, D_MODEL), BF16),
                            pltpu.VMEM((D_MODEL, 2 * D_EXPERT), BF16),
                            pltpu.VMEM((D_EXPERT, D_MODEL), BF16),
                            pltpu.SemaphoreType.DMA((2,)),
                            pltpu.SemaphoreType.DMA((2,))]),
        out_shape=jax.ShapeDtypeStruct(xs_blocks.shape, xs_blocks.dtype),
        input_output_aliases={3: 0},
        compiler_params=pltpu.CompilerParams(dimension_semantics=("arbitrary",),
                                             vmem_limit_bytes=VMEM_LIMIT),
        name="experts",
    )(table, chunk_e, n_chunks, xs_blocks, w_gate, w_up, w_down)


def _combine_kernel(tiles, y_ref, route_ref, x1_ref, gfin_ref, o_ref):
    for k in range(tiles):
        rs = slice(k * TOK_TILE, (k + 1) * TOK_TILE)
        route = route_ref[rs, :]
        pos1 = route[:, ROUTE_POS1:ROUTE_POS1 + 1].astype(I32)
        pos2 = route[:, ROUTE_POS2:ROUTE_POS2 + 1].astype(I32)
        w1 = route[:, ROUTE_W1:ROUTE_W1 + 1]
        w2 = route[:, ROUTE_W2:ROUTE_W2 + 1]
        r_idx = lax.broadcasted_iota(I32, (TOK_TILE, TILE_ROWS), 1)
        gather = jnp.where(r_idx == pos1, w1, jnp.where(r_idx == pos2, w2, 0.0)).astype(BF16)
        y = y_ref[k * TILE_BLOCKS:(k + 1) * TILE_BLOCKS].reshape(TILE_ROWS, D_MODEL)
        o_ref[rs, :] = _rmsnorm(x1_ref[rs, :] + _dot(gather, y), gfin_ref[...])


def _combine(y_blocks, route, x1, g_final, tile0, n_tiles, tiles_per_step):
    assert n_tiles % tiles_per_step == 0 and tile0 % tiles_per_step == 0
    step0 = tile0 // tiles_per_step
    tok_spec = lambda width: pl.BlockSpec((tiles_per_step * TOK_TILE, width), lambda i: (step0 + i, 0))
    return pl.pallas_call(
        functools.partial(_combine_kernel, tiles_per_step),
        grid=(n_tiles // tiles_per_step,),
        in_specs=[pl.BlockSpec((tiles_per_step * TILE_BLOCKS, BLK, D_MODEL), lambda i: (step0 + i, 0, 0)),
                  tok_spec(ROUTER_COLS), tok_spec(D_MODEL),
                  pl.BlockSpec((1, D_MODEL), lambda i: (0, 0))],
        out_specs=pl.BlockSpec((tiles_per_step * TOK_TILE, D_MODEL), lambda i: (i, 0)),
        out_shape=jax.ShapeDtypeStruct((n_tiles * TOK_TILE, D_MODEL), F32),
        compiler_params=pltpu.CompilerParams(dimension_semantics=("arbitrary",),
                                             vmem_limit_bytes=VMEM_LIMIT),
        name="combine",
    )(y_blocks, route, x1, g_final)


def kernel(x_prompt, x_sample, state_pool, state_conv, g_mix, w_in, w_pool_grp, pool_scale, w_pool_out, w_dw, b_dw, ln_g, ln_b, w_conv_out, w_out, g_ffn, w_rg, b_rg, w_re, b_re, w_gate, w_up, w_down, g_final):
    assert g_mix.shape[0] == 1
    l = 0
    bp, seq_p, _ = x_prompt.shape
    bs, seq_s, _ = x_sample.shape
    tiles_p = bp * seq_p // TOK_TILE
    tiles_s = bs * seq_s // TOK_TILE
    n_tiles = tiles_p + tiles_s

    n_pad = ROUTER_COLS - N_EXPERTS - N_EXPERT_GROUPS
    w_r = jnp.concatenate([w_re[l], w_rg[l], jnp.zeros((D_MODEL, n_pad), F32)], axis=1)
    b_r = jnp.concatenate([b_re[l], b_rg[l], jnp.zeros((n_pad,), F32)]).reshape(1, ROUTER_COLS)
    w_r_hi = w_r.astype(BF16)
    w_r_cat = jnp.concatenate([w_r_hi, (w_r - w_r_hi.astype(F32)).astype(BF16)], axis=1)
    row = lambda a: a.reshape(1, -1)
    wts = (row(g_mix[l]), w_in[l].astype(BF16), w_pool_grp[l].astype(BF16), row(pool_scale[l]),
           w_pool_out[l].astype(BF16), w_dw[l], row(b_dw[l]), row(ln_g[l]), row(ln_b[l]),
           w_conv_out[l].astype(BF16), w_out[l].astype(BF16), row(g_ffn[l]), w_r_cat, b_r)

    zp = jnp.zeros((bp, POOL_HIST, D_POOL), F32)
    zc = jnp.zeros((bp, CONV_HIST, D_CONV), F32)
    *shared, npp, ncp = _mixer(x_prompt, zp, zc, 0, 1, TOK_TILE, wts, n_tiles + 1, 0, None)
    x1, route, xs, nblk, nps, ncs = _mixer(x_sample, state_pool[l], state_conv[l], PAST_LEN, bs, seq_s, wts,
                                           n_tiles + 1, tiles_p, shared)

    n_chunks_max = -(-n_tiles * TILE_BLOCKS // CHUNK_BLOCKS) + N_EXPERTS
    table, chunk_e, n_chunks = _chunk_tables(nblk[:n_tiles, 0, :N_EXPERTS], n_chunks_max)
    y_blocks = _experts(xs.reshape((n_tiles + 1) * TILE_BLOCKS, BLK, D_MODEL), n_tiles * TILE_BLOCKS,
                        table, chunk_e, n_chunks,
                        w_gate[l].reshape(N_EXPERTS, D_MODEL, D_EXPERT),
                        w_up[l].reshape(N_EXPERTS, D_MODEL, D_EXPERT),
                        w_down[l].reshape(N_EXPERTS, D_EXPERT, D_MODEL))
    tps = COMBINE_TILES if tiles_p % COMBINE_TILES == 0 else 1
    yp = _combine(y_blocks, route, x1, row(g_final), 0, tiles_p, tps)
    ys = _combine(y_blocks, route, x1, row(g_final), tiles_p, tiles_s, 1)
    return (yp.reshape(bp, seq_p, D_MODEL), ys.reshape(bs, seq_s, D_MODEL),
            npp[None], ncp[None], nps[None], ncs[None])
```

```python
import functools

import jax
import jax.numpy as jnp
from jax import lax
from jax.experimental import pallas as pl
from jax.experimental.pallas import tpu as pltpu

D_MODEL = 1024
D_POOL = 512
N_POOL_GROUPS = 4
POOL_GROUP = D_POOL // N_POOL_GROUPS
POOL_WINDOWS = (2, 4, 8, 16)
POOL_HIST = max(POOL_WINDOWS) - 1
D_CONV = 512
CONV_WIDTH = 31
CONV_HIST = CONV_WIDTH - 1
N_EXPERT_GROUPS = 4
EXPERTS_PER_GROUP = 8
N_EXPERTS = N_EXPERT_GROUPS * EXPERTS_PER_GROUP
TOP_K = 2
D_EXPERT = 256
RMS_EPS = 1e-6
LN_EPS = 1e-5
PAST_LEN = 1024

LANES = 128
SUBLANES = 8
POOL_PAD = 16
CONV_PAD = 32
CONV_CHUNK = 64
ROUTER_COLS = LANES
VMEM_LIMIT = 56 * 1024 * 1024

TOK_TILE = 512
BLK = 2 * SUBLANES
TILE_BLOCKS = (TOP_K * TOK_TILE + N_EXPERTS * (BLK - 1)) // BLK
TILE_ROWS = TILE_BLOCKS * BLK
CHUNK_BLOCKS = 32
CHUNK_ROWS = CHUNK_BLOCKS * BLK
GATHER_SLOTS = 3
ROUTE_POS1, ROUTE_POS2, ROUTE_W1, ROUTE_W2 = 0, 1, 2, 3
COMBINE_TILES = 2

BF16 = jnp.bfloat16
F32 = jnp.float32
I32 = jnp.int32


def _dot(a, b):
    return jnp.dot(a, b, preferred_element_type=F32)


def _rmsnorm(x, g):
    return x * lax.rsqrt(jnp.mean(x * x, axis=-1, keepdims=True) + RMS_EPS) * g


def _routing(logits):
    lane = lax.broadcasted_iota(I32, logits.shape, 1)
    lane_f = lane.astype(F32)
    neg = jnp.float32(-jnp.inf)
    big = jnp.float32(1e9)
    is_grp = (lane >= N_EXPERTS) & (lane < N_EXPERTS + N_EXPERT_GROUPS)
    glog = jnp.where(is_grp, logits, neg)
    gmax = jnp.max(glog, axis=1, keepdims=True)
    gidx = jnp.min(jnp.where(glog == gmax, lane_f, big), axis=1, keepdims=True) - float(N_EXPERTS)
    gsum = jnp.sum(jnp.where(is_grp, jnp.exp(glog - gmax), 0.0), axis=1, keepdims=True)
    p_sel = 1.0 / gsum
    lane_grp = (lane >> 3).astype(F32)
    is_sel = (lane < N_EXPERTS) & (lane_grp == gidx)
    elog = jnp.where(is_sel, logits, neg)
    v1 = jnp.max(elog, axis=1, keepdims=True)
    i1 = jnp.min(jnp.where(elog == v1, lane_f, big), axis=1, keepdims=True)
    elog2 = jnp.where(lane_f == i1, neg, elog)
    v2 = jnp.max(elog2, axis=1, keepdims=True)
    i2 = jnp.min(jnp.where(elog2 == v2, lane_f, big), axis=1, keepdims=True)
    t = jnp.exp(v2 - v1)
    return i1, i2, p_sel / (1.0 + t), p_sel * (t / (1.0 + t))


def _dispatch(h2, i1, i2, w1, w2):
    rows = h2.shape[0]
    lane = lax.broadcasted_iota(I32, (rows, LANES), 1)
    lane_f = lane.astype(F32)
    hit1 = lane_f == i1
    hit2 = lane_f == i2
    sel = jnp.where(hit1 | hit2, 1.0, 0.0)
    earlier = lax.broadcasted_iota(I32, (rows, rows), 1) < lax.broadcasted_iota(I32, (rows, rows), 0)
    rank = _dot(jnp.where(earlier, 1.0, 0.0).astype(BF16), sel.astype(BF16))
    cnt = jnp.sum(sel, axis=0, keepdims=True)
    nblk = jnp.floor((cnt + float(BLK - 1)) * (1.0 / BLK))
    before = lax.broadcasted_iota(I32, (LANES, LANES), 0) < lax.broadcasted_iota(I32, (LANES, LANES), 1)
    bstart = _dot(jnp.broadcast_to(nblk, (SUBLANES, LANES)).astype(BF16),
                  jnp.where(before, 1.0, 0.0).astype(BF16))[0:1, :]
    pos = float(BLK) * bstart + rank
    pos1 = jnp.sum(jnp.where(hit1, pos, 0.0), axis=1, keepdims=True)
    pos2 = jnp.sum(jnp.where(hit2, pos, 0.0), axis=1, keepdims=True)
    route = jnp.where(lane == ROUTE_POS1, pos1,
                      jnp.where(lane == ROUTE_POS2, pos2,
                                jnp.where(lane == ROUTE_W1, w1, jnp.where(lane == ROUTE_W2, w2, 0.0))))
    route_t = route.T
    p1 = route_t[ROUTE_POS1:ROUTE_POS1 + 1, :].astype(I32)
    p2 = route_t[ROUTE_POS2:ROUTE_POS2 + 1, :].astype(I32)
    r_idx = lax.broadcasted_iota(I32, (TILE_ROWS, rows), 0)
    onehot = jnp.where((r_idx == p1) | (r_idx == p2), 1.0, 0.0).astype(BF16)
    return route, _dot(onehot, h2).astype(BF16), nblk


N_MIXER_IN = 17
N_MIXER_OUT = 6
N_MIXER_SCRATCH = 5


def _mixer_kernel(past, sb, ls, nj, n_valid, n_steps, *refs):
    t = pl.program_id(0)
    if n_valid == n_steps:
        _mixer_tile(past, sb, ls, t % nj, nj, *refs)
        return
    outs = refs[-(N_MIXER_OUT + N_MIXER_SCRATCH):][:N_MIXER_OUT]

    @pl.when(t < n_valid)
    def _():
        _mixer_tile(past, sb, ls, t % nj, nj, *refs)

    @pl.when(t >= n_valid)
    def _():
        for ref in outs[:4]:
            ref[...] = jnp.zeros(ref.shape, ref.dtype)


def _mixer_tile(past, sb, ls, j, nj, *refs):
    (x_ref, hp_ref, hc_ref, gmix_ref, win_ref, wgrp_ref, pscale_ref, wpo_ref, wdw_ref, bdw_ref,
     lng_ref, lnb_ref, wco_ref, wout_ref, gffn_ref, wrc_ref, br_ref) = refs[:N_MIXER_IN]
    x1_ref, route_ref, xs_ref, nblk_ref, npool_ref, nconv_ref = refs[-(N_MIXER_OUT + N_MIXER_SCRATCH):][:N_MIXER_OUT]
    ubuf, vbuf, sbuf, cbuf, ybuf = refs[-N_MIXER_SCRATCH:]
    rows = sb * ls

    @pl.when(j == 0)
    def _():
        ubuf[:, POOL_PAD - POOL_HIST:POOL_PAD, :] = hp_ref[...]
        vbuf[:, CONV_PAD - CONV_HIST:CONV_PAD, :] = hc_ref[...]

    x = x_ref[...].reshape(rows, D_MODEL)
    h = _rmsnorm(x, gmix_ref[...]).astype(BF16)
    u = _dot(h, win_ref[:, 0:D_POOL])
    ubuf[:, POOL_PAD:POOL_PAD + ls, :] = u.reshape(sb, ls, D_POOL)
    ga = _dot(h, win_ref[:, D_POOL:D_POOL + D_CONV])
    gb = _dot(h, win_ref[:, D_POOL + D_CONV:D_POOL + 2 * D_CONV])
    vbuf[:, CONV_PAD:CONV_PAD + ls, :] = (ga * jax.nn.sigmoid(gb)).reshape(sb, ls, D_CONV)
    c_gp = D_POOL + 2 * D_CONV
    gate_pool = _dot(h, win_ref[:, c_gp:c_gp + D_MODEL])
    gate_conv = _dot(h, win_ref[:, c_gp + D_MODEL:c_gp + 2 * D_MODEL])

    t_idx = lax.broadcasted_iota(I32, (sb, ls, 1), 1)
    frames = (past + 1 + j * ls + t_idx).astype(F32)
    ys = []
    part, width = ubuf[...], 1
    for gi, w in enumerate(POOL_WINDOWS):
        c0, c1 = gi * POOL_GROUP, (gi + 1) * POOL_GROUP
        cur = ubuf[:, POOL_PAD:POOL_PAD + ls, c0:c1]
        while width < w:
            part = part + pltpu.roll(part, width, 1)
            width *= 2
        s = part[:, POOL_PAD:POOL_PAD + ls, 0:POOL_GROUP]
        part = part[:, :, POOL_GROUP:]
        mean = s / jnp.minimum(jnp.float32(w), frames)
        ys.append(_dot((mean - cur).reshape(rows, POOL_GROUP).astype(BF16), wgrp_ref[gi]))
    yp = (jnp.concatenate(ys, axis=1) * pscale_ref[...]).astype(BF16)
    y_pool = _dot(yp, wpo_ref[...])
    new_pool = ubuf[:, POOL_PAD + ls - POOL_HIST:POOL_PAD + ls, :]
    ubuf[:, POOL_PAD - POOL_HIST:POOL_PAD, :] = new_pool

    first = CONV_PAD - CONV_HIST
    taps = [[o for o in range(first, first + CONV_WIDTH) if o % SUBLANES == r] for r in range(SUBLANES)]
    for r in range(1, SUBLANES):
        span = taps[r][-1] - taps[r][0] + ls
        sbuf[r - 1, :, 0:span, :] = vbuf[:, taps[r][0]:taps[r][0] + span, :]
    nb_c, nt_c = (1, CONV_CHUNK) if ls >= CONV_CHUNK else (CONV_CHUNK // ls, ls)
    n_grp = nt_c // SUBLANES
    for i in range(rows // CONV_CHUNK):
        b0, t0 = (0, i * nt_c) if nb_c == 1 else (i * nb_c, 0)
        for c0 in range(0, D_CONV, LANES):
            cs = slice(c0, c0 + LANES)
            accs = [jnp.zeros((nb_c, SUBLANES, LANES), F32) + bdw_ref[:, cs] for _ in range(n_grp)]
            for r in range(SUBLANES):
                ws = [wdw_ref[o - first:o - first + 1, cs] for o in taps[r]]
                for q in range(n_grp + len(taps[r]) - 1):
                    lo = t0 + q * SUBLANES
                    if r == 0:
                        val = vbuf[b0:b0 + nb_c, taps[0][0] + lo:taps[0][0] + lo + SUBLANES, cs]
                    else:
                        val = sbuf[r - 1, b0:b0 + nb_c, lo:lo + SUBLANES, cs]
                    for a in range(len(taps[r])):
                        if 0 <= q - a < n_grp:
                            accs[q - a] = accs[q - a] + val * ws[a]
            cbuf[:, cs] = jnp.concatenate(accs, axis=1).reshape(CONV_CHUNK, LANES)
        yf = cbuf[...]
        mu = jnp.mean(yf, axis=-1, keepdims=True)
        yc = yf - mu
        var = jnp.mean(yc * yc, axis=-1, keepdims=True)
        yn = yc * lax.rsqrt(var + LN_EPS) * lng_ref[...] + lnb_ref[...]
        ybuf[i * CONV_CHUNK:(i + 1) * CONV_CHUNK, :] = (yn * jax.nn.sigmoid(yn)).astype(BF16)
    y_conv = _dot(ybuf[...], wco_ref[...])
    new_conv = vbuf[:, CONV_PAD + ls - CONV_HIST:CONV_PAD + ls, :]
    vbuf[:, CONV_PAD - CONV_HIST:CONV_PAD, :] = new_conv

    @pl.when(j == nj - 1)
    def _():
        npool_ref[...] = new_pool
        nconv_ref[...] = new_conv

    merged = jax.nn.sigmoid(gate_pool) * y_pool + jax.nn.sigmoid(gate_conv) * y_conv
    x1 = x + _dot(merged.astype(BF16), wout_ref[...])
    x1_ref[...] = x1

    h2 = _rmsnorm(x1, gffn_ref[...])
    h2_hi = h2.astype(BF16)
    h2_lo = (h2 - h2_hi.astype(F32)).astype(BF16)
    hi = _dot(h2_hi, wrc_ref[...])
    logits = (hi[:, 0:ROUTER_COLS] + (hi[:, ROUTER_COLS:] + _dot(h2_lo, wrc_ref[:, 0:ROUTER_COLS]))) + br_ref[...]
    route, xs, nblk = _dispatch(h2_hi, *_routing(logits))
    route_ref[...] = route
    xs_ref[...] = xs
    nblk_ref[...] = jnp.broadcast_to(nblk, (SUBLANES, LANES)).astype(I32)


def _const_spec(shape):
    nd = len(shape)
    return pl.BlockSpec(shape, lambda t: (0,) * nd, pipeline_mode=pl.Buffered(1))


def _mixer(x, hist_pool, hist_conv, past, sb, ls, wts, n_tiles, tile0, shared):
    nb, seq, _ = x.shape
    assert nb % sb == 0 and seq % ls == 0 and sb * ls == TOK_TILE and ls % SUBLANES == 0 and ls >= CONV_HIST
    nj = seq // ls
    n_valid = (nb // sb) * nj
    n_steps = n_valid if shared is not None else n_tiles - tile0
    own = lambda t: jnp.minimum(t, n_valid - 1)
    hist_spec = lambda n, width: pl.BlockSpec((sb, n, width), lambda t: (own(t) // nj, 0, 0))
    in_specs = [pl.BlockSpec((sb, ls, D_MODEL), lambda t: (own(t) // nj, own(t) % nj, 0)),
                hist_spec(POOL_HIST, D_POOL), hist_spec(CONV_HIST, D_CONV)]
    in_specs += [_const_spec(w.shape) for w in wts]
    shared_shapes = (
        jax.ShapeDtypeStruct((n_tiles * TOK_TILE, D_MODEL), F32),
        jax.ShapeDtypeStruct((n_tiles * TOK_TILE, ROUTER_COLS), F32),
        jax.ShapeDtypeStruct((n_tiles, TILE_ROWS, D_MODEL), BF16),
        jax.ShapeDtypeStruct((n_tiles, SUBLANES, LANES), I32),
    )
    out_shape = shared_shapes + (jax.ShapeDtypeStruct((nb, POOL_HIST, D_POOL), F32),
                                 jax.ShapeDtypeStruct((nb, CONV_HIST, D_CONV), F32))
    out_specs = (pl.BlockSpec((TOK_TILE, D_MODEL), lambda t: (tile0 + t, 0)),
                 pl.BlockSpec((TOK_TILE, ROUTER_COLS), lambda t: (tile0 + t, 0)),
                 pl.BlockSpec((None, TILE_ROWS, D_MODEL), lambda t: (tile0 + t, 0, 0)),
                 pl.BlockSpec((None, SUBLANES, LANES), lambda t: (tile0 + t, 0, 0)),
                 hist_spec(POOL_HIST, D_POOL), hist_spec(CONV_HIST, D_CONV))
    args = [x, hist_pool, hist_conv, *wts]
    aliases = {}
    if shared is not None:
        in_specs += [pl.BlockSpec(memory_space=pl.ANY)] * len(shared)
        aliases = {len(args) + k: k for k in range(len(shared))}
        args += list(shared)
    rows = sb * ls
    return pl.pallas_call(
        functools.partial(_mixer_kernel, past, sb, ls, nj, n_valid, n_steps),
        grid=(n_steps,),
        in_specs=in_specs,
        out_specs=out_specs,
        out_shape=out_shape,
        input_output_aliases=aliases,
        scratch_shapes=[pltpu.VMEM((sb, POOL_PAD + ls, D_POOL), F32),
                        pltpu.VMEM((sb, CONV_PAD + ls, D_CONV), F32),
                        pltpu.VMEM((SUBLANES - 1, sb, CONV_PAD + ls, D_CONV), F32),
                        pltpu.VMEM((CONV_CHUNK, D_CONV), F32),
                        pltpu.VMEM((rows, D_CONV), BF16)],
        compiler_params=pltpu.CompilerParams(dimension_semantics=("arbitrary",),
                                             vmem_limit_bytes=VMEM_LIMIT),
        name="mixer",
    )(*args)


def _chunk_tables(nblk, n_chunks_max):
    n_tiles = nblk.shape[0]
    e_ids = jnp.arange(N_EXPERTS, dtype=I32)
    per_e = jnp.sum(nblk, axis=0)
    chunks_e = -(-per_e // CHUNK_BLOCKS)
    cend = jnp.cumsum(chunks_e)
    cstart = cend - chunks_e
    n_chunks = cend[-1]
    c = jnp.arange(n_chunks_max, dtype=I32)
    chunk_e = jnp.minimum(jnp.sum((cend[None, :] <= c[:, None]).astype(I32), axis=1), N_EXPERTS - 1)
    is_e = chunk_e[:, None] == e_ids[None, :]
    of_chunk = lambda v: jnp.sum(jnp.where(is_e, v[None, :], 0), axis=1)
    of_chunk_t = lambda a: jnp.sum(jnp.where(is_e[:, None, :], a[None, :, :], 0), axis=2)
    cum_t = jnp.cumsum(nblk, axis=0)
    bstart = jnp.cumsum(nblk, axis=1) - nblk
    k = ((c - of_chunk(cstart)) * CHUNK_BLOCKS)[:, None] + jnp.arange(CHUNK_BLOCKS, dtype=I32)[None, :]
    tile = jnp.sum((of_chunk_t(cum_t)[:, None, :] <= k[:, :, None]).astype(I32), axis=2)
    is_t = tile[:, :, None] == jnp.arange(n_tiles, dtype=I32)[None, None, :]
    shift = of_chunk_t(bstart - (cum_t - nblk))
    blk = k + jnp.sum(jnp.where(is_t, shift[:, None, :], 0), axis=2)
    valid = (k < of_chunk(per_e)[:, None]) & (c[:, None] < n_chunks)
    table = jnp.where(valid, tile * TILE_BLOCKS + blk, -1)
    return table.reshape(-1).astype(I32), chunk_e.astype(I32), n_chunks.reshape(1).astype(I32)


def _expert_kernel(spare0, tab_ref, ce_ref, nch_ref, x_hbm, wg_ref, wu_ref, wd_ref, y_hbm,
                   xin, yout0, yout1, wgu_bf, wd_bf, sem_in, sem_out):
    c = pl.program_id(0)
    nc = pl.num_programs(0)
    nch = nch_ref[0]

    zero_block = spare0 + 2 * CHUNK_BLOCKS

    def src_block(chunk, s):
        t = tab_ref[jnp.minimum(chunk, nc - 1) * CHUNK_BLOCKS + s]
        return jnp.where((t < 0) | (chunk >= nch), zero_block, t)

    def dst_block(chunk, s):
        t = tab_ref[jnp.maximum(chunk, 0) * CHUNK_BLOCKS + s]
        return jnp.where((t < 0) | (chunk < 0), spare0 + (chunk % 2) * CHUNK_BLOCKS + s, t)

    def in_copy(chunk, s):
        slot = chunk % GATHER_SLOTS
        return pltpu.make_async_copy(x_hbm.at[src_block(chunk, s)], xin.at[slot, pl.ds(s * BLK, BLK), :],
                                     sem_in.at[slot])

    def out_copy(block, buf, sl, s):
        return pltpu.make_async_copy(buf.at[pl.ds(s * BLK, BLK), :], y_hbm.at[block], sem_out.at[sl])

    def step(sl):
        yout, yout_o = (yout0, yout1) if sl == 0 else (yout1, yout0)

        for s in range(CHUNK_BLOCKS):
            in_copy(c + GATHER_SLOTS - 1, s).start()
        for s in range(CHUNK_BLOCKS):
            in_copy(c, s).wait()

        ab = _dot(xin[c % GATHER_SLOTS], wgu_bf[...])

        for s in range(CHUNK_BLOCKS):
            out_copy(dst_block(c - 1, s), yout_o, 1 - sl, s).start()

        a = ab[:, 0:D_EXPERT]
        b = ab[:, D_EXPERT:2 * D_EXPERT]
        act = ((a * jax.nn.sigmoid(a)) * b).astype(BF16)

        for s in range(CHUNK_BLOCKS):
            out_copy(dst_block(c - 2, s), yout, sl, s).wait()
        yout[...] = _dot(act, wd_bf[...]).astype(BF16)

        @pl.when(c == nch - 1)
        def _():
            for s in range(CHUNK_BLOCKS):
                out_copy(dst_block(c, s), yout, sl, s).start()
            for ahead in range(1, GATHER_SLOTS):
                for s in range(CHUNK_BLOCKS):
                    in_copy(c + ahead, s).wait()
            for s in range(CHUNK_BLOCKS):
                out_copy(dst_block(c - 1, s), yout_o, 1 - sl, s).wait()
            for s in range(CHUNK_BLOCKS):
                out_copy(dst_block(c, s), yout, sl, s).wait()

    @pl.when(c < nch)
    def _():
        @pl.when(c == 0)
        def _():
            yout0[...] = jnp.zeros(yout0.shape, yout0.dtype)
            yout1[...] = jnp.zeros(yout1.shape, yout1.dtype)
            for s in range(CHUNK_BLOCKS):
                out_copy(dst_block(c - 2, s), yout0, 0, s).start()
            for first in range(GATHER_SLOTS - 1):
                for s in range(CHUNK_BLOCKS):
                    in_copy(c + first, s).start()

        @pl.when((c == 0) | (ce_ref[c] != ce_ref[jnp.maximum(c - 1, 0)]))
        def _():
            wgu_bf[:, 0:D_EXPERT] = wg_ref[...].astype(BF16)
            wgu_bf[:, D_EXPERT:2 * D_EXPERT] = wu_ref[...].astype(BF16)
            wd_bf[...] = wd_ref[...].astype(BF16)

        @pl.when(c % 2 == 0)
        def _():
            step(0)

        @pl.when(c % 2 == 1)
        def _():
            step(1)


def _experts(xs_blocks, n_used_blocks, table, chunk_e, n_chunks, w_gate, w_up, w_down):
    assert xs_blocks.shape[0] - n_used_blocks > 2 * CHUNK_BLOCKS
    n_steps = chunk_e.shape[0]
    w_spec = lambda shape: pl.BlockSpec((None,) + shape, lambda c, tab, ce, nch: (ce[c], 0, 0))
    return pl.pallas_call(
        functools.partial(_expert_kernel, n_used_blocks),
        grid_spec=pltpu.PrefetchScalarGridSpec(
            num_scalar_prefetch=3,
            grid=(n_steps,),
            in_specs=[pl.BlockSpec(memory_space=pl.ANY),
                      w_spec((D_MODEL, D_EXPERT)), w_spec((D_MODEL, D_EXPERT)), w_spec((D_EXPERT, D_MODEL))],
            out_specs=pl.BlockSpec(memory_space=pl.ANY),
            scratch_shapes=[pltpu.VMEM((GATHER_SLOTS, CHUNK_ROWS, D_MODEL), BF16),
                            pltpu.VMEM((CHUNK_ROWS, D_MODEL), BF16),
                            pltpu.VMEM((CHUNK_ROWS, D_MODEL), BF16),
                            pltpu.VMEM((D_MODEL, 2 * D_EXPERT), BF16),
                            pltpu.VMEM((D_EXPERT, D_MODEL), BF16),
                            pltpu.SemaphoreType.DMA((GATHER_SLOTS,)),
                            pltpu.SemaphoreType.DMA((2,))]),
        out_shape=jax.ShapeDtypeStruct(xs_blocks.shape, xs_blocks.dtype),
        input_output_aliases={3: 0},
        compiler_params=pltpu.CompilerParams(dimension_semantics=("arbitrary",),
                                             vmem_limit_bytes=VMEM_LIMIT),
        name="experts",
    )(table, chunk_e, n_chunks, xs_blocks, w_gate, w_up, w_down)


def _combine_kernel(tiles, y_ref, route_ref, x1_ref, gfin_ref, o_ref):
    for k in range(tiles):
        rs = slice(k * TOK_TILE, (k + 1) * TOK_TILE)
        route = route_ref[rs, :]
        pos1 = route[:, ROUTE_POS1:ROUTE_POS1 + 1].astype(I32)
        pos2 = route[:, ROUTE_POS2:ROUTE_POS2 + 1].astype(I32)
        w1 = route[:, ROUTE_W1:ROUTE_W1 + 1]
        w2 = route[:, ROUTE_W2:ROUTE_W2 + 1]
        r_idx = lax.broadcasted_iota(I32, (TOK_TILE, TILE_ROWS), 1)
        gather = jnp.where(r_idx == pos1, w1, jnp.where(r_idx == pos2, w2, 0.0)).astype(BF16)
        y = y_ref[k * TILE_BLOCKS:(k + 1) * TILE_BLOCKS].reshape(TILE_ROWS, D_MODEL)
        o_ref[rs, :] = _rmsnorm(x1_ref[rs, :] + _dot(gather, y), gfin_ref[...])


def _combine(y_blocks, route, x1, g_final, tile0, n_tiles, tiles_per_step):
    assert n_tiles % tiles_per_step == 0 and tile0 % tiles_per_step == 0
    step0 = tile0 // tiles_per_step
    tok_spec = lambda width: pl.BlockSpec((tiles_per_step * TOK_TILE, width), lambda i: (step0 + i, 0))
    return pl.pallas_call(
        functools.partial(_combine_kernel, tiles_per_step),
        grid=(n_tiles // tiles_per_step,),
        in_specs=[pl.BlockSpec((tiles_per_step * TILE_BLOCKS, BLK, D_MODEL), lambda i: (step0 + i, 0, 0)),
                  tok_spec(ROUTER_COLS), tok_spec(D_MODEL),
                  pl.BlockSpec((1, D_MODEL), lambda i: (0, 0))],
        out_specs=pl.BlockSpec((tiles_per_step * TOK_TILE, D_MODEL), lambda i: (i, 0)),
        out_shape=jax.ShapeDtypeStruct((n_tiles * TOK_TILE, D_MODEL), F32),
        compiler_params=pltpu.CompilerParams(dimension_semantics=("arbitrary",),
                                             vmem_limit_bytes=VMEM_LIMIT),
        name="combine",
    )(y_blocks, route, x1, g_final)


def kernel(x_prompt, x_sample, state_pool, state_conv, g_mix, w_in, w_pool_grp, pool_scale, w_pool_out, w_dw, b_dw, ln_g, ln_b, w_conv_out, w_out, g_ffn, w_rg, b_rg, w_re, b_re, w_gate, w_up, w_down, g_final):
    assert g_mix.shape[0] == 1
    l = 0
    bp, seq_p, _ = x_prompt.shape
    bs, seq_s, _ = x_sample.shape
    tiles_p = bp * seq_p // TOK_TILE
    tiles_s = bs * seq_s // TOK_TILE
    n_tiles = tiles_p + tiles_s

    n_pad = ROUTER_COLS - N_EXPERTS - N_EXPERT_GROUPS
    w_r = jnp.concatenate([w_re[l], w_rg[l], jnp.zeros((D_MODEL, n_pad), F32)], axis=1)
    b_r = jnp.concatenate([b_re[l], b_rg[l], jnp.zeros((n_pad,), F32)]).reshape(1, ROUTER_COLS)
    w_r_hi = w_r.astype(BF16)
    w_r_cat = jnp.concatenate([w_r_hi, (w_r - w_r_hi.astype(F32)).astype(BF16)], axis=1)
    row = lambda a: a.reshape(1, -1)
    wts = (row(g_mix[l]), w_in[l].astype(BF16), w_pool_grp[l].astype(BF16), row(pool_scale[l]),
           w_pool_out[l].astype(BF16), w_dw[l], row(b_dw[l]), row(ln_g[l]), row(ln_b[l]),
           w_conv_out[l].astype(BF16), w_out[l].astype(BF16), row(g_ffn[l]), w_r_cat, b_r)

    zp = jnp.zeros((bp, POOL_HIST, D_POOL), F32)
    zc = jnp.zeros((bp, CONV_HIST, D_CONV), F32)
    *shared, npp, ncp = _mixer(x_prompt, zp, zc, 0, 1, TOK_TILE, wts, n_tiles + 1, 0, None)
    x1, route, xs, nblk, nps, ncs = _mixer(x_sample, state_pool[l], state_conv[l], PAST_LEN, bs, seq_s, wts,
                                           n_tiles + 1, tiles_p, shared)

    n_chunks_max = -(-n_tiles * TILE_BLOCKS // CHUNK_BLOCKS) + N_EXPERTS
    table, chunk_e, n_chunks = _chunk_tables(nblk[:n_tiles, 0, :N_EXPERTS], n_chunks_max)
    y_blocks = _experts(xs.reshape((n_tiles + 1) * TILE_BLOCKS, BLK, D_MODEL), n_tiles * TILE_BLOCKS,
                        table, chunk_e, n_chunks,
                        w_gate[l].reshape(N_EXPERTS, D_MODEL, D_EXPERT),
                        w_up[l].reshape(N_EXPERTS, D_MODEL, D_EXPERT),
                        w_down[l].reshape(N_EXPERTS, D_EXPERT, D_MODEL))
    tps = COMBINE_TILES if tiles_p % COMBINE_TILES == 0 else 1
    yp = _combine(y_blocks, route, x1, row(g_final), 0, tiles_p, tps)
    ys = _combine(y_blocks, route, x1, row(g_final), tiles_p, tiles_s, 1)
    return (yp.reshape(bp, seq_p, D_MODEL), ys.reshape(bs, seq_s, D_MODEL),
            npp[None], ncp[None], nps[None], ncs[None])
```

```python
import functools

import jax
import jax.numpy as jnp
from jax import lax
from jax.experimental import pallas as pl
from jax.experimental.pallas import tpu as pltpu

D_MODEL = 1024
D_POOL = 512
N_POOL_GROUPS = 4
POOL_GROUP = D_POOL // N_POOL_GROUPS
POOL_WINDOWS = (2, 4, 8, 16)
POOL_HIST = max(POOL_WINDOWS) - 1
D_CONV = 512
CONV_WIDTH = 31
CONV_HIST = CONV_WIDTH - 1
N_EXPERT_GROUPS = 4
EXPERTS_PER_GROUP = 8
N_EXPERTS = N_EXPERT_GROUPS * EXPERTS_PER_GROUP
TOP_K = 2
D_EXPERT = 256
RMS_EPS = 1e-6
LN_EPS = 1e-5
PAST_LEN = 1024

LANES = 128
SUBLANES = 8
POOL_PAD = 16
CONV_PAD = 32
CONV_CHUNK = 64
ROUTER_COLS = LANES
VMEM_LIMIT = 56 * 1024 * 1024

TOK_TILE = 512
BLK = 2 * SUBLANES
TILE_BLOCKS = (TOP_K * TOK_TILE + N_EXPERTS * (BLK - 1)) // BLK
TILE_ROWS = TILE_BLOCKS * BLK
CHUNK_BLOCKS = 32
CHUNK_ROWS = CHUNK_BLOCKS * BLK
RING_SLOTS = 3
ROUTE_POS1, ROUTE_POS2, ROUTE_W1, ROUTE_W2 = 0, 1, 2, 3
COMBINE_TILES = 2

BF16 = jnp.bfloat16
F32 = jnp.float32
I32 = jnp.int32


def _dot(a, b):
    return jnp.dot(a, b, preferred_element_type=F32)


def _rmsnorm(x, g):
    return x * lax.rsqrt(jnp.mean(x * x, axis=-1, keepdims=True) + RMS_EPS) * g


def _routing(logits):
    lane = lax.broadcasted_iota(I32, logits.shape, 1)
    lane_f = lane.astype(F32)
    neg = jnp.float32(-jnp.inf)
    big = jnp.float32(1e9)
    is_grp = (lane >= N_EXPERTS) & (lane < N_EXPERTS + N_EXPERT_GROUPS)
    glog = jnp.where(is_grp, logits, neg)
    gmax = jnp.max(glog, axis=1, keepdims=True)
    gidx = jnp.min(jnp.where(glog == gmax, lane_f, big), axis=1, keepdims=True) - float(N_EXPERTS)
    gsum = jnp.sum(jnp.where(is_grp, jnp.exp(glog - gmax), 0.0), axis=1, keepdims=True)
    p_sel = 1.0 / gsum
    lane_grp = (lane >> 3).astype(F32)
    is_sel = (lane < N_EXPERTS) & (lane_grp == gidx)
    elog = jnp.where(is_sel, logits, neg)
    v1 = jnp.max(elog, axis=1, keepdims=True)
    i1 = jnp.min(jnp.where(elog == v1, lane_f, big), axis=1, keepdims=True)
    elog2 = jnp.where(lane_f == i1, neg, elog)
    v2 = jnp.max(elog2, axis=1, keepdims=True)
    i2 = jnp.min(jnp.where(elog2 == v2, lane_f, big), axis=1, keepdims=True)
    t = jnp.exp(v2 - v1)
    return i1, i2, p_sel / (1.0 + t), p_sel * (t / (1.0 + t))


def _dispatch(h2, i1, i2, w1, w2):
    rows = h2.shape[0]
    lane = lax.broadcasted_iota(I32, (rows, LANES), 1)
    lane_f = lane.astype(F32)
    hit1 = lane_f == i1
    hit2 = lane_f == i2
    sel = jnp.where(hit1 | hit2, 1.0, 0.0)
    earlier = lax.broadcasted_iota(I32, (rows, rows), 1) < lax.broadcasted_iota(I32, (rows, rows), 0)
    rank = _dot(jnp.where(earlier, 1.0, 0.0).astype(BF16), sel.astype(BF16))
    cnt = jnp.sum(sel, axis=0, keepdims=True)
    nblk = jnp.floor((cnt + float(BLK - 1)) * (1.0 / BLK))
    before = lax.broadcasted_iota(I32, (LANES, LANES), 0) < lax.broadcasted_iota(I32, (LANES, LANES), 1)
    bstart = _dot(jnp.broadcast_to(nblk, (SUBLANES, LANES)).astype(BF16),
                  jnp.where(before, 1.0, 0.0).astype(BF16))[0:1, :]
    pos = float(BLK) * bstart + rank
    pos1 = jnp.sum(jnp.where(hit1, pos, 0.0), axis=1, keepdims=True)
    pos2 = jnp.sum(jnp.where(hit2, pos, 0.0), axis=1, keepdims=True)
    route = jnp.where(lane == ROUTE_POS1, pos1,
                      jnp.where(lane == ROUTE_POS2, pos2,
                                jnp.where(lane == ROUTE_W1, w1, jnp.where(lane == ROUTE_W2, w2, 0.0))))
    route_t = route.T
    p1 = route_t[ROUTE_POS1:ROUTE_POS1 + 1, :].astype(I32)
    p2 = route_t[ROUTE_POS2:ROUTE_POS2 + 1, :].astype(I32)
    r_idx = lax.broadcasted_iota(I32, (TILE_ROWS, rows), 0)
    onehot = jnp.where((r_idx == p1) | (r_idx == p2), 1.0, 0.0).astype(BF16)
    return route, _dot(onehot, h2).astype(BF16), nblk


N_MIXER_IN = 17
N_MIXER_OUT = 6
N_MIXER_SCRATCH = 5


def _mixer_kernel(past, sb, ls, nj, n_valid, n_steps, *refs):
    t = pl.program_id(0)
    if n_valid == n_steps:
        _mixer_tile(past, sb, ls, t % nj, nj, *refs)
        return
    outs = refs[-(N_MIXER_OUT + N_MIXER_SCRATCH):][:N_MIXER_OUT]

    @pl.when(t < n_valid)
    def _():
        _mixer_tile(past, sb, ls, t % nj, nj, *refs)

    @pl.when(t >= n_valid)
    def _():
        for ref in outs[:4]:
            ref[...] = jnp.zeros(ref.shape, ref.dtype)


def _mixer_tile(past, sb, ls, j, nj, *refs):
    (x_ref, hp_ref, hc_ref, gmix_ref, win_ref, wgrp_ref, pscale_ref, wpo_ref, wdw_ref, bdw_ref,
     lng_ref, lnb_ref, wco_ref, wout_ref, gffn_ref, wrc_ref, br_ref) = refs[:N_MIXER_IN]
    x1_ref, route_ref, xs_ref, nblk_ref, npool_ref, nconv_ref = refs[-(N_MIXER_OUT + N_MIXER_SCRATCH):][:N_MIXER_OUT]
    ubuf, vbuf, sbuf, cbuf, ybuf = refs[-N_MIXER_SCRATCH:]
    rows = sb * ls

    @pl.when(j == 0)
    def _():
        ubuf[:, POOL_PAD - POOL_HIST:POOL_PAD, :] = hp_ref[...]
        vbuf[:, CONV_PAD - CONV_HIST:CONV_PAD, :] = hc_ref[...]

    x = x_ref[...].reshape(rows, D_MODEL)
    h = _rmsnorm(x, gmix_ref[...]).astype(BF16)
    u = _dot(h, win_ref[:, 0:D_POOL])
    ubuf[:, POOL_PAD:POOL_PAD + ls, :] = u.reshape(sb, ls, D_POOL)
    ga = _dot(h, win_ref[:, D_POOL:D_POOL + D_CONV])
    gb = _dot(h, win_ref[:, D_POOL + D_CONV:D_POOL + 2 * D_CONV])
    vbuf[:, CONV_PAD:CONV_PAD + ls, :] = (ga * jax.nn.sigmoid(gb)).reshape(sb, ls, D_CONV)
    c_gp = D_POOL + 2 * D_CONV
    gate_pool = _dot(h, win_ref[:, c_gp:c_gp + D_MODEL])
    gate_conv = _dot(h, win_ref[:, c_gp + D_MODEL:c_gp + 2 * D_MODEL])

    t_idx = lax.broadcasted_iota(I32, (sb, ls, 1), 1)
    frames = (past + 1 + j * ls + t_idx).astype(F32)
    ys = []
    part, width = ubuf[...], 1
    for gi, w in enumerate(POOL_WINDOWS):
        c0, c1 = gi * POOL_GROUP, (gi + 1) * POOL_GROUP
        cur = ubuf[:, POOL_PAD:POOL_PAD + ls, c0:c1]
        while width < w:
            part = part + pltpu.roll(part, width, 1)
            width *= 2
        s = part[:, POOL_PAD:POOL_PAD + ls, 0:POOL_GROUP]
        part = part[:, :, POOL_GROUP:]
        mean = s / jnp.minimum(jnp.float32(w), frames)
        ys.append(_dot((mean - cur).reshape(rows, POOL_GROUP).astype(BF16), wgrp_ref[gi]))
    yp = (jnp.concatenate(ys, axis=1) * pscale_ref[...]).astype(BF16)
    y_pool = _dot(yp, wpo_ref[...])
    new_pool = ubuf[:, POOL_PAD + ls - POOL_HIST:POOL_PAD + ls, :]
    ubuf[:, POOL_PAD - POOL_HIST:POOL_PAD, :] = new_pool

    first = CONV_PAD - CONV_HIST
    taps = [[o for o in range(first, first + CONV_WIDTH) if o % SUBLANES == r] for r in range(SUBLANES)]
    for r in range(1, SUBLANES):
        span = taps[r][-1] - taps[r][0] + ls
        sbuf[r - 1, :, 0:span, :] = vbuf[:, taps[r][0]:taps[r][0] + span, :]
    nb_c, nt_c = (1, CONV_CHUNK) if ls >= CONV_CHUNK else (CONV_CHUNK // ls, ls)
    n_grp = nt_c // SUBLANES
    for i in range(rows // CONV_CHUNK):
        b0, t0 = (0, i * nt_c) if nb_c == 1 else (i * nb_c, 0)
        for c0 in range(0, D_CONV, LANES):
            cs = slice(c0, c0 + LANES)
            accs = [jnp.zeros((nb_c, SUBLANES, LANES), F32) + bdw_ref[:, cs] for _ in range(n_grp)]
            for r in range(SUBLANES):
                ws = [wdw_ref[o - first:o - first + 1, cs] for o in taps[r]]
                for q in range(n_grp + len(taps[r]) - 1):
                    lo = t0 + q * SUBLANES
                    if r == 0:
                        val = vbuf[b0:b0 + nb_c, taps[0][0] + lo:taps[0][0] + lo + SUBLANES, cs]
                    else:
                        val = sbuf[r - 1, b0:b0 + nb_c, lo:lo + SUBLANES, cs]
                    for a in range(len(taps[r])):
                        if 0 <= q - a < n_grp:
                            accs[q - a] = accs[q - a] + val * ws[a]
            cbuf[:, cs] = jnp.concatenate(accs, axis=1).reshape(CONV_CHUNK, LANES)
        yf = cbuf[...]
        mu = jnp.mean(yf, axis=-1, keepdims=True)
        yc = yf - mu
        var = jnp.mean(yc * yc, axis=-1, keepdims=True)
        yn = yc * lax.rsqrt(var + LN_EPS) * lng_ref[...] + lnb_ref[...]
        ybuf[i * CONV_CHUNK:(i + 1) * CONV_CHUNK, :] = (yn * jax.nn.sigmoid(yn)).astype(BF16)
    y_conv = _dot(ybuf[...], wco_ref[...])
    new_conv = vbuf[:, CONV_PAD + ls - CONV_HIST:CONV_PAD + ls, :]
    vbuf[:, CONV_PAD - CONV_HIST:CONV_PAD, :] = new_conv

    @pl.when(j == nj - 1)
    def _():
        npool_ref[...] = new_pool
        nconv_ref[...] = new_conv

    merged = jax.nn.sigmoid(gate_pool) * y_pool + jax.nn.sigmoid(gate_conv) * y_conv
    x1 = x + _dot(merged.astype(BF16), wout_ref[...])
    x1_ref[...] = x1

    h2 = _rmsnorm(x1, gffn_ref[...])
    h2_hi = h2.astype(BF16)
    h2_lo = (h2 - h2_hi.astype(F32)).astype(BF16)
    hi = _dot(h2_hi, wrc_ref[...])
    logits = (hi[:, 0:ROUTER_COLS] + (hi[:, ROUTER_COLS:] + _dot(h2_lo, wrc_ref[:, 0:ROUTER_COLS]))) + br_ref[...]
    route, xs, nblk = _dispatch(h2_hi, *_routing(logits))
    route_ref[...] = route
    xs_ref[...] = xs
    nblk_ref[...] = jnp.broadcast_to(nblk, (SUBLANES, LANES)).astype(I32)


def _const_spec(shape):
    nd = len(shape)
    return pl.BlockSpec(shape, lambda t: (0,) * nd, pipeline_mode=pl.Buffered(1))


def _mixer(x, hist_pool, hist_conv, past, sb, ls, wts, n_tiles, tile0, shared):
    nb, seq, _ = x.shape
    assert nb % sb == 0 and seq % ls == 0 and sb * ls == TOK_TILE and ls % SUBLANES == 0 and ls >= CONV_HIST
    nj = seq // ls
    n_valid = (nb // sb) * nj
    n_steps = n_valid if shared is not None else n_tiles - tile0
    own = lambda t: jnp.minimum(t, n_valid - 1)
    hist_spec = lambda n, width: pl.BlockSpec((sb, n, width), lambda t: (own(t) // nj, 0, 0))
    in_specs = [pl.BlockSpec((sb, ls, D_MODEL), lambda t: (own(t) // nj, own(t) % nj, 0)),
                hist_spec(POOL_HIST, D_POOL), hist_spec(CONV_HIST, D_CONV)]
    in_specs += [_const_spec(w.shape) for w in wts]
    shared_shapes = (
        jax.ShapeDtypeStruct((n_tiles * TOK_TILE, D_MODEL), F32),
        jax.ShapeDtypeStruct((n_tiles * TOK_TILE, ROUTER_COLS), F32),
        jax.ShapeDtypeStruct((n_tiles, TILE_ROWS, D_MODEL), BF16),
        jax.ShapeDtypeStruct((n_tiles, SUBLANES, LANES), I32),
    )
    out_shape = shared_shapes + (jax.ShapeDtypeStruct((nb, POOL_HIST, D_POOL), F32),
                                 jax.ShapeDtypeStruct((nb, CONV_HIST, D_CONV), F32))
    out_specs = (pl.BlockSpec((TOK_TILE, D_MODEL), lambda t: (tile0 + t, 0)),
                 pl.BlockSpec((TOK_TILE, ROUTER_COLS), lambda t: (tile0 + t, 0)),
                 pl.BlockSpec((None, TILE_ROWS, D_MODEL), lambda t: (tile0 + t, 0, 0)),
                 pl.BlockSpec((None, SUBLANES, LANES), lambda t: (tile0 + t, 0, 0)),
                 hist_spec(POOL_HIST, D_POOL), hist_spec(CONV_HIST, D_CONV))
    args = [x, hist_pool, hist_conv, *wts]
    aliases = {}
    if shared is not None:
        in_specs += [pl.BlockSpec(memory_space=pl.ANY)] * len(shared)
        aliases = {len(args) + k: k for k in range(len(shared))}
        args += list(shared)
    rows = sb * ls
    return pl.pallas_call(
        functools.partial(_mixer_kernel, past, sb, ls, nj, n_valid, n_steps),
        grid=(n_steps,),
        in_specs=in_specs,
        out_specs=out_specs,
        out_shape=out_shape,
        input_output_aliases=aliases,
        scratch_shapes=[pltpu.VMEM((sb, POOL_PAD + ls, D_POOL), F32),
                        pltpu.VMEM((sb, CONV_PAD + ls, D_CONV), F32),
                        pltpu.VMEM((SUBLANES - 1, sb, CONV_PAD + ls, D_CONV), F32),
                        pltpu.VMEM((CONV_CHUNK, D_CONV), F32),
                        pltpu.VMEM((rows, D_CONV), BF16)],
        compiler_params=pltpu.CompilerParams(dimension_semantics=("arbitrary",),
                                             vmem_limit_bytes=VMEM_LIMIT),
        name="mixer",
    )(*args)


def _chunk_tables(nblk, n_chunks_max):
    n_tiles = nblk.shape[0]
    e_ids = jnp.arange(N_EXPERTS, dtype=I32)
    per_e = jnp.sum(nblk, axis=0)
    chunks_e = -(-per_e // CHUNK_BLOCKS)
    cend = jnp.cumsum(chunks_e)
    cstart = cend - chunks_e
    n_chunks = cend[-1]
    c = jnp.arange(n_chunks_max, dtype=I32)
    chunk_e = jnp.minimum(jnp.sum((cend[None, :] <= c[:, None]).astype(I32), axis=1), N_EXPERTS - 1)
    is_e = chunk_e[:, None] == e_ids[None, :]
    of_chunk = lambda v: jnp.sum(jnp.where(is_e, v[None, :], 0), axis=1)
    of_chunk_t = lambda a: jnp.sum(jnp.where(is_e[:, None, :], a[None, :, :], 0), axis=2)
    cum_t = jnp.cumsum(nblk, axis=0)
    bstart = jnp.cumsum(nblk, axis=1) - nblk
    k = ((c - of_chunk(cstart)) * CHUNK_BLOCKS)[:, None] + jnp.arange(CHUNK_BLOCKS, dtype=I32)[None, :]
    tile = jnp.sum((of_chunk_t(cum_t)[:, None, :] <= k[:, :, None]).astype(I32), axis=2)
    is_t = tile[:, :, None] == jnp.arange(n_tiles, dtype=I32)[None, None, :]
    shift = of_chunk_t(bstart - (cum_t - nblk))
    blk = k + jnp.sum(jnp.where(is_t, shift[:, None, :], 0), axis=2)
    valid = (k < of_chunk(per_e)[:, None]) & (c[:, None] < n_chunks)
    table = jnp.where(valid, tile * TILE_BLOCKS + blk, -1)
    return table.reshape(-1).astype(I32), chunk_e.astype(I32), n_chunks.reshape(1).astype(I32)


def _expert_kernel(spare0, tab_ref, ce_ref, nch_ref, x_hbm, wg_ref, wu_ref, wd_ref, y_hbm,
                   xin, yout, wgu_bf, wd_bf, sem_in, sem_out):
    c = pl.program_id(0)
    nc = pl.num_programs(0)
    nch = nch_ref[0]

    zero_block = spare0 + RING_SLOTS * CHUNK_BLOCKS

    def src_block(chunk, s):
        t = tab_ref[jnp.minimum(chunk, nc - 1) * CHUNK_BLOCKS + s]
        return jnp.where((t < 0) | (chunk >= nch), zero_block, t)

    def dst_block(chunk, s):
        t = tab_ref[jnp.maximum(chunk, 0) * CHUNK_BLOCKS + s]
        return jnp.where((t < 0) | (chunk < 0), spare0 + (chunk % RING_SLOTS) * CHUNK_BLOCKS + s, t)

    def in_copy(chunk, s):
        slot = chunk % RING_SLOTS
        return pltpu.make_async_copy(x_hbm.at[src_block(chunk, s)], xin.at[slot, pl.ds(s * BLK, BLK), :],
                                     sem_in.at[slot])

    def out_copy(chunk, s):
        slot = chunk % RING_SLOTS
        return pltpu.make_async_copy(yout.at[slot, pl.ds(s * BLK, BLK), :], y_hbm.at[dst_block(chunk, s)],
                                     sem_out.at[slot])

    @pl.when(c < nch)
    def _():
        @pl.when(c == 0)
        def _():
            yout[...] = jnp.zeros(yout.shape, yout.dtype)
            for before in range(RING_SLOTS, 1, -1):
                for s in range(CHUNK_BLOCKS):
                    out_copy(c - before, s).start()
            for first in range(RING_SLOTS - 1):
                for s in range(CHUNK_BLOCKS):
                    in_copy(c + first, s).start()

        @pl.when((c == 0) | (ce_ref[c] != ce_ref[jnp.maximum(c - 1, 0)]))
        def _():
            wgu_bf[:, 0:D_EXPERT] = wg_ref[...].astype(BF16)
            wgu_bf[:, D_EXPERT:2 * D_EXPERT] = wu_ref[...].astype(BF16)
            wd_bf[...] = wd_ref[...].astype(BF16)

        for s in range(CHUNK_BLOCKS):
            in_copy(c + RING_SLOTS - 1, s).start()
        for s in range(CHUNK_BLOCKS):
            in_copy(c, s).wait()

        ab = _dot(xin[c % RING_SLOTS], wgu_bf[...])

        for s in range(CHUNK_BLOCKS):
            out_copy(c - 1, s).start()

        a = ab[:, 0:D_EXPERT]
        b = ab[:, D_EXPERT:2 * D_EXPERT]
        act = ((a * jax.nn.sigmoid(a)) * b).astype(BF16)

        for s in range(CHUNK_BLOCKS):
            out_copy(c - RING_SLOTS, s).wait()
        yout[c % RING_SLOTS] = _dot(act, wd_bf[...]).astype(BF16)

        @pl.when(c == nch - 1)
        def _():
            for s in range(CHUNK_BLOCKS):
                out_copy(c, s).start()
            for ahead in range(1, RING_SLOTS):
                for s in range(CHUNK_BLOCKS):
                    in_copy(c + ahead, s).wait()
            for before in range(RING_SLOTS - 1, -1, -1):
                for s in range(CHUNK_BLOCKS):
                    out_copy(c - before, s).wait()


def _experts(xs_blocks, n_used_blocks, table, chunk_e, n_chunks, w_gate, w_up, w_down):
    assert xs_blocks.shape[0] - n_used_blocks > RING_SLOTS * CHUNK_BLOCKS
    n_steps = chunk_e.shape[0]
    w_spec = lambda shape: pl.BlockSpec((None,) + shape, lambda c, tab, ce, nch: (ce[c], 0, 0))
    return pl.pallas_call(
        functools.partial(_expert_kernel, n_used_blocks),
        grid_spec=pltpu.PrefetchScalarGridSpec(
            num_scalar_prefetch=3,
            grid=(n_steps,),
            in_specs=[pl.BlockSpec(memory_space=pl.ANY),
                      w_spec((D_MODEL, D_EXPERT)), w_spec((D_MODEL, D_EXPERT)), w_spec((D_EXPERT, D_MODEL))],
            out_specs=pl.BlockSpec(memory_space=pl.ANY),
            scratch_shapes=[pltpu.VMEM((RING_SLOTS, CHUNK_ROWS, D_MODEL), BF16),
                            pltpu.VMEM((RING_SLOTS, CHUNK_ROWS, D_MODEL), BF16),
                            pltpu.VMEM((D_MODEL, 2 * D_EXPERT), BF16),
                            pltpu.VMEM((D_EXPERT, D_MODEL), BF16),
                            pltpu.SemaphoreType.DMA((RING_SLOTS,)),
                            pltpu.SemaphoreType.DMA((RING_SLOTS,))]),
        out_shape=jax.ShapeDtypeStruct(xs_blocks.shape, xs_blocks.dtype),
        input_output_aliases={3: 0},
        compiler_params=pltpu.CompilerParams(dimension_semantics=("arbitrary",),
                                             vmem_limit_bytes=VMEM_LIMIT),
        name="experts",
    )(table, chunk_e, n_chunks, xs_blocks, w_gate, w_up, w_down)


def _combine_kernel(tiles, y_ref, route_ref, x1_ref, gfin_ref, o_ref):
    for k in range(tiles):
        rs = slice(k * TOK_TILE, (k + 1) * TOK_TILE)
        route = route_ref[rs, :]
        pos1 = route[:, ROUTE_POS1:ROUTE_POS1 + 1].astype(I32)
        pos2 = route[:, ROUTE_POS2:ROUTE_POS2 + 1].astype(I32)
        w1 = route[:, ROUTE_W1:ROUTE_W1 + 1]
        w2 = route[:, ROUTE_W2:ROUTE_W2 + 1]
        r_idx = lax.broadcasted_iota(I32, (TOK_TILE, TILE_ROWS), 1)
        gather = jnp.where(r_idx == pos1, w1, jnp.where(r_idx == pos2, w2, 0.0)).astype(BF16)
        y = y_ref[k * TILE_BLOCKS:(k + 1) * TILE_BLOCKS].reshape(TILE_ROWS, D_MODEL)
        o_ref[rs, :] = _rmsnorm(x1_ref[rs, :] + _dot(gather, y), gfin_ref[...])


def _combine(y_blocks, route, x1, g_final, tile0, n_tiles, tiles_per_step):
    assert n_tiles % tiles_per_step == 0 and tile0 % tiles_per_step == 0
    step0 = tile0 // tiles_per_step
    tok_spec = lambda width: pl.BlockSpec((tiles_per_step * TOK_TILE, width), lambda i: (step0 + i, 0))
    return pl.pallas_call(
        functools.partial(_combine_kernel, tiles_per_step),
        grid=(n_tiles // tiles_per_step,),
        in_specs=[pl.BlockSpec((tiles_per_step * TILE_BLOCKS, BLK, D_MODEL), lambda i: (step0 + i, 0, 0)),
                  tok_spec(ROUTER_COLS), tok_spec(D_MODEL),
                  pl.BlockSpec((1, D_MODEL), lambda i: (0, 0))],
        out_specs=pl.BlockSpec((tiles_per_step * TOK_TILE, D_MODEL), lambda i: (i, 0)),
        out_shape=jax.ShapeDtypeStruct((n_tiles * TOK_TILE, D_MODEL), F32),
        compiler_params=pltpu.CompilerParams(dimension_semantics=("arbitrary",),
                                             vmem_limit_bytes=VMEM_LIMIT),
        name="combine",
    )(y_blocks, route, x1, g_final)


def kernel(x_prompt, x_sample, state_pool, state_conv, g_mix, w_in, w_pool_grp, pool_scale, w_pool_out, w_dw, b_dw, ln_g, ln_b, w_conv_out, w_out, g_ffn, w_rg, b_rg, w_re, b_re, w_gate, w_up, w_down, g_final):
    assert g_mix.shape[0] == 1
    l = 0
    bp, seq_p, _ = x_prompt.shape
    bs, seq_s, _ = x_sample.shape
    tiles_p = bp * seq_p // TOK_TILE
    tiles_s = bs * seq_s // TOK_TILE
    n_tiles = tiles_p + tiles_s

    n_pad = ROUTER_COLS - N_EXPERTS - N_EXPERT_GROUPS
    w_r = jnp.concatenate([w_re[l], w_rg[l], jnp.zeros((D_MODEL, n_pad), F32)], axis=1)
    b_r = jnp.concatenate([b_re[l], b_rg[l], jnp.zeros((n_pad,), F32)]).reshape(1, ROUTER_COLS)
    w_r_hi = w_r.astype(BF16)
    w_r_cat = jnp.concatenate([w_r_hi, (w_r - w_r_hi.astype(F32)).astype(BF16)], axis=1)
    row = lambda a: a.reshape(1, -1)
    wts = (row(g_mix[l]), w_in[l].astype(BF16), w_pool_grp[l].astype(BF16), row(pool_scale[l]),
           w_pool_out[l].astype(BF16), w_dw[l], row(b_dw[l]), row(ln_g[l]), row(ln_b[l]),
           w_conv_out[l].astype(BF16), w_out[l].astype(BF16), row(g_ffn[l]), w_r_cat, b_r)

    zp = jnp.zeros((bp, POOL_HIST, D_POOL), F32)
    zc = jnp.zeros((bp, CONV_HIST, D_CONV), F32)
    n_all = n_tiles + -(-(RING_SLOTS * CHUNK_BLOCKS + 1) // TILE_BLOCKS)
    *shared, npp, ncp = _mixer(x_prompt, zp, zc, 0, 1, TOK_TILE, wts, n_all, 0, None)
    x1, route, xs, nblk, nps, ncs = _mixer(x_sample, state_pool[l], state_conv[l], PAST_LEN, bs, seq_s, wts,
                                           n_all, tiles_p, shared)

    n_chunks_max = -(-n_tiles * TILE_BLOCKS // CHUNK_BLOCKS) + N_EXPERTS
    table, chunk_e, n_chunks = _chunk_tables(nblk[:n_tiles, 0, :N_EXPERTS], n_chunks_max)
    y_blocks = _experts(xs.reshape(n_all * TILE_BLOCKS, BLK, D_MODEL), n_tiles * TILE_BLOCKS,
                        table, chunk_e, n_chunks,
                        w_gate[l].reshape(N_EXPERTS, D_MODEL, D_EXPERT),
                        w_up[l].reshape(N_EXPERTS, D_MODEL, D_EXPERT),
                        w_down[l].reshape(N_EXPERTS, D_EXPERT, D_MODEL))
    tps = COMBINE_TILES if tiles_p % COMBINE_TILES == 0 else 1
    yp = _combine(y_blocks, route, x1, row(g_final), 0, tiles_p, tps)
    ys = _combine(y_blocks, route, x1, row(g_final), tiles_p, tiles_s, 1)
    return (yp.reshape(bp, seq_p, D_MODEL), ys.reshape(bs, seq_s, D_MODEL),
            npp[None], ncp[None], nps[None], ncs[None])
```

```python
import functools

import jax
import jax.numpy as jnp
from jax import lax
from jax.experimental import pallas as pl
from jax.experimental.pallas import tpu as pltpu

D_MODEL = 1024
D_POOL = 512
N_POOL_GROUPS = 4
POOL_GROUP = D_POOL // N_POOL_GROUPS
POOL_WINDOWS = (2, 4, 8, 16)
POOL_HIST = max(POOL_WINDOWS) - 1
D_CONV = 512
CONV_WIDTH = 31
CONV_HIST = CONV_WIDTH - 1
N_EXPERT_GROUPS = 4
EXPERTS_PER_GROUP = 8
N_EXPERTS = N_EXPERT_GROUPS * EXPERTS_PER_GROUP
TOP_K = 2
D_EXPERT = 256
RMS_EPS = 1e-6
LN_EPS = 1e-5
PAST_LEN = 1024

LANES = 128
SUBLANES = 8
POOL_PAD = 16
CONV_PAD = 32
CONV_CHUNK = 64
ROUTER_COLS = LANES
VMEM_LIMIT = 56 * 1024 * 1024

TOK_TILE = 512
BLK = 2 * SUBLANES
TILE_BLOCKS = (TOP_K * TOK_TILE + N_EXPERTS * (BLK - 1)) // BLK
TILE_ROWS = TILE_BLOCKS * BLK
CHUNK_BLOCKS = 32
CHUNK_ROWS = CHUNK_BLOCKS * BLK
RING_SLOTS = 3
ROUTE_POS1, ROUTE_POS2, ROUTE_W1, ROUTE_W2 = 0, 1, 2, 3
COMBINE_TILES = 2

BF16 = jnp.bfloat16
F32 = jnp.float32
I32 = jnp.int32


def _dot(a, b):
    return jnp.dot(a, b, preferred_element_type=F32)


def _rmsnorm(x, g):
    return x * lax.rsqrt(jnp.mean(x * x, axis=-1, keepdims=True) + RMS_EPS) * g


def _routing(logits):
    lane = lax.broadcasted_iota(I32, logits.shape, 1)
    lane_f = lane.astype(F32)
    neg = jnp.float32(-jnp.inf)
    big = jnp.float32(1e9)
    is_grp = (lane >= N_EXPERTS) & (lane < N_EXPERTS + N_EXPERT_GROUPS)
    glog = jnp.where(is_grp, logits, neg)
    gmax = jnp.max(glog, axis=1, keepdims=True)
    gidx = jnp.min(jnp.where(glog == gmax, lane_f, big), axis=1, keepdims=True) - float(N_EXPERTS)
    gsum = jnp.sum(jnp.where(is_grp, jnp.exp(glog - gmax), 0.0), axis=1, keepdims=True)
    p_sel = 1.0 / gsum
    lane_grp = (lane >> 3).astype(F32)
    is_sel = (lane < N_EXPERTS) & (lane_grp == gidx)
    elog = jnp.where(is_sel, logits, neg)
    v1 = jnp.max(elog, axis=1, keepdims=True)
    i1 = jnp.min(jnp.where(elog == v1, lane_f, big), axis=1, keepdims=True)
    elog2 = jnp.where(lane_f == i1, neg, elog)
    v2 = jnp.max(elog2, axis=1, keepdims=True)
    i2 = jnp.min(jnp.where(elog2 == v2, lane_f, big), axis=1, keepdims=True)
    t = jnp.exp(v2 - v1)
    return i1, i2, p_sel / (1.0 + t), p_sel * (t / (1.0 + t))


def _dispatch(h2, i1, i2, w1, w2):
    rows = h2.shape[0]
    lane = lax.broadcasted_iota(I32, (rows, LANES), 1)
    lane_f = lane.astype(F32)
    hit1 = lane_f == i1
    hit2 = lane_f == i2
    sel = jnp.where(hit1 | hit2, 1.0, 0.0)
    earlier = lax.broadcasted_iota(I32, (rows, rows), 1) < lax.broadcasted_iota(I32, (rows, rows), 0)
    rank = _dot(jnp.where(earlier, 1.0, 0.0).astype(BF16), sel.astype(BF16))
    cnt = jnp.sum(sel, axis=0, keepdims=True)
    nblk = jnp.floor((cnt + float(BLK - 1)) * (1.0 / BLK))
    before = lax.broadcasted_iota(I32, (LANES, LANES), 0) < lax.broadcasted_iota(I32, (LANES, LANES), 1)
    bstart = _dot(jnp.broadcast_to(nblk, (SUBLANES, LANES)).astype(BF16),
                  jnp.where(before, 1.0, 0.0).astype(BF16))[0:1, :]
    pos = float(BLK) * bstart + rank
    pos1 = jnp.sum(jnp.where(hit1, pos, 0.0), axis=1, keepdims=True)
    pos2 = jnp.sum(jnp.where(hit2, pos, 0.0), axis=1, keepdims=True)
    route = jnp.where(lane == ROUTE_POS1, pos1,
                      jnp.where(lane == ROUTE_POS2, pos2,
                                jnp.where(lane == ROUTE_W1, w1, jnp.where(lane == ROUTE_W2, w2, 0.0))))
    route_t = route.T
    p1 = route_t[ROUTE_POS1:ROUTE_POS1 + 1, :].astype(I32)
    p2 = route_t[ROUTE_POS2:ROUTE_POS2 + 1, :].astype(I32)
    r_idx = lax.broadcasted_iota(I32, (TILE_ROWS, rows), 0)
    onehot = jnp.where((r_idx == p1) | (r_idx == p2), 1.0, 0.0).astype(BF16)
    return route, _dot(onehot, h2).astype(BF16), nblk


N_MIXER_IN = 17
N_MIXER_OUT = 6
N_MIXER_SCRATCH = 5


def _mixer_kernel(past, sb, ls, nj, n_valid, n_steps, cast_experts, *refs):
    t = pl.program_id(0)
    if n_valid == n_steps:
        assert not cast_experts
        _mixer_tile(past, sb, ls, t % nj, nj, *refs)
        return
    outs = refs[-(N_MIXER_OUT + N_MIXER_SCRATCH):][:N_MIXER_OUT]

    @pl.when(t < n_valid)
    def _():
        _mixer_tile(past, sb, ls, t % nj, nj, *refs)
        if cast_experts:
            wg_ref, wu_ref, wd_ref = refs[N_MIXER_IN:N_MIXER_IN + 3]
            wgu_out, wd_out = refs[-(N_MIXER_OUT + N_MIXER_SCRATCH + 2):-(N_MIXER_OUT + N_MIXER_SCRATCH)]
            wgu_out[:, 0:D_EXPERT] = wg_ref[...].astype(BF16)
            wgu_out[:, D_EXPERT:2 * D_EXPERT] = wu_ref[...].astype(BF16)
            wd_out[...] = wd_ref[...].astype(BF16)

    @pl.when(t >= n_valid)
    def _():
        for ref in outs[:4]:
            ref[...] = jnp.zeros(ref.shape, ref.dtype)


def _mixer_tile(past, sb, ls, j, nj, *refs):
    (x_ref, hp_ref, hc_ref, gmix_ref, win_ref, wgrp_ref, pscale_ref, wpo_ref, wdw_ref, bdw_ref,
     lng_ref, lnb_ref, wco_ref, wout_ref, gffn_ref, wrc_ref, br_ref) = refs[:N_MIXER_IN]
    x1_ref, route_ref, xs_ref, nblk_ref, npool_ref, nconv_ref = refs[-(N_MIXER_OUT + N_MIXER_SCRATCH):][:N_MIXER_OUT]
    ubuf, vbuf, sbuf, cbuf, ybuf = refs[-N_MIXER_SCRATCH:]
    rows = sb * ls

    @pl.when(j == 0)
    def _():
        ubuf[:, POOL_PAD - POOL_HIST:POOL_PAD, :] = hp_ref[...]
        vbuf[:, CONV_PAD - CONV_HIST:CONV_PAD, :] = hc_ref[...]

    x = x_ref[...].reshape(rows, D_MODEL)
    h = _rmsnorm(x, gmix_ref[...]).astype(BF16)
    u = _dot(h, win_ref[:, 0:D_POOL])
    ubuf[:, POOL_PAD:POOL_PAD + ls, :] = u.reshape(sb, ls, D_POOL)
    ga = _dot(h, win_ref[:, D_POOL:D_POOL + D_CONV])
    gb = _dot(h, win_ref[:, D_POOL + D_CONV:D_POOL + 2 * D_CONV])
    vbuf[:, CONV_PAD:CONV_PAD + ls, :] = (ga * jax.nn.sigmoid(gb)).reshape(sb, ls, D_CONV)
    c_gp = D_POOL + 2 * D_CONV
    gate_pool = _dot(h, win_ref[:, c_gp:c_gp + D_MODEL])
    gate_conv = _dot(h, win_ref[:, c_gp + D_MODEL:c_gp + 2 * D_MODEL])

    t_idx = lax.broadcasted_iota(I32, (sb, ls, 1), 1)
    frames = (past + 1 + j * ls + t_idx).astype(F32)
    ys = []
    part, width = ubuf[...], 1
    for gi, w in enumerate(POOL_WINDOWS):
        c0, c1 = gi * POOL_GROUP, (gi + 1) * POOL_GROUP
        cur = ubuf[:, POOL_PAD:POOL_PAD + ls, c0:c1]
        while width < w:
            part = part + pltpu.roll(part, width, 1)
            width *= 2
        s = part[:, POOL_PAD:POOL_PAD + ls, 0:POOL_GROUP]
        part = part[:, :, POOL_GROUP:]
        mean = s / jnp.minimum(jnp.float32(w), frames)
        ys.append(_dot((mean - cur).reshape(rows, POOL_GROUP).astype(BF16), wgrp_ref[gi]))
    yp = (jnp.concatenate(ys, axis=1) * pscale_ref[...]).astype(BF16)
    y_pool = _dot(yp, wpo_ref[...])
    new_pool = ubuf[:, POOL_PAD + ls - POOL_HIST:POOL_PAD + ls, :]
    ubuf[:, POOL_PAD - POOL_HIST:POOL_PAD, :] = new_pool

    first = CONV_PAD - CONV_HIST
    taps = [[o for o in range(first, first + CONV_WIDTH) if o % SUBLANES == r] for r in range(SUBLANES)]
    for r in range(1, SUBLANES):
        span = taps[r][-1] - taps[r][0] + ls
        sbuf[r - 1, :, 0:span, :] = vbuf[:, taps[r][0]:taps[r][0] + span, :]
    nb_c, nt_c = (1, CONV_CHUNK) if ls >= CONV_CHUNK else (CONV_CHUNK // ls, ls)
    n_grp = nt_c // SUBLANES
    for i in range(rows // CONV_CHUNK):
        b0, t0 = (0, i * nt_c) if nb_c == 1 else (i * nb_c, 0)
        for c0 in range(0, D_CONV, LANES):
            cs = slice(c0, c0 + LANES)
            accs = [jnp.zeros((nb_c, SUBLANES, LANES), F32) + bdw_ref[:, cs] for _ in range(n_grp)]
            for r in range(SUBLANES):
                ws = [wdw_ref[o - first:o - first + 1, cs] for o in taps[r]]
                for q in range(n_grp + len(taps[r]) - 1):
                    lo = t0 + q * SUBLANES
                    if r == 0:
                        val = vbuf[b0:b0 + nb_c, taps[0][0] + lo:taps[0][0] + lo + SUBLANES, cs]
                    else:
                        val = sbuf[r - 1, b0:b0 + nb_c, lo:lo + SUBLANES, cs]
                    for a in range(len(taps[r])):
                        if 0 <= q - a < n_grp:
                            accs[q - a] = accs[q - a] + val * ws[a]
            cbuf[:, cs] = jnp.concatenate(accs, axis=1).reshape(CONV_CHUNK, LANES)
        yf = cbuf[...]
        mu = jnp.mean(yf, axis=-1, keepdims=True)
        yc = yf - mu
        var = jnp.mean(yc * yc, axis=-1, keepdims=True)
        yn = yc * lax.rsqrt(var + LN_EPS) * lng_ref[...] + lnb_ref[...]
        ybuf[i * CONV_CHUNK:(i + 1) * CONV_CHUNK, :] = (yn * jax.nn.sigmoid(yn)).astype(BF16)
    y_conv = _dot(ybuf[...], wco_ref[...])
    new_conv = vbuf[:, CONV_PAD + ls - CONV_HIST:CONV_PAD + ls, :]
    vbuf[:, CONV_PAD - CONV_HIST:CONV_PAD, :] = new_conv

    @pl.when(j == nj - 1)
    def _():
        npool_ref[...] = new_pool
        nconv_ref[...] = new_conv

    merged = jax.nn.sigmoid(gate_pool) * y_pool + jax.nn.sigmoid(gate_conv) * y_conv
    x1 = x + _dot(merged.astype(BF16), wout_ref[...])
    x1_ref[...] = x1

    h2 = _rmsnorm(x1, gffn_ref[...])
    h2_hi = h2.astype(BF16)
    h2_lo = (h2 - h2_hi.astype(F32)).astype(BF16)
    hi = _dot(h2_hi, wrc_ref[...])
    logits = (hi[:, 0:ROUTER_COLS] + (hi[:, ROUTER_COLS:] + _dot(h2_lo, wrc_ref[:, 0:ROUTER_COLS]))) + br_ref[...]
    route, xs, nblk = _dispatch(h2_hi, *_routing(logits))
    route_ref[...] = route
    xs_ref[...] = xs
    nblk_ref[...] = jnp.broadcast_to(nblk, (SUBLANES, LANES)).astype(I32)


def _const_spec(shape):
    nd = len(shape)
    return pl.BlockSpec(shape, lambda t: (0,) * nd, pipeline_mode=pl.Buffered(1))


def _mixer(x, hist_pool, hist_conv, past, sb, ls, wts, n_tiles, tile0, shared, expert_w=None):
    nb, seq, _ = x.shape
    assert nb % sb == 0 and seq % ls == 0 and sb * ls == TOK_TILE and ls % SUBLANES == 0 and ls >= CONV_HIST
    nj = seq // ls
    n_valid = (nb // sb) * nj
    n_steps = n_valid if shared is not None else n_tiles - tile0
    own = lambda t: jnp.minimum(t, n_valid - 1)
    hist_spec = lambda n, width: pl.BlockSpec((sb, n, width), lambda t: (own(t) // nj, 0, 0))
    in_specs = [pl.BlockSpec((sb, ls, D_MODEL), lambda t: (own(t) // nj, own(t) % nj, 0)),
                hist_spec(POOL_HIST, D_POOL), hist_spec(CONV_HIST, D_CONV)]
    in_specs += [_const_spec(w.shape) for w in wts]
    shared_shapes = (
        jax.ShapeDtypeStruct((n_tiles * TOK_TILE, D_MODEL), F32),
        jax.ShapeDtypeStruct((n_tiles * TOK_TILE, ROUTER_COLS), F32),
        jax.ShapeDtypeStruct((n_tiles, TILE_ROWS, D_MODEL), BF16),
        jax.ShapeDtypeStruct((n_tiles, SUBLANES, LANES), I32),
    )
    out_shape = shared_shapes + (jax.ShapeDtypeStruct((nb, POOL_HIST, D_POOL), F32),
                                 jax.ShapeDtypeStruct((nb, CONV_HIST, D_CONV), F32))
    out_specs = (pl.BlockSpec((TOK_TILE, D_MODEL), lambda t: (tile0 + t, 0)),
                 pl.BlockSpec((TOK_TILE, ROUTER_COLS), lambda t: (tile0 + t, 0)),
                 pl.BlockSpec((None, TILE_ROWS, D_MODEL), lambda t: (tile0 + t, 0, 0)),
                 pl.BlockSpec((None, SUBLANES, LANES), lambda t: (tile0 + t, 0, 0)),
                 hist_spec(POOL_HIST, D_POOL), hist_spec(CONV_HIST, D_CONV))
    args = [x, hist_pool, hist_conv, *wts]
    if expert_w is not None:
        assert shared is None
        for w in expert_w:
            assert w.shape[0] % (n_valid * BLK) == 0
            in_specs.append(pl.BlockSpec((w.shape[0] // n_valid, w.shape[1]), lambda t: (own(t), 0)))
        wg, wu, wd = expert_w
        cast_shapes = (jax.ShapeDtypeStruct((wg.shape[0], wg.shape[1] + wu.shape[1]), BF16),
                       jax.ShapeDtypeStruct(wd.shape, BF16))
        out_shape = cast_shapes + out_shape
        out_specs = tuple(pl.BlockSpec((s.shape[0] // n_valid, s.shape[1]), lambda t: (own(t), 0))
                          for s in cast_shapes) + out_specs
        args += list(expert_w)
    aliases = {}
    if shared is not None:
        in_specs += [pl.BlockSpec(memory_space=pl.ANY)] * len(shared)
        aliases = {len(args) + k: k for k in range(len(shared))}
        args += list(shared)
    rows = sb * ls
    outs = pl.pallas_call(
        functools.partial(_mixer_kernel, past, sb, ls, nj, n_valid, n_steps, expert_w is not None),
        grid=(n_steps,),
        in_specs=in_specs,
        out_specs=out_specs,
        out_shape=out_shape,
        input_output_aliases=aliases,
        scratch_shapes=[pltpu.VMEM((sb, POOL_PAD + ls, D_POOL), F32),
                        pltpu.VMEM((sb, CONV_PAD + ls, D_CONV), F32),
                        pltpu.VMEM((SUBLANES - 1, sb, CONV_PAD + ls, D_CONV), F32),
                        pltpu.VMEM((CONV_CHUNK, D_CONV), F32),
                        pltpu.VMEM((rows, D_CONV), BF16)],
        compiler_params=pltpu.CompilerParams(dimension_semantics=("arbitrary",),
                                             vmem_limit_bytes=VMEM_LIMIT),
        name="mixer",
    )(*args)
    return outs if expert_w is None else (*outs[2:], *outs[:2])


def _chunk_tables(nblk, n_chunks_max):
    n_tiles = nblk.shape[0]
    e_ids = jnp.arange(N_EXPERTS, dtype=I32)
    per_e = jnp.sum(nblk, axis=0)
    chunks_e = -(-per_e // CHUNK_BLOCKS)
    cend = jnp.cumsum(chunks_e)
    cstart = cend - chunks_e
    n_chunks = cend[-1]
    c = jnp.arange(n_chunks_max, dtype=I32)
    chunk_e = jnp.minimum(jnp.sum((cend[None, :] <= c[:, None]).astype(I32), axis=1), N_EXPERTS - 1)
    is_e = chunk_e[:, None] == e_ids[None, :]
    of_chunk = lambda v: jnp.sum(jnp.where(is_e, v[None, :], 0), axis=1)
    of_chunk_t = lambda a: jnp.sum(jnp.where(is_e[:, None, :], a[None, :, :], 0), axis=2)
    cum_t = jnp.cumsum(nblk, axis=0)
    bstart = jnp.cumsum(nblk, axis=1) - nblk
    k = ((c - of_chunk(cstart)) * CHUNK_BLOCKS)[:, None] + jnp.arange(CHUNK_BLOCKS, dtype=I32)[None, :]
    tile = jnp.sum((of_chunk_t(cum_t)[:, None, :] <= k[:, :, None]).astype(I32), axis=2)
    is_t = tile[:, :, None] == jnp.arange(n_tiles, dtype=I32)[None, None, :]
    shift = of_chunk_t(bstart - (cum_t - nblk))
    blk = k + jnp.sum(jnp.where(is_t, shift[:, None, :], 0), axis=2)
    valid = (k < of_chunk(per_e)[:, None]) & (c[:, None] < n_chunks)
    table = jnp.where(valid, tile * TILE_BLOCKS + blk, -1)
    return table.reshape(-1).astype(I32), chunk_e.astype(I32), n_chunks.reshape(1).astype(I32)


def _expert_kernel(spare0, tab_ref, ce_ref, nch_ref, x_hbm, wgu_ref, wd_ref, y_hbm,
                   xin, yout, sem_in, sem_out):
    c = pl.program_id(0)
    nc = pl.num_programs(0)
    nch = nch_ref[0]

    zero_block = spare0 + RING_SLOTS * CHUNK_BLOCKS

    def src_block(chunk, s):
        t = tab_ref[jnp.minimum(chunk, nc - 1) * CHUNK_BLOCKS + s]
        return jnp.where((t < 0) | (chunk >= nch), zero_block, t)

    def dst_block(chunk, s):
        t = tab_ref[jnp.maximum(chunk, 0) * CHUNK_BLOCKS + s]
        return jnp.where((t < 0) | (chunk < 0), spare0 + (chunk % RING_SLOTS) * CHUNK_BLOCKS + s, t)

    def in_copy(chunk, s):
        slot = chunk % RING_SLOTS
        return pltpu.make_async_copy(x_hbm.at[src_block(chunk, s)], xin.at[slot, pl.ds(s * BLK, BLK), :],
                                     sem_in.at[slot])

    def out_copy(chunk, s):
        slot = chunk % RING_SLOTS
        return pltpu.make_async_copy(yout.at[slot, pl.ds(s * BLK, BLK), :], y_hbm.at[dst_block(chunk, s)],
                                     sem_out.at[slot])

    @pl.when(c < nch)
    def _():
        @pl.when(c == 0)
        def _():
            yout[...] = jnp.zeros(yout.shape, yout.dtype)
            for before in range(RING_SLOTS, 1, -1):
                for s in range(CHUNK_BLOCKS):
                    out_copy(c - before, s).start()
            for first in range(RING_SLOTS - 1):
                for s in range(CHUNK_BLOCKS):
                    in_copy(c + first, s).start()

        for s in range(CHUNK_BLOCKS):
            in_copy(c + RING_SLOTS - 1, s).start()
        for s in range(CHUNK_BLOCKS):
            in_copy(c, s).wait()

        ab = _dot(xin[c % RING_SLOTS], wgu_ref[...])

        for s in range(CHUNK_BLOCKS):
            out_copy(c - 1, s).start()

        a = ab[:, 0:D_EXPERT]
        b = ab[:, D_EXPERT:2 * D_EXPERT]
        act = ((a * jax.nn.sigmoid(a)) * b).astype(BF16)

        for s in range(CHUNK_BLOCKS):
            out_copy(c - RING_SLOTS, s).wait()
        yout[c % RING_SLOTS] = _dot(act, wd_ref[...]).astype(BF16)

        @pl.when(c == nch - 1)
        def _():
            for s in range(CHUNK_BLOCKS):
                out_copy(c, s).start()
            for ahead in range(1, RING_SLOTS):
                for s in range(CHUNK_BLOCKS):
                    in_copy(c + ahead, s).wait()
            for before in range(RING_SLOTS - 1, -1, -1):
                for s in range(CHUNK_BLOCKS):
                    out_copy(c - before, s).wait()


def _experts(xs_blocks, n_used_blocks, table, chunk_e, n_chunks, w_gate_up, w_down):
    assert xs_blocks.shape[0] - n_used_blocks > RING_SLOTS * CHUNK_BLOCKS
    n_steps = chunk_e.shape[0]
    w_spec = lambda shape: pl.BlockSpec((None,) + shape, lambda c, tab, ce, nch: (ce[c], 0, 0))
    return pl.pallas_call(
        functools.partial(_expert_kernel, n_used_blocks),
        grid_spec=pltpu.PrefetchScalarGridSpec(
            num_scalar_prefetch=3,
            grid=(n_steps,),
            in_specs=[pl.BlockSpec(memory_space=pl.ANY),
                      w_spec((D_MODEL, 2 * D_EXPERT)), w_spec((D_EXPERT, D_MODEL))],
            out_specs=pl.BlockSpec(memory_space=pl.ANY),
            scratch_shapes=[pltpu.VMEM((RING_SLOTS, CHUNK_ROWS, D_MODEL), BF16),
                            pltpu.VMEM((RING_SLOTS, CHUNK_ROWS, D_MODEL), BF16),
                            pltpu.SemaphoreType.DMA((RING_SLOTS,)),
                            pltpu.SemaphoreType.DMA((RING_SLOTS,))]),
        out_shape=jax.ShapeDtypeStruct(xs_blocks.shape, xs_blocks.dtype),
        input_output_aliases={3: 0},
        compiler_params=pltpu.CompilerParams(dimension_semantics=("arbitrary",),
                                             vmem_limit_bytes=VMEM_LIMIT),
        name="experts",
    )(table, chunk_e, n_chunks, xs_blocks, w_gate_up, w_down)


def _combine_kernel(tiles, y_ref, route_ref, x1_ref, gfin_ref, o_ref):
    for k in range(tiles):
        rs = slice(k * TOK_TILE, (k + 1) * TOK_TILE)
        route = route_ref[rs, :]
        pos1 = route[:, ROUTE_POS1:ROUTE_POS1 + 1].astype(I32)
        pos2 = route[:, ROUTE_POS2:ROUTE_POS2 + 1].astype(I32)
        w1 = route[:, ROUTE_W1:ROUTE_W1 + 1]
        w2 = route[:, ROUTE_W2:ROUTE_W2 + 1]
        r_idx = lax.broadcasted_iota(I32, (TOK_TILE, TILE_ROWS), 1)
        gather = jnp.where(r_idx == pos1, w1, jnp.where(r_idx == pos2, w2, 0.0)).astype(BF16)
        y = y_ref[k * TILE_BLOCKS:(k + 1) * TILE_BLOCKS].reshape(TILE_ROWS, D_MODEL)
        o_ref[rs, :] = _rmsnorm(x1_ref[rs, :] + _dot(gather, y), gfin_ref[...])


def _combine(y_blocks, route, x1, g_final, tile0, n_tiles, tiles_per_step):
    assert n_tiles % tiles_per_step == 0 and tile0 % tiles_per_step == 0
    step0 = tile0 // tiles_per_step
    tok_spec = lambda width: pl.BlockSpec((tiles_per_step * TOK_TILE, width), lambda i: (step0 + i, 0))
    return pl.pallas_call(
        functools.partial(_combine_kernel, tiles_per_step),
        grid=(n_tiles // tiles_per_step,),
        in_specs=[pl.BlockSpec((tiles_per_step * TILE_BLOCKS, BLK, D_MODEL), lambda i: (step0 + i, 0, 0)),
                  tok_spec(ROUTER_COLS), tok_spec(D_MODEL),
                  pl.BlockSpec((1, D_MODEL), lambda i: (0, 0))],
        out_specs=pl.BlockSpec((tiles_per_step * TOK_TILE, D_MODEL), lambda i: (i, 0)),
        out_shape=jax.ShapeDtypeStruct((n_tiles * TOK_TILE, D_MODEL), F32),
        compiler_params=pltpu.CompilerParams(dimension_semantics=("arbitrary",),
                                             vmem_limit_bytes=VMEM_LIMIT),
        name="combine",
    )(y_blocks, route, x1, g_final)


def kernel(x_prompt, x_sample, state_pool, state_conv, g_mix, w_in, w_pool_grp, pool_scale, w_pool_out, w_dw, b_dw, ln_g, ln_b, w_conv_out, w_out, g_ffn, w_rg, b_rg, w_re, b_re, w_gate, w_up, w_down, g_final):
    assert g_mix.shape[0] == 1
    l = 0
    bp, seq_p, _ = x_prompt.shape
    bs, seq_s, _ = x_sample.shape
    tiles_p = bp * seq_p // TOK_TILE
    tiles_s = bs * seq_s // TOK_TILE
    n_tiles = tiles_p + tiles_s

    n_pad = ROUTER_COLS - N_EXPERTS - N_EXPERT_GROUPS
    w_r = jnp.concatenate([w_re[l], w_rg[l], jnp.zeros((D_MODEL, n_pad), F32)], axis=1)
    b_r = jnp.concatenate([b_re[l], b_rg[l], jnp.zeros((n_pad,), F32)]).reshape(1, ROUTER_COLS)
    w_r_hi = w_r.astype(BF16)
    w_r_cat = jnp.concatenate([w_r_hi, (w_r - w_r_hi.astype(F32)).astype(BF16)], axis=1)
    row = lambda a: a.reshape(1, -1)
    wts = (row(g_mix[l]), w_in[l].astype(BF16), w_pool_grp[l].astype(BF16), row(pool_scale[l]),
           w_pool_out[l].astype(BF16), w_dw[l], row(b_dw[l]), row(ln_g[l]), row(ln_b[l]),
           w_conv_out[l].astype(BF16), w_out[l].astype(BF16), row(g_ffn[l]), w_r_cat, b_r)

    zp = jnp.zeros((bp, POOL_HIST, D_POOL), F32)
    zc = jnp.zeros((bp, CONV_HIST, D_CONV), F32)
    n_all = n_tiles + -(-(RING_SLOTS * CHUNK_BLOCKS + 1) // TILE_BLOCKS)
    expert_w = (w_gate[l].reshape(N_EXPERTS * D_MODEL, D_EXPERT), w_up[l].reshape(N_EXPERTS * D_MODEL, D_EXPERT),
                w_down[l].reshape(N_EXPERTS * D_EXPERT, D_MODEL))
    *shared, npp, ncp, w_gate_up, w_down_bf = _mixer(x_prompt, zp, zc, 0, 1, TOK_TILE, wts, n_all, 0, None, expert_w)
    x1, route, xs, nblk, nps, ncs = _mixer(x_sample, state_pool[l], state_conv[l], PAST_LEN, bs, seq_s, wts,
                                           n_all, tiles_p, shared)

    n_chunks_max = -(-n_tiles * TILE_BLOCKS // CHUNK_BLOCKS) + N_EXPERTS
    table, chunk_e, n_chunks = _chunk_tables(nblk[:n_tiles, 0, :N_EXPERTS], n_chunks_max)
    y_blocks = _experts(xs.reshape(n_all * TILE_BLOCKS, BLK, D_MODEL), n_tiles * TILE_BLOCKS,
                        table, chunk_e, n_chunks,
                        w_gate_up.reshape(N_EXPERTS, D_MODEL, 2 * D_EXPERT),
                        w_down_bf.reshape(N_EXPERTS, D_EXPERT, D_MODEL))
    tps = COMBINE_TILES if tiles_p % COMBINE_TILES == 0 else 1
    yp = _combine(y_blocks, route, x1, row(g_final), 0, tiles_p, tps)
    ys = _combine(y_blocks, route, x1, row(g_final), tiles_p, tiles_s, 1)
    return (yp.reshape(bp, seq_p, D_MODEL), ys.reshape(bs, seq_s, D_MODEL),
            npp[None], ncp[None], nps[None], ncs[None])
```

```python
import functools

import jax
import jax.numpy as jnp
from jax import lax
from jax.experimental import pallas as pl
from jax.experimental.pallas import tpu as pltpu

D_MODEL = 1024
D_POOL = 512
N_POOL_GROUPS = 4
POOL_GROUP = D_POOL // N_POOL_GROUPS
POOL_WINDOWS = (2, 4, 8, 16)
POOL_HIST = max(POOL_WINDOWS) - 1
D_CONV = 512
CONV_WIDTH = 31
CONV_HIST = CONV_WIDTH - 1
N_EXPERT_GROUPS = 4
EXPERTS_PER_GROUP = 8
N_EXPERTS = N_EXPERT_GROUPS * EXPERTS_PER_GROUP
TOP_K = 2
D_EXPERT = 256
RMS_EPS = 1e-6
LN_EPS = 1e-5
PAST_LEN = 1024

LANES = 128
SUBLANES = 8
POOL_PAD = 16
CONV_PAD = 32
CONV_CHUNK = 64
ROUTER_COLS = LANES
VMEM_LIMIT = 56 * 1024 * 1024

TOK_TILE = 512
BLK = 2 * SUBLANES
TILE_BLOCKS = (TOP_K * TOK_TILE + N_EXPERTS * (BLK - 1)) // BLK
TILE_ROWS = TILE_BLOCKS * BLK
CHUNK_BLOCKS = 32
CHUNK_ROWS = CHUNK_BLOCKS * BLK
RING_SLOTS = 4
ROUTE_POS1, ROUTE_POS2, ROUTE_W1, ROUTE_W2 = 0, 1, 2, 3
COMBINE_TILES = 2

BF16 = jnp.bfloat16
F32 = jnp.float32
I32 = jnp.int32


def _dot(a, b):
    return jnp.dot(a, b, preferred_element_type=F32)


def _rmsnorm(x, g):
    return x * lax.rsqrt(jnp.mean(x * x, axis=-1, keepdims=True) + RMS_EPS) * g


def _routing(logits):
    lane = lax.broadcasted_iota(I32, logits.shape, 1)
    lane_f = lane.astype(F32)
    neg = jnp.float32(-jnp.inf)
    big = jnp.float32(1e9)
    is_grp = (lane >= N_EXPERTS) & (lane < N_EXPERTS + N_EXPERT_GROUPS)
    glog = jnp.where(is_grp, logits, neg)
    gmax = jnp.max(glog, axis=1, keepdims=True)
    gidx = jnp.min(jnp.where(glog == gmax, lane_f, big), axis=1, keepdims=True) - float(N_EXPERTS)
    gsum = jnp.sum(jnp.where(is_grp, jnp.exp(glog - gmax), 0.0), axis=1, keepdims=True)
    p_sel = 1.0 / gsum
    lane_grp = (lane >> 3).astype(F32)
    is_sel = (lane < N_EXPERTS) & (lane_grp == gidx)
    elog = jnp.where(is_sel, logits, neg)
    v1 = jnp.max(elog, axis=1, keepdims=True)
    i1 = jnp.min(jnp.where(elog == v1, lane_f, big), axis=1, keepdims=True)
    elog2 = jnp.where(lane_f == i1, neg, elog)
    v2 = jnp.max(elog2, axis=1, keepdims=True)
    i2 = jnp.min(jnp.where(elog2 == v2, lane_f, big), axis=1, keepdims=True)
    t = jnp.exp(v2 - v1)
    return i1, i2, p_sel / (1.0 + t), p_sel * (t / (1.0 + t))


def _dispatch(h2, i1, i2, w1, w2):
    rows = h2.shape[0]
    lane = lax.broadcasted_iota(I32, (rows, LANES), 1)
    lane_f = lane.astype(F32)
    hit1 = lane_f == i1
    hit2 = lane_f == i2
    sel = jnp.where(hit1 | hit2, 1.0, 0.0)
    earlier = lax.broadcasted_iota(I32, (rows, rows), 1) < lax.broadcasted_iota(I32, (rows, rows), 0)
    rank = _dot(jnp.where(earlier, 1.0, 0.0).astype(BF16), sel.astype(BF16))
    cnt = jnp.sum(sel, axis=0, keepdims=True)
    nblk = jnp.floor((cnt + float(BLK - 1)) * (1.0 / BLK))
    before = lax.broadcasted_iota(I32, (LANES, LANES), 0) < lax.broadcasted_iota(I32, (LANES, LANES), 1)
    bstart = _dot(jnp.broadcast_to(nblk, (SUBLANES, LANES)).astype(BF16),
                  jnp.where(before, 1.0, 0.0).astype(BF16))[0:1, :]
    pos = float(BLK) * bstart + rank
    pos1 = jnp.sum(jnp.where(hit1, pos, 0.0), axis=1, keepdims=True)
    pos2 = jnp.sum(jnp.where(hit2, pos, 0.0), axis=1, keepdims=True)
    route = jnp.where(lane == ROUTE_POS1, pos1,
                      jnp.where(lane == ROUTE_POS2, pos2,
                                jnp.where(lane == ROUTE_W1, w1, jnp.where(lane == ROUTE_W2, w2, 0.0))))
    route_t = route.T
    p1 = route_t[ROUTE_POS1:ROUTE_POS1 + 1, :].astype(I32)
    p2 = route_t[ROUTE_POS2:ROUTE_POS2 + 1, :].astype(I32)
    r_idx = lax.broadcasted_iota(I32, (TILE_ROWS, rows), 0)
    onehot = jnp.where((r_idx == p1) | (r_idx == p2), 1.0, 0.0).astype(BF16)
    return route, _dot(onehot, h2).astype(BF16), nblk


N_MIXER_IN = 17
N_MIXER_OUT = 6
N_MIXER_SCRATCH = 5


def _mixer_kernel(past, sb, ls, nj, n_valid, n_steps, cast_experts, *refs):
    t = pl.program_id(0)
    if n_valid == n_steps:
        assert not cast_experts
        _mixer_tile(past, sb, ls, t % nj, nj, *refs)
        return
    outs = refs[-(N_MIXER_OUT + N_MIXER_SCRATCH):][:N_MIXER_OUT]

    @pl.when(t < n_valid)
    def _():
        _mixer_tile(past, sb, ls, t % nj, nj, *refs)
        if cast_experts:
            wg_ref, wu_ref, wd_ref = refs[N_MIXER_IN:N_MIXER_IN + 3]
            wgu_out, wd_out = refs[-(N_MIXER_OUT + N_MIXER_SCRATCH + 2):-(N_MIXER_OUT + N_MIXER_SCRATCH)]
            wgu_out[:, 0:D_EXPERT] = wg_ref[...].astype(BF16)
            wgu_out[:, D_EXPERT:2 * D_EXPERT] = wu_ref[...].astype(BF16)
            wd_out[...] = wd_ref[...].astype(BF16)

    @pl.when(t >= n_valid)
    def _():
        for ref in outs[:4]:
            ref[...] = jnp.zeros(ref.shape, ref.dtype)


def _mixer_tile(past, sb, ls, j, nj, *refs):
    (x_ref, hp_ref, hc_ref, gmix_ref, win_ref, wgrp_ref, pscale_ref, wpo_ref, wdw_ref, bdw_ref,
     lng_ref, lnb_ref, wco_ref, wout_ref, gffn_ref, wrc_ref, br_ref) = refs[:N_MIXER_IN]
    x1_ref, route_ref, xs_ref, nblk_ref, npool_ref, nconv_ref = refs[-(N_MIXER_OUT + N_MIXER_SCRATCH):][:N_MIXER_OUT]
    ubuf, vbuf, sbuf, cbuf, ybuf = refs[-N_MIXER_SCRATCH:]
    rows = sb * ls

    @pl.when(j == 0)
    def _():
        ubuf[:, POOL_PAD - POOL_HIST:POOL_PAD, :] = hp_ref[...]
        vbuf[:, CONV_PAD - CONV_HIST:CONV_PAD, :] = hc_ref[...]

    x = x_ref[...].reshape(rows, D_MODEL)
    h = _rmsnorm(x, gmix_ref[...]).astype(BF16)
    u = _dot(h, win_ref[:, 0:D_POOL])
    ubuf[:, POOL_PAD:POOL_PAD + ls, :] = u.reshape(sb, ls, D_POOL)
    ga = _dot(h, win_ref[:, D_POOL:D_POOL + D_CONV])
    gb = _dot(h, win_ref[:, D_POOL + D_CONV:D_POOL + 2 * D_CONV])
    vbuf[:, CONV_PAD:CONV_PAD + ls, :] = (ga * jax.nn.sigmoid(gb)).reshape(sb, ls, D_CONV)
    c_gp = D_POOL + 2 * D_CONV
    gate_pool = _dot(h, win_ref[:, c_gp:c_gp + D_MODEL])
    gate_conv = _dot(h, win_ref[:, c_gp + D_MODEL:c_gp + 2 * D_MODEL])

    t_idx = lax.broadcasted_iota(I32, (sb, ls, 1), 1)
    frames = (past + 1 + j * ls + t_idx).astype(F32)
    ys = []
    part, width = ubuf[...], 1
    for gi, w in enumerate(POOL_WINDOWS):
        c0, c1 = gi * POOL_GROUP, (gi + 1) * POOL_GROUP
        cur = ubuf[:, POOL_PAD:POOL_PAD + ls, c0:c1]
        while width < w:
            part = part + pltpu.roll(part, width, 1)
            width *= 2
        s = part[:, POOL_PAD:POOL_PAD + ls, 0:POOL_GROUP]
        part = part[:, :, POOL_GROUP:]
        mean = s / jnp.minimum(jnp.float32(w), frames)
        ys.append(_dot((mean - cur).reshape(rows, POOL_GROUP).astype(BF16), wgrp_ref[gi]))
    yp = (jnp.concatenate(ys, axis=1) * pscale_ref[...]).astype(BF16)
    y_pool = _dot(yp, wpo_ref[...])
    new_pool = ubuf[:, POOL_PAD + ls - POOL_HIST:POOL_PAD + ls, :]
    ubuf[:, POOL_PAD - POOL_HIST:POOL_PAD, :] = new_pool

    first = CONV_PAD - CONV_HIST
    taps = [[o for o in range(first, first + CONV_WIDTH) if o % SUBLANES == r] for r in range(SUBLANES)]
    for r in range(1, SUBLANES):
        span = taps[r][-1] - taps[r][0] + ls
        sbuf[r - 1, :, 0:span, :] = vbuf[:, taps[r][0]:taps[r][0] + span, :]
    nb_c, nt_c = (1, CONV_CHUNK) if ls >= CONV_CHUNK else (CONV_CHUNK // ls, ls)
    n_grp = nt_c // SUBLANES
    for i in range(rows // CONV_CHUNK):
        b0, t0 = (0, i * nt_c) if nb_c == 1 else (i * nb_c, 0)
        for c0 in range(0, D_CONV, LANES):
            cs = slice(c0, c0 + LANES)
            accs = [jnp.zeros((nb_c, SUBLANES, LANES), F32) + bdw_ref[:, cs] for _ in range(n_grp)]
            for r in range(SUBLANES):
                ws = [wdw_ref[o - first:o - first + 1, cs] for o in taps[r]]
                for q in range(n_grp + len(taps[r]) - 1):
                    lo = t0 + q * SUBLANES
                    if r == 0:
                        val = vbuf[b0:b0 + nb_c, taps[0][0] + lo:taps[0][0] + lo + SUBLANES, cs]
                    else:
                        val = sbuf[r - 1, b0:b0 + nb_c, lo:lo + SUBLANES, cs]
                    for a in range(len(taps[r])):
                        if 0 <= q - a < n_grp:
                            accs[q - a] = accs[q - a] + val * ws[a]
            cbuf[:, cs] = jnp.concatenate(accs, axis=1).reshape(CONV_CHUNK, LANES)
        yf = cbuf[...]
        mu = jnp.mean(yf, axis=-1, keepdims=True)
        yc = yf - mu
        var = jnp.mean(yc * yc, axis=-1, keepdims=True)
        yn = yc * lax.rsqrt(var + LN_EPS) * lng_ref[...] + lnb_ref[...]
        ybuf[i * CONV_CHUNK:(i + 1) * CONV_CHUNK, :] = (yn * jax.nn.sigmoid(yn)).astype(BF16)
    y_conv = _dot(ybuf[...], wco_ref[...])
    new_conv = vbuf[:, CONV_PAD + ls - CONV_HIST:CONV_PAD + ls, :]
    vbuf[:, CONV_PAD - CONV_HIST:CONV_PAD, :] = new_conv

    @pl.when(j == nj - 1)
    def _():
        npool_ref[...] = new_pool
        nconv_ref[...] = new_conv

    merged = jax.nn.sigmoid(gate_pool) * y_pool + jax.nn.sigmoid(gate_conv) * y_conv
    x1 = x + _dot(merged.astype(BF16), wout_ref[...])
    x1_ref[...] = x1

    h2 = _rmsnorm(x1, gffn_ref[...])
    h2_hi = h2.astype(BF16)
    h2_lo = (h2 - h2_hi.astype(F32)).astype(BF16)
    hi = _dot(h2_hi, wrc_ref[...])
    logits = (hi[:, 0:ROUTER_COLS] + (hi[:, ROUTER_COLS:] + _dot(h2_lo, wrc_ref[:, 0:ROUTER_COLS]))) + br_ref[...]
    route, xs, nblk = _dispatch(h2_hi, *_routing(logits))
    route_ref[...] = route
    xs_ref[...] = xs
    nblk_ref[...] = jnp.broadcast_to(nblk, (SUBLANES, LANES)).astype(I32)


def _const_spec(shape):
    nd = len(shape)
    return pl.BlockSpec(shape, lambda t: (0,) * nd, pipeline_mode=pl.Buffered(1))


def _mixer(x, hist_pool, hist_conv, past, sb, ls, wts, n_tiles, tile0, shared, expert_w=None):
    nb, seq, _ = x.shape
    assert nb % sb == 0 and seq % ls == 0 and sb * ls == TOK_TILE and ls % SUBLANES == 0 and ls >= CONV_HIST
    nj = seq // ls
    n_valid = (nb // sb) * nj
    n_steps = n_valid if shared is not None else n_tiles - tile0
    own = lambda t: jnp.minimum(t, n_valid - 1)
    hist_spec = lambda n, width: pl.BlockSpec((sb, n, width), lambda t: (own(t) // nj, 0, 0))
    in_specs = [pl.BlockSpec((sb, ls, D_MODEL), lambda t: (own(t) // nj, own(t) % nj, 0)),
                hist_spec(POOL_HIST, D_POOL), hist_spec(CONV_HIST, D_CONV)]
    in_specs += [_const_spec(w.shape) for w in wts]
    shared_shapes = (
        jax.ShapeDtypeStruct((n_tiles * TOK_TILE, D_MODEL), F32),
        jax.ShapeDtypeStruct((n_tiles * TOK_TILE, ROUTER_COLS), F32),
        jax.ShapeDtypeStruct((n_tiles, TILE_ROWS, D_MODEL), BF16),
        jax.ShapeDtypeStruct((n_tiles, SUBLANES, LANES), I32),
    )
    out_shape = shared_shapes + (jax.ShapeDtypeStruct((nb, POOL_HIST, D_POOL), F32),
                                 jax.ShapeDtypeStruct((nb, CONV_HIST, D_CONV), F32))
    out_specs = (pl.BlockSpec((TOK_TILE, D_MODEL), lambda t: (tile0 + t, 0)),
                 pl.BlockSpec((TOK_TILE, ROUTER_COLS), lambda t: (tile0 + t, 0)),
                 pl.BlockSpec((None, TILE_ROWS, D_MODEL), lambda t: (tile0 + t, 0, 0)),
                 pl.BlockSpec((None, SUBLANES, LANES), lambda t: (tile0 + t, 0, 0)),
                 hist_spec(POOL_HIST, D_POOL), hist_spec(CONV_HIST, D_CONV))
    args = [x, hist_pool, hist_conv, *wts]
    if expert_w is not None:
        assert shared is None
        for w in expert_w:
            assert w.shape[0] % (n_valid * BLK) == 0
            in_specs.append(pl.BlockSpec((w.shape[0] // n_valid, w.shape[1]), lambda t: (own(t), 0)))
        wg, wu, wd = expert_w
        cast_shapes = (jax.ShapeDtypeStruct((wg.shape[0], wg.shape[1] + wu.shape[1]), BF16),
                       jax.ShapeDtypeStruct(wd.shape, BF16))
        out_shape = cast_shapes + out_shape
        out_specs = tuple(pl.BlockSpec((s.shape[0] // n_valid, s.shape[1]), lambda t: (own(t), 0))
                          for s in cast_shapes) + out_specs
        args += list(expert_w)
    aliases = {}
    if shared is not None:
        in_specs += [pl.BlockSpec(memory_space=pl.ANY)] * len(shared)
        aliases = {len(args) + k: k for k in range(len(shared))}
        args += list(shared)
    rows = sb * ls
    outs = pl.pallas_call(
        functools.partial(_mixer_kernel, past, sb, ls, nj, n_valid, n_steps, expert_w is not None),
        grid=(n_steps,),
        in_specs=in_specs,
        out_specs=out_specs,
        out_shape=out_shape,
        input_output_aliases=aliases,
        scratch_shapes=[pltpu.VMEM((sb, POOL_PAD + ls, D_POOL), F32),
                        pltpu.VMEM((sb, CONV_PAD + ls, D_CONV), F32),
                        pltpu.VMEM((SUBLANES - 1, sb, CONV_PAD + ls, D_CONV), F32),
                        pltpu.VMEM((CONV_CHUNK, D_CONV), F32),
                        pltpu.VMEM((rows, D_CONV), BF16)],
        compiler_params=pltpu.CompilerParams(dimension_semantics=("arbitrary",),
                                             vmem_limit_bytes=VMEM_LIMIT),
        name="mixer",
    )(*args)
    return outs if expert_w is None else (*outs[2:], *outs[:2])


def _chunk_tables(nblk, n_chunks_max):
    n_tiles = nblk.shape[0]
    e_ids = jnp.arange(N_EXPERTS, dtype=I32)
    per_e = jnp.sum(nblk, axis=0)
    chunks_e = -(-per_e // CHUNK_BLOCKS)
    cend = jnp.cumsum(chunks_e)
    cstart = cend - chunks_e
    n_chunks = cend[-1]
    c = jnp.arange(n_chunks_max, dtype=I32)
    chunk_e = jnp.minimum(jnp.sum((cend[None, :] <= c[:, None]).astype(I32), axis=1), N_EXPERTS - 1)
    is_e = chunk_e[:, None] == e_ids[None, :]
    of_chunk = lambda v: jnp.sum(jnp.where(is_e, v[None, :], 0), axis=1)
    of_chunk_t = lambda a: jnp.sum(jnp.where(is_e[:, None, :], a[None, :, :], 0), axis=2)
    cum_t = jnp.cumsum(nblk, axis=0)
    bstart = jnp.cumsum(nblk, axis=1) - nblk
    k = ((c - of_chunk(cstart)) * CHUNK_BLOCKS)[:, None] + jnp.arange(CHUNK_BLOCKS, dtype=I32)[None, :]
    tile = jnp.sum((of_chunk_t(cum_t)[:, None, :] <= k[:, :, None]).astype(I32), axis=2)
    is_t = tile[:, :, None] == jnp.arange(n_tiles, dtype=I32)[None, None, :]
    shift = of_chunk_t(bstart - (cum_t - nblk))
    blk = k + jnp.sum(jnp.where(is_t, shift[:, None, :], 0), axis=2)
    valid = (k < of_chunk(per_e)[:, None]) & (c[:, None] < n_chunks)
    table = jnp.where(valid, tile * TILE_BLOCKS + blk, -1)
    return table.reshape(-1).astype(I32), chunk_e.astype(I32), n_chunks.reshape(1).astype(I32)


def _expert_kernel(spare0, tab_ref, ce_ref, nch_ref, x_hbm, wgu_ref, wd_ref, y_hbm,
                   xin, yout, sem_in, sem_out):
    c = pl.program_id(0)
    nc = pl.num_programs(0)
    nch = nch_ref[0]

    zero_block = spare0 + RING_SLOTS * CHUNK_BLOCKS

    def src_block(chunk, s):
        t = tab_ref[jnp.minimum(chunk, nc - 1) * CHUNK_BLOCKS + s]
        return jnp.where((t < 0) | (chunk >= nch), zero_block, t)

    def dst_block(chunk, s):
        t = tab_ref[jnp.maximum(chunk, 0) * CHUNK_BLOCKS + s]
        return jnp.where((t < 0) | (chunk < 0), spare0 + (chunk % RING_SLOTS) * CHUNK_BLOCKS + s, t)

    def in_copy(chunk, s):
        slot = chunk % RING_SLOTS
        return pltpu.make_async_copy(x_hbm.at[src_block(chunk, s)], xin.at[slot, pl.ds(s * BLK, BLK), :],
                                     sem_in.at[slot])

    def out_copy(chunk, s):
        slot = chunk % RING_SLOTS
        return pltpu.make_async_copy(yout.at[slot, pl.ds(s * BLK, BLK), :], y_hbm.at[dst_block(chunk, s)],
                                     sem_out.at[slot])

    @pl.when(c < nch)
    def _():
        @pl.when(c == 0)
        def _():
            yout[...] = jnp.zeros(yout.shape, yout.dtype)
            for before in range(RING_SLOTS, 1, -1):
                for s in range(CHUNK_BLOCKS):
                    out_copy(c - before, s).start()
            for first in range(RING_SLOTS - 1):
                for s in range(CHUNK_BLOCKS):
                    in_copy(c + first, s).start()

        for s in range(CHUNK_BLOCKS):
            in_copy(c + RING_SLOTS - 1, s).start()
        for s in range(CHUNK_BLOCKS):
            in_copy(c, s).wait()

        ab = _dot(xin[c % RING_SLOTS], wgu_ref[...])

        for s in range(CHUNK_BLOCKS):
            out_copy(c - 1, s).start()

        a = ab[:, 0:D_EXPERT]
        b = ab[:, D_EXPERT:2 * D_EXPERT]
        act = ((a * jax.nn.sigmoid(a)) * b).astype(BF16)

        for s in range(CHUNK_BLOCKS):
            out_copy(c - RING_SLOTS, s).wait()
        yout[c % RING_SLOTS] = _dot(act, wd_ref[...]).astype(BF16)

        @pl.when(c == nch - 1)
        def _():
            for s in range(CHUNK_BLOCKS):
                out_copy(c, s).start()
            for ahead in range(1, RING_SLOTS):
                for s in range(CHUNK_BLOCKS):
                    in_copy(c + ahead, s).wait()
            for before in range(RING_SLOTS - 1, -1, -1):
                for s in range(CHUNK_BLOCKS):
                    out_copy(c - before, s).wait()


def _experts(xs_blocks, n_used_blocks, table, chunk_e, n_chunks, w_gate_up, w_down):
    assert xs_blocks.shape[0] - n_used_blocks > RING_SLOTS * CHUNK_BLOCKS
    n_steps = chunk_e.shape[0]
    w_spec = lambda shape: pl.BlockSpec((None,) + shape, lambda c, tab, ce, nch: (ce[c], 0, 0))
    return pl.pallas_call(
        functools.partial(_expert_kernel, n_used_blocks),
        grid_spec=pltpu.PrefetchScalarGridSpec(
            num_scalar_prefetch=3,
            grid=(n_steps,),
            in_specs=[pl.BlockSpec(memory_space=pl.ANY),
                      w_spec((D_MODEL, 2 * D_EXPERT)), w_spec((D_EXPERT, D_MODEL))],
            out_specs=pl.BlockSpec(memory_space=pl.ANY),
            scratch_shapes=[pltpu.VMEM((RING_SLOTS, CHUNK_ROWS, D_MODEL), BF16),
                            pltpu.VMEM((RING_SLOTS, CHUNK_ROWS, D_MODEL), BF16),
                            pltpu.SemaphoreType.DMA((RING_SLOTS,)),
                            pltpu.SemaphoreType.DMA((RING_SLOTS,))]),
        out_shape=jax.ShapeDtypeStruct(xs_blocks.shape, xs_blocks.dtype),
        input_output_aliases={3: 0},
        compiler_params=pltpu.CompilerParams(dimension_semantics=("arbitrary",),
                                             vmem_limit_bytes=VMEM_LIMIT),
        name="experts",
    )(table, chunk_e, n_chunks, xs_blocks, w_gate_up, w_down)


def _combine_kernel(tiles, y_ref, route_ref, x1_ref, gfin_ref, o_ref):
    for k in range(tiles):
        rs = slice(k * TOK_TILE, (k + 1) * TOK_TILE)
        route = route_ref[rs, :]
        pos1 = route[:, ROUTE_POS1:ROUTE_POS1 + 1].astype(I32)
        pos2 = route[:, ROUTE_POS2:ROUTE_POS2 + 1].astype(I32)
        w1 = route[:, ROUTE_W1:ROUTE_W1 + 1]
        w2 = route[:, ROUTE_W2:ROUTE_W2 + 1]
        r_idx = lax.broadcasted_iota(I32, (TOK_TILE, TILE_ROWS), 1)
        gather = jnp.where(r_idx == pos1, w1, jnp.where(r_idx == pos2, w2, 0.0)).astype(BF16)
        y = y_ref[k * TILE_BLOCKS:(k + 1) * TILE_BLOCKS].reshape(TILE_ROWS, D_MODEL)
        o_ref[rs, :] = _rmsnorm(x1_ref[rs, :] + _dot(gather, y), gfin_ref[...])


def _combine(y_blocks, route, x1, g_final, tile0, n_tiles, tiles_per_step):
    assert n_tiles % tiles_per_step == 0 and tile0 % tiles_per_step == 0
    step0 = tile0 // tiles_per_step
    tok_spec = lambda width: pl.BlockSpec((tiles_per_step * TOK_TILE, width), lambda i: (step0 + i, 0))
    return pl.pallas_call(
        functools.partial(_combine_kernel, tiles_per_step),
        grid=(n_tiles // tiles_per_step,),
        in_specs=[pl.BlockSpec((tiles_per_step * TILE_BLOCKS, BLK, D_MODEL), lambda i: (step0 + i, 0, 0)),
                  tok_spec(ROUTER_COLS), tok_spec(D_MODEL),
                  pl.BlockSpec((1, D_MODEL), lambda i: (0, 0))],
        out_specs=pl.BlockSpec((tiles_per_step * TOK_TILE, D_MODEL), lambda i: (i, 0)),
        out_shape=jax.ShapeDtypeStruct((n_tiles * TOK_TILE, D_MODEL), F32),
        compiler_params=pltpu.CompilerParams(dimension_semantics=("arbitrary",),
                                             vmem_limit_bytes=VMEM_LIMIT),
        name="combine",
    )(y_blocks, route, x1, g_final)


def kernel(x_prompt, x_sample, state_pool, state_conv, g_mix, w_in, w_pool_grp, pool_scale, w_pool_out, w_dw, b_dw, ln_g, ln_b, w_conv_out, w_out, g_ffn, w_rg, b_rg, w_re, b_re, w_gate, w_up, w_down, g_final):
    assert g_mix.shape[0] == 1
    l = 0
    bp, seq_p, _ = x_prompt.shape
    bs, seq_s, _ = x_sample.shape
    tiles_p = bp * seq_p // TOK_TILE
    tiles_s = bs * seq_s // TOK_TILE
    n_tiles = tiles_p + tiles_s

    n_pad = ROUTER_COLS - N_EXPERTS - N_EXPERT_GROUPS
    w_r = jnp.concatenate([w_re[l], w_rg[l], jnp.zeros((D_MODEL, n_pad), F32)], axis=1)
    b_r = jnp.concatenate([b_re[l], b_rg[l], jnp.zeros((n_pad,), F32)]).reshape(1, ROUTER_COLS)
    w_r_hi = w_r.astype(BF16)
    w_r_cat = jnp.concatenate([w_r_hi, (w_r - w_r_hi.astype(F32)).astype(BF16)], axis=1)
    row = lambda a: a.reshape(1, -1)
    wts = (row(g_mix[l]), w_in[l].astype(BF16), w_pool_grp[l].astype(BF16), row(pool_scale[l]),
           w_pool_out[l].astype(BF16), w_dw[l], row(b_dw[l]), row(ln_g[l]), row(ln_b[l]),
           w_conv_out[l].astype(BF16), w_out[l].astype(BF16), row(g_ffn[l]), w_r_cat, b_r)

    zp = jnp.zeros((bp, POOL_HIST, D_POOL), F32)
    zc = jnp.zeros((bp, CONV_HIST, D_CONV), F32)
    n_all = n_tiles + -(-(RING_SLOTS * CHUNK_BLOCKS + 1) // TILE_BLOCKS)
    expert_w = (w_gate[l].reshape(N_EXPERTS * D_MODEL, D_EXPERT), w_up[l].reshape(N_EXPERTS * D_MODEL, D_EXPERT),
                w_down[l].reshape(N_EXPERTS * D_EXPERT, D_MODEL))
    *shared, npp, ncp, w_gate_up, w_down_bf = _mixer(x_prompt, zp, zc, 0, 1, TOK_TILE, wts, n_all, 0, None, expert_w)
    x1, route, xs, nblk, nps, ncs = _mixer(x_sample, state_pool[l], state_conv[l], PAST_LEN, bs, seq_s, wts,
                                           n_all, tiles_p, shared)

    n_chunks_max = -(-n_tiles * TILE_BLOCKS // CHUNK_BLOCKS) + N_EXPERTS
    table, chunk_e, n_chunks = _chunk_tables(nblk[:n_tiles, 0, :N_EXPERTS], n_chunks_max)
    y_blocks = _experts(xs.reshape(n_all * TILE_BLOCKS, BLK, D_MODEL), n_tiles * TILE_BLOCKS,
                        table, chunk_e, n_chunks,
                        w_gate_up.reshape(N_EXPERTS, D_MODEL, 2 * D_EXPERT),
                        w_down_bf.reshape(N_EXPERTS, D_EXPERT, D_MODEL))
    tps = COMBINE_TILES if tiles_p % COMBINE_TILES == 0 else 1
    yp = _combine(y_blocks, route, x1, row(g_final), 0, tiles_p, tps)
    ys = _combine(y_blocks, route, x1, row(g_final), tiles_p, tiles_s, 1)
    return (yp.reshape(bp, seq_p, D_MODEL), ys.reshape(bs, seq_s, D_MODEL),
            npp[None], ncp[None], nps[None], ncs[None])
```

```python
import functools

import jax
import jax.numpy as jnp
from jax import lax
from jax.experimental import pallas as pl
from jax.experimental.pallas import tpu as pltpu

D_MODEL = 1024
D_POOL = 512
N_POOL_GROUPS = 4
POOL_GROUP = D_POOL // N_POOL_GROUPS
POOL_WINDOWS = (2, 4, 8, 16)
POOL_HIST = max(POOL_WINDOWS) - 1
D_CONV = 512
CONV_WIDTH = 31
CONV_HIST = CONV_WIDTH - 1
N_EXPERT_GROUPS = 4
EXPERTS_PER_GROUP = 8
N_EXPERTS = N_EXPERT_GROUPS * EXPERTS_PER_GROUP
TOP_K = 2
D_EXPERT = 256
RMS_EPS = 1e-6
LN_EPS = 1e-5
PAST_LEN = 1024

LANES = 128
SUBLANES = 8
POOL_PAD = 16
CONV_PAD = 32
CONV_CHUNK = 64
ROUTER_COLS = LANES
VMEM_LIMIT = 56 * 1024 * 1024

TOK_TILE = 512
BLK = 2 * SUBLANES
TILE_BLOCKS = (TOP_K * TOK_TILE + N_EXPERTS * (BLK - 1)) // BLK
TILE_ROWS = TILE_BLOCKS * BLK
CHUNK_BLOCKS = 32
CHUNK_ROWS = CHUNK_BLOCKS * BLK
RING_SLOTS = 3
ROUTE_POS1, ROUTE_POS2, ROUTE_W1, ROUTE_W2 = 0, 1, 2, 3
COMBINE_TILES = 2

BF16 = jnp.bfloat16
F32 = jnp.float32
I32 = jnp.int32


def _dot(a, b):
    return jnp.dot(a, b, preferred_element_type=F32)


def _rmsnorm(x, g):
    return x * lax.rsqrt(jnp.mean(x * x, axis=-1, keepdims=True) + RMS_EPS) * g


def _routing(logits):
    lane = lax.broadcasted_iota(I32, logits.shape, 1)
    lane_f = lane.astype(F32)
    neg = jnp.float32(-jnp.inf)
    big = jnp.float32(1e9)
    is_grp = (lane >= N_EXPERTS) & (lane < N_EXPERTS + N_EXPERT_GROUPS)
    glog = jnp.where(is_grp, logits, neg)
    gmax = jnp.max(glog, axis=1, keepdims=True)
    gidx = jnp.min(jnp.where(glog == gmax, lane_f, big), axis=1, keepdims=True) - float(N_EXPERTS)
    gsum = jnp.sum(jnp.where(is_grp, jnp.exp(glog - gmax), 0.0), axis=1, keepdims=True)
    p_sel = 1.0 / gsum
    lane_grp = (lane >> 3).astype(F32)
    is_sel = (lane < N_EXPERTS) & (lane_grp == gidx)
    elog = jnp.where(is_sel, logits, neg)
    v1 = jnp.max(elog, axis=1, keepdims=True)
    i1 = jnp.min(jnp.where(elog == v1, lane_f, big), axis=1, keepdims=True)
    elog2 = jnp.where(lane_f == i1, neg, elog)
    v2 = jnp.max(elog2, axis=1, keepdims=True)
    i2 = jnp.min(jnp.where(elog2 == v2, lane_f, big), axis=1, keepdims=True)
    t = jnp.exp(v2 - v1)
    return i1, i2, p_sel / (1.0 + t), p_sel * (t / (1.0 + t))


def _dispatch(h2, i1, i2, w1, w2):
    rows = h2.shape[0]
    lane = lax.broadcasted_iota(I32, (rows, LANES), 1)
    lane_f = lane.astype(F32)
    hit1 = lane_f == i1
    hit2 = lane_f == i2
    sel = jnp.where(hit1 | hit2, 1.0, 0.0)
    earlier = lax.broadcasted_iota(I32, (rows, rows), 1) < lax.broadcasted_iota(I32, (rows, rows), 0)
    rank = _dot(jnp.where(earlier, 1.0, 0.0).astype(BF16), sel.astype(BF16))
    cnt = jnp.sum(sel, axis=0, keepdims=True)
    nblk = jnp.floor((cnt + float(BLK - 1)) * (1.0 / BLK))
    before = lax.broadcasted_iota(I32, (LANES, LANES), 0) < lax.broadcasted_iota(I32, (LANES, LANES), 1)
    bstart = _dot(jnp.broadcast_to(nblk, (SUBLANES, LANES)).astype(BF16),
                  jnp.where(before, 1.0, 0.0).astype(BF16))[0:1, :]
    pos = float(BLK) * bstart + rank
    pos1 = jnp.sum(jnp.where(hit1, pos, 0.0), axis=1, keepdims=True)
    pos2 = jnp.sum(jnp.where(hit2, pos, 0.0), axis=1, keepdims=True)
    route = jnp.where(lane == ROUTE_POS1, pos1,
                      jnp.where(lane == ROUTE_POS2, pos2,
                                jnp.where(lane == ROUTE_W1, w1, jnp.where(lane == ROUTE_W2, w2, 0.0))))
    route_t = route.T
    p1 = route_t[ROUTE_POS1:ROUTE_POS1 + 1, :].astype(I32)
    p2 = route_t[ROUTE_POS2:ROUTE_POS2 + 1, :].astype(I32)
    r_idx = lax.broadcasted_iota(I32, (TILE_ROWS, rows), 0)
    onehot = jnp.where((r_idx == p1) | (r_idx == p2), 1.0, 0.0).astype(BF16)
    return route, _dot(onehot, h2).astype(BF16), nblk


N_MIXER_IN = 17
N_MIXER_OUT = 6
N_MIXER_SCRATCH = 5


def _mixer_kernel(past, sb, ls, nj, n_valid, n_steps, cast_experts, *refs):
    t = pl.program_id(0)
    if n_valid == n_steps:
        assert not cast_experts
        _mixer_tile(past, sb, ls, t % nj, nj, *refs)
        return
    outs = refs[-(N_MIXER_OUT + N_MIXER_SCRATCH):][:N_MIXER_OUT]

    @pl.when(t < n_valid)
    def _():
        _mixer_tile(past, sb, ls, t % nj, nj, *refs)
        if cast_experts:
            wg_ref, wu_ref, wd_ref = refs[N_MIXER_IN:N_MIXER_IN + 3]
            wgu_out, wd_out = refs[-(N_MIXER_OUT + N_MIXER_SCRATCH + 2):-(N_MIXER_OUT + N_MIXER_SCRATCH)]
            wgu_out[:, 0:D_EXPERT] = wg_ref[...].astype(BF16)
            wgu_out[:, D_EXPERT:2 * D_EXPERT] = wu_ref[...].astype(BF16)
            wd_out[...] = wd_ref[...].astype(BF16)

    @pl.when(t >= n_valid)
    def _():
        for ref in outs[:4]:
            ref[...] = jnp.zeros(ref.shape, ref.dtype)


def _mixer_tile(past, sb, ls, j, nj, *refs):
    (x_ref, hp_ref, hc_ref, gmix_ref, win_ref, wgrp_ref, pscale_ref, wpo_ref, wdw_ref, bdw_ref,
     lng_ref, lnb_ref, wco_ref, wout_ref, gffn_ref, wrc_ref, br_ref) = refs[:N_MIXER_IN]
    x1_ref, route_ref, xs_ref, nblk_ref, npool_ref, nconv_ref = refs[-(N_MIXER_OUT + N_MIXER_SCRATCH):][:N_MIXER_OUT]
    ubuf, vbuf, sbuf, cbuf, ybuf = refs[-N_MIXER_SCRATCH:]
    rows = sb * ls

    @pl.when(j == 0)
    def _():
        ubuf[:, POOL_PAD - POOL_HIST:POOL_PAD, :] = hp_ref[...]
        vbuf[:, CONV_PAD - CONV_HIST:CONV_PAD, :] = hc_ref[...]

    x = x_ref[...].reshape(rows, D_MODEL)
    h = _rmsnorm(x, gmix_ref[...]).astype(BF16)
    u = _dot(h, win_ref[:, 0:D_POOL])
    ubuf[:, POOL_PAD:POOL_PAD + ls, :] = u.reshape(sb, ls, D_POOL)
    ga = _dot(h, win_ref[:, D_POOL:D_POOL + D_CONV])
    gb = _dot(h, win_ref[:, D_POOL + D_CONV:D_POOL + 2 * D_CONV])
    vbuf[:, CONV_PAD:CONV_PAD + ls, :] = (ga * jax.nn.sigmoid(gb)).reshape(sb, ls, D_CONV)
    c_gp = D_POOL + 2 * D_CONV
    gate_pool = _dot(h, win_ref[:, c_gp:c_gp + D_MODEL])
    gate_conv = _dot(h, win_ref[:, c_gp + D_MODEL:c_gp + 2 * D_MODEL])

    t_idx = lax.broadcasted_iota(I32, (sb, ls, 1), 1)
    frames = (past + 1 + j * ls + t_idx).astype(F32)
    ys = []
    part, width = ubuf[...], 1
    for gi, w in enumerate(POOL_WINDOWS):
        c0, c1 = gi * POOL_GROUP, (gi + 1) * POOL_GROUP
        cur = ubuf[:, POOL_PAD:POOL_PAD + ls, c0:c1]
        while width < w:
            part = part + pltpu.roll(part, width, 1)
            width *= 2
        s = part[:, POOL_PAD:POOL_PAD + ls, 0:POOL_GROUP]
        part = part[:, :, POOL_GROUP:]
        mean = s / jnp.minimum(jnp.float32(w), frames)
        ys.append(_dot((mean - cur).reshape(rows, POOL_GROUP).astype(BF16), wgrp_ref[gi]))
    yp = (jnp.concatenate(ys, axis=1) * pscale_ref[...]).astype(BF16)
    y_pool = _dot(yp, wpo_ref[...])
    new_pool = ubuf[:, POOL_PAD + ls - POOL_HIST:POOL_PAD + ls, :]
    ubuf[:, POOL_PAD - POOL_HIST:POOL_PAD, :] = new_pool

    first = CONV_PAD - CONV_HIST
    taps = [[o for o in range(first, first + CONV_WIDTH) if o % SUBLANES == r] for r in range(SUBLANES)]
    for r in range(1, SUBLANES):
        span = taps[r][-1] - taps[r][0] + ls
        sbuf[r - 1, :, 0:span, :] = vbuf[:, taps[r][0]:taps[r][0] + span, :]
    nb_c, nt_c = (1, CONV_CHUNK) if ls >= CONV_CHUNK else (CONV_CHUNK // ls, ls)
    n_grp = nt_c // SUBLANES
    for i in range(rows // CONV_CHUNK):
        b0, t0 = (0, i * nt_c) if nb_c == 1 else (i * nb_c, 0)
        for c0 in range(0, D_CONV, LANES):
            cs = slice(c0, c0 + LANES)
            accs = [jnp.zeros((nb_c, SUBLANES, LANES), F32) + bdw_ref[:, cs] for _ in range(n_grp)]
            for r in range(SUBLANES):
                ws = [wdw_ref[o - first:o - first + 1, cs] for o in taps[r]]
                for q in range(n_grp + len(taps[r]) - 1):
                    lo = t0 + q * SUBLANES
                    if r == 0:
                        val = vbuf[b0:b0 + nb_c, taps[0][0] + lo:taps[0][0] + lo + SUBLANES, cs]
                    else:
                        val = sbuf[r - 1, b0:b0 + nb_c, lo:lo + SUBLANES, cs]
                    for a in range(len(taps[r])):
                        if 0 <= q - a < n_grp:
                            accs[q - a] = accs[q - a] + val * ws[a]
            cbuf[:, cs] = jnp.concatenate(accs, axis=1).reshape(CONV_CHUNK, LANES)
        yf = cbuf[...]
        mu = jnp.mean(yf, axis=-1, keepdims=True)
        yc = yf - mu
        var = jnp.mean(yc * yc, axis=-1, keepdims=True)
        yn = yc * lax.rsqrt(var + LN_EPS) * lng_ref[...] + lnb_ref[...]
        ybuf[i * CONV_CHUNK:(i + 1) * CONV_CHUNK, :] = (yn * jax.nn.sigmoid(yn)).astype(BF16)
    y_conv = _dot(ybuf[...], wco_ref[...])
    new_conv = vbuf[:, CONV_PAD + ls - CONV_HIST:CONV_PAD + ls, :]
    vbuf[:, CONV_PAD - CONV_HIST:CONV_PAD, :] = new_conv

    @pl.when(j == nj - 1)
    def _():
        npool_ref[...] = new_pool
        nconv_ref[...] = new_conv

    merged = jax.nn.sigmoid(gate_pool) * y_pool + jax.nn.sigmoid(gate_conv) * y_conv
    x1 = x + _dot(merged.astype(BF16), wout_ref[...])
    x1_ref[...] = x1

    h2 = _rmsnorm(x1, gffn_ref[...])
    h2_hi = h2.astype(BF16)
    h2_lo = (h2 - h2_hi.astype(F32)).astype(BF16)
    hi = _dot(h2_hi, wrc_ref[...])
    logits = (hi[:, 0:ROUTER_COLS] + (hi[:, ROUTER_COLS:] + _dot(h2_lo, wrc_ref[:, 0:ROUTER_COLS]))) + br_ref[...]
    route, xs, nblk = _dispatch(h2_hi, *_routing(logits))
    route_ref[...] = route
    xs_ref[...] = xs
    nblk_ref[...] = jnp.broadcast_to(nblk, (SUBLANES, LANES)).astype(I32)


def _const_spec(shape):
    nd = len(shape)
    return pl.BlockSpec(shape, lambda t: (0,) * nd, pipeline_mode=pl.Buffered(1))


def _mixer(x, hist_pool, hist_conv, past, sb, ls, wts, n_tiles, tile0, shared, expert_w=None):
    nb, seq, _ = x.shape
    assert nb % sb == 0 and seq % ls == 0 and sb * ls == TOK_TILE and ls % SUBLANES == 0 and ls >= CONV_HIST
    nj = seq // ls
    n_valid = (nb // sb) * nj
    n_steps = n_valid if shared is not None else n_tiles - tile0
    own = lambda t: jnp.minimum(t, n_valid - 1)
    hist_spec = lambda n, width: pl.BlockSpec((sb, n, width), lambda t: (own(t) // nj, 0, 0))
    in_specs = [pl.BlockSpec((sb, ls, D_MODEL), lambda t: (own(t) // nj, own(t) % nj, 0)),
                hist_spec(POOL_HIST, D_POOL), hist_spec(CONV_HIST, D_CONV)]
    in_specs += [_const_spec(w.shape) for w in wts]
    shared_shapes = (
        jax.ShapeDtypeStruct((n_tiles * TOK_TILE, D_MODEL), F32),
        jax.ShapeDtypeStruct((n_tiles * TOK_TILE, ROUTER_COLS), F32),
        jax.ShapeDtypeStruct((n_tiles, TILE_ROWS, D_MODEL), BF16),
        jax.ShapeDtypeStruct((n_tiles, SUBLANES, LANES), I32),
    )
    out_shape = shared_shapes + (jax.ShapeDtypeStruct((nb, POOL_HIST, D_POOL), F32),
                                 jax.ShapeDtypeStruct((nb, CONV_HIST, D_CONV), F32))
    out_specs = (pl.BlockSpec((TOK_TILE, D_MODEL), lambda t: (tile0 + t, 0)),
                 pl.BlockSpec((TOK_TILE, ROUTER_COLS), lambda t: (tile0 + t, 0)),
                 pl.BlockSpec((None, TILE_ROWS, D_MODEL), lambda t: (tile0 + t, 0, 0)),
                 pl.BlockSpec((None, SUBLANES, LANES), lambda t: (tile0 + t, 0, 0)),
                 hist_spec(POOL_HIST, D_POOL), hist_spec(CONV_HIST, D_CONV))
    args = [x, hist_pool, hist_conv, *wts]
    if expert_w is not None:
        assert shared is None
        for w in expert_w:
            assert w.shape[0] % (n_valid * BLK) == 0
            in_specs.append(pl.BlockSpec((w.shape[0] // n_valid, w.shape[1]), lambda t: (own(t), 0)))
        wg, wu, wd = expert_w
        cast_shapes = (jax.ShapeDtypeStruct((wg.shape[0], wg.shape[1] + wu.shape[1]), BF16),
                       jax.ShapeDtypeStruct(wd.shape, BF16))
        out_shape = cast_shapes + out_shape
        out_specs = tuple(pl.BlockSpec((s.shape[0] // n_valid, s.shape[1]), lambda t: (own(t), 0))
                          for s in cast_shapes) + out_specs
        args += list(expert_w)
    aliases = {}
    if shared is not None:
        in_specs += [pl.BlockSpec(memory_space=pl.ANY)] * len(shared)
        aliases = {len(args) + k: k for k in range(len(shared))}
        args += list(shared)
    rows = sb * ls
    outs = pl.pallas_call(
        functools.partial(_mixer_kernel, past, sb, ls, nj, n_valid, n_steps, expert_w is not None),
        grid=(n_steps,),
        in_specs=in_specs,
        out_specs=out_specs,
        out_shape=out_shape,
        input_output_aliases=aliases,
        scratch_shapes=[pltpu.VMEM((sb, POOL_PAD + ls, D_POOL), F32),
                        pltpu.VMEM((sb, CONV_PAD + ls, D_CONV), F32),
                        pltpu.VMEM((SUBLANES - 1, sb, CONV_PAD + ls, D_CONV), F32),
                        pltpu.VMEM((CONV_CHUNK, D_CONV), F32),
                        pltpu.VMEM((rows, D_CONV), BF16)],
        compiler_params=pltpu.CompilerParams(dimension_semantics=("arbitrary",),
                                             vmem_limit_bytes=VMEM_LIMIT),
        name="mixer",
    )(*args)
    return outs if expert_w is None else (*outs[2:], *outs[:2])


def _chunk_tables(nblk, n_chunks_max):
    n_tiles = nblk.shape[0]
    e_ids = jnp.arange(N_EXPERTS, dtype=I32)
    per_e = jnp.sum(nblk, axis=0)
    chunks_e = -(-per_e // CHUNK_BLOCKS)
    cend = jnp.cumsum(chunks_e)
    cstart = cend - chunks_e
    n_chunks = cend[-1]
    c = jnp.arange(n_chunks_max, dtype=I32)
    chunk_e = jnp.minimum(jnp.sum((cend[None, :] <= c[:, None]).astype(I32), axis=1), N_EXPERTS - 1)
    is_e = chunk_e[:, None] == e_ids[None, :]
    of_chunk = lambda v: jnp.sum(jnp.where(is_e, v[None, :], 0), axis=1)
    of_chunk_t = lambda a: jnp.sum(jnp.where(is_e[:, None, :], a[None, :, :], 0), axis=2)
    cum_t = jnp.cumsum(nblk, axis=0)
    bstart = jnp.cumsum(nblk, axis=1) - nblk
    k = ((c - of_chunk(cstart)) * CHUNK_BLOCKS)[:, None] + jnp.arange(CHUNK_BLOCKS, dtype=I32)[None, :]
    tile = jnp.sum((of_chunk_t(cum_t)[:, None, :] <= k[:, :, None]).astype(I32), axis=2)
    is_t = tile[:, :, None] == jnp.arange(n_tiles, dtype=I32)[None, None, :]
    shift = of_chunk_t(bstart - (cum_t - nblk))
    blk = k + jnp.sum(jnp.where(is_t, shift[:, None, :], 0), axis=2)
    valid = (k < of_chunk(per_e)[:, None]) & (c[:, None] < n_chunks)
    table = jnp.where(valid, tile * TILE_BLOCKS + blk, -1)
    return table.reshape(-1).astype(I32), chunk_e.astype(I32), n_chunks.reshape(1).astype(I32)


def _expert_kernel(src_ref, dst_ref, ce_ref, nch_ref, x_hbm, wgu_ref, wd_ref, y_hbm,
                   xin, yout, sem_in, sem_out):
    c = pl.program_id(0)
    nc = pl.num_programs(0)
    nch = nch_ref[0]

    def src_block(chunk, s):
        return src_ref[chunk * CHUNK_BLOCKS + s]

    def dst_block(chunk, s):
        return dst_ref[(chunk + RING_SLOTS) * CHUNK_BLOCKS + s]

    def in_copy(chunk, s):
        slot = chunk % RING_SLOTS
        return pltpu.make_async_copy(x_hbm.at[src_block(chunk, s)], xin.at[slot, pl.ds(s * BLK, BLK), :],
                                     sem_in.at[slot])

    def out_copy(chunk, s):
        slot = chunk % RING_SLOTS
        return pltpu.make_async_copy(yout.at[slot, pl.ds(s * BLK, BLK), :], y_hbm.at[dst_block(chunk, s)],
                                     sem_out.at[slot])

    @pl.when(c < nch)
    def _():
        @pl.when(c == 0)
        def _():
            yout[...] = jnp.zeros(yout.shape, yout.dtype)
            for before in range(RING_SLOTS, 1, -1):
                for s in range(CHUNK_BLOCKS):
                    out_copy(c - before, s).start()
            for first in range(RING_SLOTS - 1):
                for s in range(CHUNK_BLOCKS):
                    in_copy(c + first, s).start()

        for s in range(CHUNK_BLOCKS):
            in_copy(c + RING_SLOTS - 1, s).start()
        for s in range(CHUNK_BLOCKS):
            in_copy(c, s).wait()

        ab = _dot(xin[c % RING_SLOTS], wgu_ref[...])

        for s in range(CHUNK_BLOCKS):
            out_copy(c - 1, s).start()

        a = ab[:, 0:D_EXPERT]
        b = ab[:, D_EXPERT:2 * D_EXPERT]
        act = ((a * jax.nn.sigmoid(a)) * b).astype(BF16)

        for s in range(CHUNK_BLOCKS):
            out_copy(c - RING_SLOTS, s).wait()
        yout[c % RING_SLOTS] = _dot(act, wd_ref[...]).astype(BF16)

        @pl.when(c == nch - 1)
        def _():
            for s in range(CHUNK_BLOCKS):
                out_copy(c, s).start()
            for ahead in range(1, RING_SLOTS):
                for s in range(CHUNK_BLOCKS):
                    in_copy(c + ahead, s).wait()
            for before in range(RING_SLOTS - 1, -1, -1):
                for s in range(CHUNK_BLOCKS):
                    out_copy(c - before, s).wait()


def _experts(xs_blocks, n_used_blocks, table, chunk_e, n_chunks, w_gate_up, w_down):
    assert xs_blocks.shape[0] - n_used_blocks > RING_SLOTS * CHUNK_BLOCKS
    n_steps = chunk_e.shape[0]
    slot_ids = jnp.arange(CHUNK_BLOCKS, dtype=I32)[None, :]
    spare = lambda chunks: n_used_blocks + (chunks[:, None] % RING_SLOTS) * CHUNK_BLOCKS + slot_ids
    zero_block = n_used_blocks + RING_SLOTS * CHUNK_BLOCKS
    tab = table.reshape(n_steps, CHUNK_BLOCKS)
    chunks = jnp.arange(n_steps, dtype=I32)
    src_tab = jnp.where((tab < 0) | (chunks[:, None] >= n_chunks[0]), zero_block, tab)
    src_tab = jnp.concatenate([src_tab, jnp.full((RING_SLOTS - 1, CHUNK_BLOCKS), zero_block, I32)], axis=0)
    dst_tab = jnp.concatenate([spare(jnp.arange(-RING_SLOTS, 0, dtype=I32)), jnp.where(tab < 0, spare(chunks), tab)], axis=0)
    w_spec = lambda shape: pl.BlockSpec((None,) + shape, lambda c, src, dst, ce, nch: (ce[c], 0, 0))
    return pl.pallas_call(
        _expert_kernel,
        grid_spec=pltpu.PrefetchScalarGridSpec(
            num_scalar_prefetch=4,
            grid=(n_steps,),
            in_specs=[pl.BlockSpec(memory_space=pl.ANY),
                      w_spec((D_MODEL, 2 * D_EXPERT)), w_spec((D_EXPERT, D_MODEL))],
            out_specs=pl.BlockSpec(memory_space=pl.ANY),
            scratch_shapes=[pltpu.VMEM((RING_SLOTS, CHUNK_ROWS, D_MODEL), BF16),
                            pltpu.VMEM((RING_SLOTS, CHUNK_ROWS, D_MODEL), BF16),
                            pltpu.SemaphoreType.DMA((RING_SLOTS,)),
                            pltpu.SemaphoreType.DMA((RING_SLOTS,))]),
        out_shape=jax.ShapeDtypeStruct(xs_blocks.shape, xs_blocks.dtype),
        input_output_aliases={4: 0},
        compiler_params=pltpu.CompilerParams(dimension_semantics=("arbitrary",),
                                             vmem_limit_bytes=VMEM_LIMIT),
        name="experts",
    )(src_tab.reshape(-1), dst_tab.reshape(-1).astype(I32), chunk_e, n_chunks, xs_blocks, w_gate_up, w_down)


def _combine_kernel(tiles, y_ref, route_ref, x1_ref, gfin_ref, o_ref):
    for k in range(tiles):
        rs = slice(k * TOK_TILE, (k + 1) * TOK_TILE)
        route = route_ref[rs, :]
        pos1 = route[:, ROUTE_POS1:ROUTE_POS1 + 1].astype(I32)
        pos2 = route[:, ROUTE_POS2:ROUTE_POS2 + 1].astype(I32)
        w1 = route[:, ROUTE_W1:ROUTE_W1 + 1]
        w2 = route[:, ROUTE_W2:ROUTE_W2 + 1]
        r_idx = lax.broadcasted_iota(I32, (TOK_TILE, TILE_ROWS), 1)
        gather = jnp.where(r_idx == pos1, w1, jnp.where(r_idx == pos2, w2, 0.0)).astype(BF16)
        y = y_ref[k * TILE_BLOCKS:(k + 1) * TILE_BLOCKS].reshape(TILE_ROWS, D_MODEL)
        o_ref[rs, :] = _rmsnorm(x1_ref[rs, :] + _dot(gather, y), gfin_ref[...])


def _combine(y_blocks, route, x1, g_final, tile0, n_tiles, tiles_per_step):
    assert n_tiles % tiles_per_step == 0 and tile0 % tiles_per_step == 0
    step0 = tile0 // tiles_per_step
    tok_spec = lambda width: pl.BlockSpec((tiles_per_step * TOK_TILE, width), lambda i: (step0 + i, 0))
    return pl.pallas_call(
        functools.partial(_combine_kernel, tiles_per_step),
        grid=(n_tiles // tiles_per_step,),
        in_specs=[pl.BlockSpec((tiles_per_step * TILE_BLOCKS, BLK, D_MODEL), lambda i: (step0 + i, 0, 0)),
                  tok_spec(ROUTER_COLS), tok_spec(D_MODEL),
                  pl.BlockSpec((1, D_MODEL), lambda i: (0, 0))],
        out_specs=pl.BlockSpec((tiles_per_step * TOK_TILE, D_MODEL), lambda i: (i, 0)),
        out_shape=jax.ShapeDtypeStruct((n_tiles * TOK_TILE, D_MODEL), F32),
        compiler_params=pltpu.CompilerParams(dimension_semantics=("arbitrary",),
                                             vmem_limit_bytes=VMEM_LIMIT),
        name="combine",
    )(y_blocks, route, x1, g_final)


def kernel(x_prompt, x_sample, state_pool, state_conv, g_mix, w_in, w_pool_grp, pool_scale, w_pool_out, w_dw, b_dw, ln_g, ln_b, w_conv_out, w_out, g_ffn, w_rg, b_rg, w_re, b_re, w_gate, w_up, w_down, g_final):
    assert g_mix.shape[0] == 1
    l = 0
    bp, seq_p, _ = x_prompt.shape
    bs, seq_s, _ = x_sample.shape
    tiles_p = bp * seq_p // TOK_TILE
    tiles_s = bs * seq_s // TOK_TILE
    n_tiles = tiles_p + tiles_s

    n_pad = ROUTER_COLS - N_EXPERTS - N_EXPERT_GROUPS
    w_r = jnp.concatenate([w_re[l], w_rg[l], jnp.zeros((D_MODEL, n_pad), F32)], axis=1)
    b_r = jnp.concatenate([b_re[l], b_rg[l], jnp.zeros((n_pad,), F32)]).reshape(1, ROUTER_COLS)
    w_r_hi = w_r.astype(BF16)
    w_r_cat = jnp.concatenate([w_r_hi, (w_r - w_r_hi.astype(F32)).astype(BF16)], axis=1)
    row = lambda a: a.reshape(1, -1)
    wts = (row(g_mix[l]), w_in[l].astype(BF16), w_pool_grp[l].astype(BF16), row(pool_scale[l]),
           w_pool_out[l].astype(BF16), w_dw[l], row(b_dw[l]), row(ln_g[l]), row(ln_b[l]),
           w_conv_out[l].astype(BF16), w_out[l].astype(BF16), row(g_ffn[l]), w_r_cat, b_r)

    zp = jnp.zeros((bp, POOL_HIST, D_POOL), F32)
    zc = jnp.zeros((bp, CONV_HIST, D_CONV), F32)
    n_all = n_tiles + -(-(RING_SLOTS * CHUNK_BLOCKS + 1) // TILE_BLOCKS)
    expert_w = (w_gate[l].reshape(N_EXPERTS * D_MODEL, D_EXPERT), w_up[l].reshape(N_EXPERTS * D_MODEL, D_EXPERT),
                w_down[l].reshape(N_EXPERTS * D_EXPERT, D_MODEL))
    *shared, npp, ncp, w_gate_up, w_down_bf = _mixer(x_prompt, zp, zc, 0, 1, TOK_TILE, wts, n_all, 0, None, expert_w)
    x1, route, xs, nblk, nps, ncs = _mixer(x_sample, state_pool[l], state_conv[l], PAST_LEN, bs, seq_s, wts,
                                           n_all, tiles_p, shared)

    n_chunks_max = -(-n_tiles * TILE_BLOCKS // CHUNK_BLOCKS) + N_EXPERTS
    table, chunk_e, n_chunks = _chunk_tables(nblk[:n_tiles, 0, :N_EXPERTS], n_chunks_max)
    y_blocks = _experts(xs.reshape(n_all * TILE_BLOCKS, BLK, D_MODEL), n_tiles * TILE_BLOCKS,
                        table, chunk_e, n_chunks,
                        w_gate_up.reshape(N_EXPERTS, D_MODEL, 2 * D_EXPERT),
                        w_down_bf.reshape(N_EXPERTS, D_EXPERT, D_MODEL))
    tps = COMBINE_TILES if tiles_p % COMBINE_TILES == 0 else 1
    yp = _combine(y_blocks, route, x1, row(g_final), 0, tiles_p, tps)
    ys = _combine(y_blocks, route, x1, row(g_final), tiles_p, tiles_s, 1)
    return (yp.reshape(bp, seq_p, D_MODEL), ys.reshape(bs, seq_s, D_MODEL),
            npp[None], ncp[None], nps[None], ncs[None])
```

```python
import functools

import jax
import jax.numpy as jnp
from jax import lax
from jax.experimental import pallas as pl
from jax.experimental.pallas import tpu as pltpu

D_MODEL = 1024
D_POOL = 512
N_POOL_GROUPS = 4
POOL_GROUP = D_POOL // N_POOL_GROUPS
POOL_WINDOWS = (2, 4, 8, 16)
POOL_HIST = max(POOL_WINDOWS) - 1
D_CONV = 512
CONV_WIDTH = 31
CONV_HIST = CONV_WIDTH - 1
N_EXPERT_GROUPS = 4
EXPERTS_PER_GROUP = 8
N_EXPERTS = N_EXPERT_GROUPS * EXPERTS_PER_GROUP
TOP_K = 2
D_EXPERT = 256
RMS_EPS = 1e-6
LN_EPS = 1e-5
PAST_LEN = 1024

LANES = 128
SUBLANES = 8
POOL_PAD = 16
CONV_PAD = 32
CONV_CHUNK = 64
ROUTER_COLS = LANES
VMEM_LIMIT = 56 * 1024 * 1024

TOK_TILE = 512
BLK = 2 * SUBLANES
TILE_BLOCKS = (TOP_K * TOK_TILE + N_EXPERTS * (BLK - 1)) // BLK
TILE_ROWS = TILE_BLOCKS * BLK
CHUNK_BLOCKS = 32
CHUNK_ROWS = CHUNK_BLOCKS * BLK
RING_SLOTS = 3
ROUTE_POS1, ROUTE_POS2, ROUTE_W1, ROUTE_W2 = 0, 1, 2, 3
COMBINE_TILES = 2

BF16 = jnp.bfloat16
F32 = jnp.float32
I32 = jnp.int32


def _dot(a, b):
    return jnp.dot(a, b, preferred_element_type=F32)


def _rmsnorm(x, g):
    return x * lax.rsqrt(jnp.mean(x * x, axis=-1, keepdims=True) + RMS_EPS) * g


def _routing(logits):
    lane = lax.broadcasted_iota(I32, logits.shape, 1)
    lane_f = lane.astype(F32)
    neg = jnp.float32(-jnp.inf)
    big = jnp.float32(1e9)
    is_grp = (lane >= N_EXPERTS) & (lane < N_EXPERTS + N_EXPERT_GROUPS)
    glog = jnp.where(is_grp, logits, neg)
    gmax = jnp.max(glog, axis=1, keepdims=True)
    gidx = jnp.min(jnp.where(glog == gmax, lane_f, big), axis=1, keepdims=True) - float(N_EXPERTS)
    gsum = jnp.sum(jnp.where(is_grp, jnp.exp(glog - gmax), 0.0), axis=1, keepdims=True)
    p_sel = 1.0 / gsum
    lane_grp = (lane >> 3).astype(F32)
    is_sel = (lane < N_EXPERTS) & (lane_grp == gidx)
    elog = jnp.where(is_sel, logits, neg)
    v1 = jnp.max(elog, axis=1, keepdims=True)
    i1 = jnp.min(jnp.where(elog == v1, lane_f, big), axis=1, keepdims=True)
    elog2 = jnp.where(lane_f == i1, neg, elog)
    v2 = jnp.max(elog2, axis=1, keepdims=True)
    i2 = jnp.min(jnp.where(elog2 == v2, lane_f, big), axis=1, keepdims=True)
    t = jnp.exp(v2 - v1)
    return i1, i2, p_sel / (1.0 + t), p_sel * (t / (1.0 + t))


def _dispatch(h2, i1, i2, w1, w2):
    rows = h2.shape[0]
    lane = lax.broadcasted_iota(I32, (rows, LANES), 1)
    lane_f = lane.astype(F32)
    hit1 = lane_f == i1
    hit2 = lane_f == i2
    sel = jnp.where(hit1 | hit2, 1.0, 0.0)
    earlier = lax.broadcasted_iota(I32, (rows, rows), 1) < lax.broadcasted_iota(I32, (rows, rows), 0)
    rank = _dot(jnp.where(earlier, 1.0, 0.0).astype(BF16), sel.astype(BF16))
    cnt = jnp.sum(sel, axis=0, keepdims=True)
    nblk = jnp.floor((cnt + float(BLK - 1)) * (1.0 / BLK))
    before = lax.broadcasted_iota(I32, (LANES, LANES), 0) < lax.broadcasted_iota(I32, (LANES, LANES), 1)
    bstart = _dot(jnp.broadcast_to(nblk, (SUBLANES, LANES)).astype(BF16),
                  jnp.where(before, 1.0, 0.0).astype(BF16))[0:1, :]
    pos = float(BLK) * bstart + rank
    pos1 = jnp.sum(jnp.where(hit1, pos, 0.0), axis=1, keepdims=True)
    pos2 = jnp.sum(jnp.where(hit2, pos, 0.0), axis=1, keepdims=True)
    route = jnp.where(lane == ROUTE_POS1, pos1,
                      jnp.where(lane == ROUTE_POS2, pos2,
                                jnp.where(lane == ROUTE_W1, w1, jnp.where(lane == ROUTE_W2, w2, 0.0))))
    route_t = route.T
    p1 = route_t[ROUTE_POS1:ROUTE_POS1 + 1, :].astype(I32)
    p2 = route_t[ROUTE_POS2:ROUTE_POS2 + 1, :].astype(I32)
    r_idx = lax.broadcasted_iota(I32, (TILE_ROWS, rows), 0)
    onehot = jnp.where((r_idx == p1) | (r_idx == p2), 1.0, 0.0).astype(BF16)
    return route, _dot(onehot, h2).astype(BF16), nblk


N_MIXER_IN = 17
N_MIXER_OUT = 6
N_MIXER_SCRATCH = 5


def _mixer_kernel(past, sb, ls, nj, n_valid, n_steps, cast_experts, *refs):
    t = pl.program_id(0)
    if n_valid == n_steps:
        assert not cast_experts
        _mixer_tile(past, sb, ls, t % nj, nj, *refs)
        return
    outs = refs[-(N_MIXER_OUT + N_MIXER_SCRATCH):][:N_MIXER_OUT]

    @pl.when(t < n_valid)
    def _():
        if cast_experts:
            wg_ref, wu_ref, wd_ref = refs[N_MIXER_IN:N_MIXER_IN + 3]
            wgu_out, wd_out = refs[-(N_MIXER_OUT + N_MIXER_SCRATCH + 2):-(N_MIXER_OUT + N_MIXER_SCRATCH)]
            wgu_out[:, 0:D_EXPERT] = wg_ref[...].astype(BF16)
            wgu_out[:, D_EXPERT:2 * D_EXPERT] = wu_ref[...].astype(BF16)
            wd_out[...] = wd_ref[...].astype(BF16)
        _mixer_tile(past, sb, ls, t % nj, nj, *refs)

    @pl.when(t >= n_valid)
    def _():
        for ref in outs[:4]:
            ref[...] = jnp.zeros(ref.shape, ref.dtype)


def _mixer_tile(past, sb, ls, j, nj, *refs):
    (x_ref, hp_ref, hc_ref, gmix_ref, win_ref, wgrp_ref, pscale_ref, wpo_ref, wdw_ref, bdw_ref,
     lng_ref, lnb_ref, wco_ref, wout_ref, gffn_ref, wrc_ref, br_ref) = refs[:N_MIXER_IN]
    x1_ref, route_ref, xs_ref, nblk_ref, npool_ref, nconv_ref = refs[-(N_MIXER_OUT + N_MIXER_SCRATCH):][:N_MIXER_OUT]
    ubuf, vbuf, sbuf, cbuf, ybuf = refs[-N_MIXER_SCRATCH:]
    rows = sb * ls

    @pl.when(j == 0)
    def _():
        ubuf[:, POOL_PAD - POOL_HIST:POOL_PAD, :] = hp_ref[...]
        vbuf[:, CONV_PAD - CONV_HIST:CONV_PAD, :] = hc_ref[...]

    x = x_ref[...].reshape(rows, D_MODEL)
    h = _rmsnorm(x, gmix_ref[...]).astype(BF16)
    u = _dot(h, win_ref[:, 0:D_POOL])
    ubuf[:, POOL_PAD:POOL_PAD + ls, :] = u.reshape(sb, ls, D_POOL)
    ga = _dot(h, win_ref[:, D_POOL:D_POOL + D_CONV])
    gb = _dot(h, win_ref[:, D_POOL + D_CONV:D_POOL + 2 * D_CONV])
    vbuf[:, CONV_PAD:CONV_PAD + ls, :] = (ga * jax.nn.sigmoid(gb)).reshape(sb, ls, D_CONV)
    c_gp = D_POOL + 2 * D_CONV
    gate_pool = _dot(h, win_ref[:, c_gp:c_gp + D_MODEL])
    gate_conv = _dot(h, win_ref[:, c_gp + D_MODEL:c_gp + 2 * D_MODEL])

    t_idx = lax.broadcasted_iota(I32, (sb, ls, 1), 1)
    frames = (past + 1 + j * ls + t_idx).astype(F32)
    ys = []
    part, width = ubuf[...], 1
    for gi, w in enumerate(POOL_WINDOWS):
        c0, c1 = gi * POOL_GROUP, (gi + 1) * POOL_GROUP
        cur = ubuf[:, POOL_PAD:POOL_PAD + ls, c0:c1]
        while width < w:
            part = part + pltpu.roll(part, width, 1)
            width *= 2
        s = part[:, POOL_PAD:POOL_PAD + ls, 0:POOL_GROUP]
        part = part[:, :, POOL_GROUP:]
        mean = s / jnp.minimum(jnp.float32(w), frames)
        ys.append(_dot((mean - cur).reshape(rows, POOL_GROUP).astype(BF16), wgrp_ref[gi]))
    yp = (jnp.concatenate(ys, axis=1) * pscale_ref[...]).astype(BF16)
    y_pool = _dot(yp, wpo_ref[...])
    new_pool = ubuf[:, POOL_PAD + ls - POOL_HIST:POOL_PAD + ls, :]
    ubuf[:, POOL_PAD - POOL_HIST:POOL_PAD, :] = new_pool

    first = CONV_PAD - CONV_HIST
    taps = [[o for o in range(first, first + CONV_WIDTH) if o % SUBLANES == r] for r in range(SUBLANES)]
    for r in range(1, SUBLANES):
        span = taps[r][-1] - taps[r][0] + ls
        sbuf[r - 1, :, 0:span, :] = vbuf[:, taps[r][0]:taps[r][0] + span, :]
    nb_c, nt_c = (1, CONV_CHUNK) if ls >= CONV_CHUNK else (CONV_CHUNK // ls, ls)
    n_grp = nt_c // SUBLANES
    for i in range(rows // CONV_CHUNK):
        b0, t0 = (0, i * nt_c) if nb_c == 1 else (i * nb_c, 0)
        for c0 in range(0, D_CONV, LANES):
            cs = slice(c0, c0 + LANES)
            accs = [jnp.zeros((nb_c, SUBLANES, LANES), F32) + bdw_ref[:, cs] for _ in range(n_grp)]
            for r in range(SUBLANES):
                ws = [wdw_ref[o - first:o - first + 1, cs] for o in taps[r]]
                for q in range(n_grp + len(taps[r]) - 1):
                    lo = t0 + q * SUBLANES
                    if r == 0:
                        val = vbuf[b0:b0 + nb_c, taps[0][0] + lo:taps[0][0] + lo + SUBLANES, cs]
                    else:
                        val = sbuf[r - 1, b0:b0 + nb_c, lo:lo + SUBLANES, cs]
                    for a in range(len(taps[r])):
                        if 0 <= q - a < n_grp:
                            accs[q - a] = accs[q - a] + val * ws[a]
            cbuf[:, cs] = jnp.concatenate(accs, axis=1).reshape(CONV_CHUNK, LANES)
        yf = cbuf[...]
        mu = jnp.mean(yf, axis=-1, keepdims=True)
        yc = yf - mu
        var = jnp.mean(yc * yc, axis=-1, keepdims=True)
        yn = yc * lax.rsqrt(var + LN_EPS) * lng_ref[...] + lnb_ref[...]
        ybuf[i * CONV_CHUNK:(i + 1) * CONV_CHUNK, :] = (yn * jax.nn.sigmoid(yn)).astype(BF16)
    y_conv = _dot(ybuf[...], wco_ref[...])
    new_conv = vbuf[:, CONV_PAD + ls - CONV_HIST:CONV_PAD + ls, :]
    vbuf[:, CONV_PAD - CONV_HIST:CONV_PAD, :] = new_conv

    @pl.when(j == nj - 1)
    def _():
        npool_ref[...] = new_pool
        nconv_ref[...] = new_conv

    merged = jax.nn.sigmoid(gate_pool) * y_pool + jax.nn.sigmoid(gate_conv) * y_conv
    x1 = x + _dot(merged.astype(BF16), wout_ref[...])
    x1_ref[...] = x1

    h2 = _rmsnorm(x1, gffn_ref[...])
    h2_hi = h2.astype(BF16)
    h2_lo = (h2 - h2_hi.astype(F32)).astype(BF16)
    hi = _dot(h2_hi, wrc_ref[...])
    logits = (hi[:, 0:ROUTER_COLS] + (hi[:, ROUTER_COLS:] + _dot(h2_lo, wrc_ref[:, 0:ROUTER_COLS]))) + br_ref[...]
    route, xs, nblk = _dispatch(h2_hi, *_routing(logits))
    route_ref[...] = route
    xs_ref[...] = xs
    nblk_ref[...] = jnp.broadcast_to(nblk, (SUBLANES, LANES)).astype(I32)


def _const_spec(shape):
    nd = len(shape)
    return pl.BlockSpec(shape, lambda t: (0,) * nd, pipeline_mode=pl.Buffered(1))


def _mixer(x, hist_pool, hist_conv, past, sb, ls, wts, n_tiles, tile0, shared, expert_w=None):
    nb, seq, _ = x.shape
    assert nb % sb == 0 and seq % ls == 0 and sb * ls == TOK_TILE and ls % SUBLANES == 0 and ls >= CONV_HIST
    nj = seq // ls
    n_valid = (nb // sb) * nj
    n_steps = n_valid if shared is not None else n_tiles - tile0
    own = lambda t: jnp.minimum(t, n_valid - 1)
    hist_spec = lambda n, width: pl.BlockSpec((sb, n, width), lambda t: (own(t) // nj, 0, 0))
    in_specs = [pl.BlockSpec((sb, ls, D_MODEL), lambda t: (own(t) // nj, own(t) % nj, 0)),
                hist_spec(POOL_HIST, D_POOL), hist_spec(CONV_HIST, D_CONV)]
    in_specs += [_const_spec(w.shape) for w in wts]
    shared_shapes = (
        jax.ShapeDtypeStruct((n_tiles * TOK_TILE, D_MODEL), F32),
        jax.ShapeDtypeStruct((n_tiles * TOK_TILE, ROUTER_COLS), F32),
        jax.ShapeDtypeStruct((n_tiles, TILE_ROWS, D_MODEL), BF16),
        jax.ShapeDtypeStruct((n_tiles, SUBLANES, LANES), I32),
    )
    out_shape = shared_shapes + (jax.ShapeDtypeStruct((nb, POOL_HIST, D_POOL), F32),
                                 jax.ShapeDtypeStruct((nb, CONV_HIST, D_CONV), F32))
    out_specs = (pl.BlockSpec((TOK_TILE, D_MODEL), lambda t: (tile0 + t, 0)),
                 pl.BlockSpec((TOK_TILE, ROUTER_COLS), lambda t: (tile0 + t, 0)),
                 pl.BlockSpec((None, TILE_ROWS, D_MODEL), lambda t: (tile0 + t, 0, 0)),
                 pl.BlockSpec((None, SUBLANES, LANES), lambda t: (tile0 + t, 0, 0)),
                 hist_spec(POOL_HIST, D_POOL), hist_spec(CONV_HIST, D_CONV))
    args = [x, hist_pool, hist_conv, *wts]
    if expert_w is not None:
        assert shared is None
        for w in expert_w:
            assert w.shape[0] % (n_valid * BLK) == 0
            in_specs.append(pl.BlockSpec((w.shape[0] // n_valid, w.shape[1]), lambda t: (own(t), 0)))
        wg, wu, wd = expert_w
        cast_shapes = (jax.ShapeDtypeStruct((wg.shape[0], wg.shape[1] + wu.shape[1]), BF16),
                       jax.ShapeDtypeStruct(wd.shape, BF16))
        out_shape = cast_shapes + out_shape
        out_specs = tuple(pl.BlockSpec((s.shape[0] // n_valid, s.shape[1]), lambda t: (own(t), 0))
                          for s in cast_shapes) + out_specs
        args += list(expert_w)
    aliases = {}
    if shared is not None:
        in_specs += [pl.BlockSpec(memory_space=pl.ANY)] * len(shared)
        aliases = {len(args) + k: k for k in range(len(shared))}
        args += list(shared)
    rows = sb * ls
    outs = pl.pallas_call(
        functools.partial(_mixer_kernel, past, sb, ls, nj, n_valid, n_steps, expert_w is not None),
        grid=(n_steps,),
        in_specs=in_specs,
        out_specs=out_specs,
        out_shape=out_shape,
        input_output_aliases=aliases,
        scratch_shapes=[pltpu.VMEM((sb, POOL_PAD + ls, D_POOL), F32),
                        pltpu.VMEM((sb, CONV_PAD + ls, D_CONV), F32),
                        pltpu.VMEM((SUBLANES - 1, sb, CONV_PAD + ls, D_CONV), F32),
                        pltpu.VMEM((CONV_CHUNK, D_CONV), F32),
                        pltpu.VMEM((rows, D_CONV), BF16)],
        compiler_params=pltpu.CompilerParams(dimension_semantics=("arbitrary",),
                                             vmem_limit_bytes=VMEM_LIMIT),
        name="mixer",
    )(*args)
    return outs if expert_w is None else (*outs[2:], *outs[:2])


def _chunk_tables(nblk, n_chunks_max):
    n_tiles = nblk.shape[0]
    e_ids = jnp.arange(N_EXPERTS, dtype=I32)
    per_e = jnp.sum(nblk, axis=0)
    chunks_e = -(-per_e // CHUNK_BLOCKS)
    cend = jnp.cumsum(chunks_e)
    cstart = cend - chunks_e
    n_chunks = cend[-1]
    c = jnp.arange(n_chunks_max, dtype=I32)
    chunk_e = jnp.minimum(jnp.sum((cend[None, :] <= c[:, None]).astype(I32), axis=1), N_EXPERTS - 1)
    is_e = chunk_e[:, None] == e_ids[None, :]
    of_chunk = lambda v: jnp.sum(jnp.where(is_e, v[None, :], 0), axis=1)
    of_chunk_t = lambda a: jnp.sum(jnp.where(is_e[:, None, :], a[None, :, :], 0), axis=2)
    cum_t = jnp.cumsum(nblk, axis=0)
    bstart = jnp.cumsum(nblk, axis=1) - nblk
    k = ((c - of_chunk(cstart)) * CHUNK_BLOCKS)[:, None] + jnp.arange(CHUNK_BLOCKS, dtype=I32)[None, :]
    tile = jnp.sum((of_chunk_t(cum_t)[:, None, :] <= k[:, :, None]).astype(I32), axis=2)
    is_t = tile[:, :, None] == jnp.arange(n_tiles, dtype=I32)[None, None, :]
    shift = of_chunk_t(bstart - (cum_t - nblk))
    blk = k + jnp.sum(jnp.where(is_t, shift[:, None, :], 0), axis=2)
    valid = (k < of_chunk(per_e)[:, None]) & (c[:, None] < n_chunks)
    table = jnp.where(valid, tile * TILE_BLOCKS + blk, -1)
    return table.reshape(-1).astype(I32), chunk_e.astype(I32), n_chunks.reshape(1).astype(I32)


def _expert_kernel(spare0, tab_ref, ce_ref, nch_ref, x_hbm, wgu_ref, wd_ref, y_hbm,
                   xin, yout, sem_in, sem_out):
    c = pl.program_id(0)
    nc = pl.num_programs(0)
    nch = nch_ref[0]

    zero_block = spare0 + RING_SLOTS * CHUNK_BLOCKS

    def src_block(chunk, s):
        t = tab_ref[jnp.minimum(chunk, nc - 1) * CHUNK_BLOCKS + s]
        return jnp.where((t < 0) | (chunk >= nch), zero_block, t)

    def dst_block(chunk, s):
        t = tab_ref[jnp.maximum(chunk, 0) * CHUNK_BLOCKS + s]
        return jnp.where((t < 0) | (chunk < 0), spare0 + (chunk % RING_SLOTS) * CHUNK_BLOCKS + s, t)

    def in_copy(chunk, s):
        slot = chunk % RING_SLOTS
        return pltpu.make_async_copy(x_hbm.at[src_block(chunk, s)], xin.at[slot, pl.ds(s * BLK, BLK), :],
                                     sem_in.at[slot])

    def out_copy(chunk, s):
        slot = chunk % RING_SLOTS
        return pltpu.make_async_copy(yout.at[slot, pl.ds(s * BLK, BLK), :], y_hbm.at[dst_block(chunk, s)],
                                     sem_out.at[slot])

    @pl.when(c < nch)
    def _():
        @pl.when(c == 0)
        def _():
            yout[...] = jnp.zeros(yout.shape, yout.dtype)
            for before in range(RING_SLOTS, 1, -1):
                for s in range(CHUNK_BLOCKS):
                    out_copy(c - before, s).start()
            for first in range(RING_SLOTS - 1):
                for s in range(CHUNK_BLOCKS):
                    in_copy(c + first, s).start()

        for s in range(CHUNK_BLOCKS):
            in_copy(c + RING_SLOTS - 1, s).start()
        for s in range(CHUNK_BLOCKS):
            in_copy(c, s).wait()

        ab = _dot(xin[c % RING_SLOTS], wgu_ref[...])

        for s in range(CHUNK_BLOCKS):
            out_copy(c - 1, s).start()

        a = ab[:, 0:D_EXPERT]
        b = ab[:, D_EXPERT:2 * D_EXPERT]
        act = ((a * jax.nn.sigmoid(a)) * b).astype(BF16)

        for s in range(CHUNK_BLOCKS):
            out_copy(c - RING_SLOTS, s).wait()
        yout[c % RING_SLOTS] = _dot(act, wd_ref[...]).astype(BF16)

        @pl.when(c == nch - 1)
        def _():
            for s in range(CHUNK_BLOCKS):
                out_copy(c, s).start()
            for ahead in range(1, RING_SLOTS):
                for s in range(CHUNK_BLOCKS):
                    in_copy(c + ahead, s).wait()
            for before in range(RING_SLOTS - 1, -1, -1):
                for s in range(CHUNK_BLOCKS):
                    out_copy(c - before, s).wait()


def _experts(xs_blocks, n_used_blocks, table, chunk_e, n_chunks, w_gate_up, w_down):
    assert xs_blocks.shape[0] - n_used_blocks > RING_SLOTS * CHUNK_BLOCKS
    n_steps = chunk_e.shape[0]
    w_spec = lambda shape: pl.BlockSpec((None,) + shape, lambda c, tab, ce, nch: (ce[c], 0, 0))
    return pl.pallas_call(
        functools.partial(_expert_kernel, n_used_blocks),
        grid_spec=pltpu.PrefetchScalarGridSpec(
            num_scalar_prefetch=3,
            grid=(n_steps,),
            in_specs=[pl.BlockSpec(memory_space=pl.ANY),
                      w_spec((D_MODEL, 2 * D_EXPERT)), w_spec((D_EXPERT, D_MODEL))],
            out_specs=pl.BlockSpec(memory_space=pl.ANY),
            scratch_shapes=[pltpu.VMEM((RING_SLOTS, CHUNK_ROWS, D_MODEL), BF16),
                            pltpu.VMEM((RING_SLOTS, CHUNK_ROWS, D_MODEL), BF16),
                            pltpu.SemaphoreType.DMA((RING_SLOTS,)),
                            pltpu.SemaphoreType.DMA((RING_SLOTS,))]),
        out_shape=jax.ShapeDtypeStruct(xs_blocks.shape, xs_blocks.dtype),
        input_output_aliases={3: 0},
        compiler_params=pltpu.CompilerParams(dimension_semantics=("arbitrary",),
                                             vmem_limit_bytes=VMEM_LIMIT),
        name="experts",
    )(table, chunk_e, n_chunks, xs_blocks, w_gate_up, w_down)


def _combine_kernel(tiles, y_ref, route_ref, x1_ref, gfin_ref, o_ref):
    for k in range(tiles):
        rs = slice(k * TOK_TILE, (k + 1) * TOK_TILE)
        route = route_ref[rs, :]
        pos1 = route[:, ROUTE_POS1:ROUTE_POS1 + 1].astype(I32)
        pos2 = route[:, ROUTE_POS2:ROUTE_POS2 + 1].astype(I32)
        w1 = route[:, ROUTE_W1:ROUTE_W1 + 1]
        w2 = route[:, ROUTE_W2:ROUTE_W2 + 1]
        r_idx = lax.broadcasted_iota(I32, (TOK_TILE, TILE_ROWS), 1)
        gather = jnp.where(r_idx == pos1, w1, jnp.where(r_idx == pos2, w2, 0.0)).astype(BF16)
        y = y_ref[k * TILE_BLOCKS:(k + 1) * TILE_BLOCKS].reshape(TILE_ROWS, D_MODEL)
        o_ref[rs, :] = _rmsnorm(x1_ref[rs, :] + _dot(gather, y), gfin_ref[...])


def _combine(y_blocks, route, x1, g_final, tile0, n_tiles, tiles_per_step):
    assert n_tiles % tiles_per_step == 0 and tile0 % tiles_per_step == 0
    step0 = tile0 // tiles_per_step
    tok_spec = lambda width: pl.BlockSpec((tiles_per_step * TOK_TILE, width), lambda i: (step0 + i, 0))
    return pl.pallas_call(
        functools.partial(_combine_kernel, tiles_per_step),
        grid=(n_tiles // tiles_per_step,),
        in_specs=[pl.BlockSpec((tiles_per_step * TILE_BLOCKS, BLK, D_MODEL), lambda i: (step0 + i, 0, 0)),
                  tok_spec(ROUTER_COLS), tok_spec(D_MODEL),
                  pl.BlockSpec((1, D_MODEL), lambda i: (0, 0))],
        out_specs=pl.BlockSpec((tiles_per_step * TOK_TILE, D_MODEL), lambda i: (i, 0)),
        out_shape=jax.ShapeDtypeStruct((n_tiles * TOK_TILE, D_MODEL), F32),
        compiler_params=pltpu.CompilerParams(dimension_semantics=("arbitrary",),
                                             vmem_limit_bytes=VMEM_LIMIT),
        name="combine",
    )(y_blocks, route, x1, g_final)


def kernel(x_prompt, x_sample, state_pool, state_conv, g_mix, w_in, w_pool_grp, pool_scale, w_pool_out, w_dw, b_dw, ln_g, ln_b, w_conv_out, w_out, g_ffn, w_rg, b_rg, w_re, b_re, w_gate, w_up, w_down, g_final):
    assert g_mix.shape[0] == 1
    l = 0
    bp, seq_p, _ = x_prompt.shape
    bs, seq_s, _ = x_sample.shape
    tiles_p = bp * seq_p // TOK_TILE
    tiles_s = bs * seq_s // TOK_TILE
    n_tiles = tiles_p + tiles_s

    n_pad = ROUTER_COLS - N_EXPERTS - N_EXPERT_GROUPS
    w_r = jnp.concatenate([w_re[l], w_rg[l], jnp.zeros((D_MODEL, n_pad), F32)], axis=1)
    b_r = jnp.concatenate([b_re[l], b_rg[l], jnp.zeros((n_pad,), F32)]).reshape(1, ROUTER_COLS)
    w_r_hi = w_r.astype(BF16)
    w_r_cat = jnp.concatenate([w_r_hi, (w_r - w_r_hi.astype(F32)).astype(BF16)], axis=1)
    row = lambda a: a.reshape(1, -1)
    wts = (row(g_mix[l]), w_in[l].astype(BF16), w_pool_grp[l].astype(BF16), row(pool_scale[l]),
           w_pool_out[l].astype(BF16), w_dw[l], row(b_dw[l]), row(ln_g[l]), row(ln_b[l]),
           w_conv_out[l].astype(BF16), w_out[l].astype(BF16), row(g_ffn[l]), w_r_cat, b_r)

    zp = jnp.zeros((bp, POOL_HIST, D_POOL), F32)
    zc = jnp.zeros((bp, CONV_HIST, D_CONV), F32)
    n_all = n_tiles + -(-(RING_SLOTS * CHUNK_BLOCKS + 1) // TILE_BLOCKS)
    expert_w = (w_gate[l].reshape(N_EXPERTS * D_MODEL, D_EXPERT), w_up[l].reshape(N_EXPERTS * D_MODEL, D_EXPERT),
                w_down[l].reshape(N_EXPERTS * D_EXPERT, D_MODEL))
    *shared, npp, ncp, w_gate_up, w_down_bf = _mixer(x_prompt, zp, zc, 0, 1, TOK_TILE, wts, n_all, 0, None, expert_w)
    x1, route, xs, nblk, nps, ncs = _mixer(x_sample, state_pool[l], state_conv[l], PAST_LEN, bs, seq_s, wts,
                                           n_all, tiles_p, shared)

    n_chunks_max = -(-n_tiles * TILE_BLOCKS // CHUNK_BLOCKS) + N_EXPERTS
    table, chunk_e, n_chunks = _chunk_tables(nblk[:n_tiles, 0, :N_EXPERTS], n_chunks_max)
    y_blocks = _experts(xs.reshape(n_all * TILE_BLOCKS, BLK, D_MODEL), n_tiles * TILE_BLOCKS,
                        table, chunk_e, n_chunks,
                        w_gate_up.reshape(N_EXPERTS, D_MODEL, 2 * D_EXPERT),
                        w_down_bf.reshape(N_EXPERTS, D_EXPERT, D_MODEL))
    tps = COMBINE_TILES if tiles_p % COMBINE_TILES == 0 else 1
    yp = _combine(y_blocks, route, x1, row(g_final), 0, tiles_p, tps)
    ys = _combine(y_blocks, route, x1, row(g_final), tiles_p, tiles_s, 1)
    return (yp.reshape(bp, seq_p, D_MODEL), ys.reshape(bs, seq_s, D_MODEL),
            npp[None], ncp[None], nps[None], ncs[None])
```

```python
import functools

import jax
import jax.numpy as jnp
from jax import lax
from jax.experimental import pallas as pl
from jax.experimental.pallas import tpu as pltpu

D_MODEL = 1024
D_POOL = 512
N_POOL_GROUPS = 4
POOL_GROUP = D_POOL // N_POOL_GROUPS
POOL_WINDOWS = (2, 4, 8, 16)
POOL_HIST = max(POOL_WINDOWS) - 1
D_CONV = 512
CONV_WIDTH = 31
CONV_HIST = CONV_WIDTH - 1
N_EXPERT_GROUPS = 4
EXPERTS_PER_GROUP = 8
N_EXPERTS = N_EXPERT_GROUPS * EXPERTS_PER_GROUP
TOP_K = 2
D_EXPERT = 256
RMS_EPS = 1e-6
LN_EPS = 1e-5
PAST_LEN = 1024

LANES = 128
SUBLANES = 8
POOL_PAD = 16
CONV_PAD = 32
CONV_CHUNK = 64
ROUTER_COLS = LANES
VMEM_LIMIT = 56 * 1024 * 1024

TOK_TILE = 512
BLK = 2 * SUBLANES
TILE_BLOCKS = (TOP_K * TOK_TILE + N_EXPERTS * (BLK - 1)) // BLK
TILE_ROWS = TILE_BLOCKS * BLK
CHUNK_BLOCKS = 32
CHUNK_ROWS = CHUNK_BLOCKS * BLK
RING_SLOTS = 3
ROUTE_POS1, ROUTE_POS2, ROUTE_W1, ROUTE_W2 = 0, 1, 2, 3
COMBINE_TILES = 2

BF16 = jnp.bfloat16
F32 = jnp.float32
I32 = jnp.int32


def _dot(a, b):
    return jnp.dot(a, b, preferred_element_type=F32)


def _rmsnorm(x, g):
    return x * lax.rsqrt(jnp.mean(x * x, axis=-1, keepdims=True) + RMS_EPS) * g


def _routing(logits):
    lane = lax.broadcasted_iota(I32, logits.shape, 1)
    lane_f = lane.astype(F32)
    neg = jnp.float32(-jnp.inf)
    big = jnp.float32(1e9)
    is_grp = (lane >= N_EXPERTS) & (lane < N_EXPERTS + N_EXPERT_GROUPS)
    glog = jnp.where(is_grp, logits, neg)
    gmax = jnp.max(glog, axis=1, keepdims=True)
    gidx = jnp.min(jnp.where(glog == gmax, lane_f, big), axis=1, keepdims=True) - float(N_EXPERTS)
    gsum = jnp.sum(jnp.where(is_grp, jnp.exp(glog - gmax), 0.0), axis=1, keepdims=True)
    p_sel = 1.0 / gsum
    lane_grp = (lane >> 3).astype(F32)
    is_sel = (lane < N_EXPERTS) & (lane_grp == gidx)
    elog = jnp.where(is_sel, logits, neg)
    v1 = jnp.max(elog, axis=1, keepdims=True)
    i1 = jnp.min(jnp.where(elog == v1, lane_f, big), axis=1, keepdims=True)
    elog2 = jnp.where(lane_f == i1, neg, elog)
    v2 = jnp.max(elog2, axis=1, keepdims=True)
    i2 = jnp.min(jnp.where(elog2 == v2, lane_f, big), axis=1, keepdims=True)
    t = jnp.exp(v2 - v1)
    return i1, i2, p_sel / (1.0 + t), p_sel * (t / (1.0 + t))


def _dispatch(h2, i1, i2, w1, w2):
    rows = h2.shape[0]
    lane = lax.broadcasted_iota(I32, (rows, LANES), 1)
    lane_f = lane.astype(F32)
    hit1 = lane_f == i1
    hit2 = lane_f == i2
    sel = jnp.where(hit1 | hit2, 1.0, 0.0)
    earlier = lax.broadcasted_iota(I32, (rows, rows), 1) < lax.broadcasted_iota(I32, (rows, rows), 0)
    rank = _dot(jnp.where(earlier, 1.0, 0.0).astype(BF16), sel.astype(BF16))
    cnt = jnp.sum(sel, axis=0, keepdims=True)
    nblk = jnp.floor((cnt + float(BLK - 1)) * (1.0 / BLK))
    before = lax.broadcasted_iota(I32, (LANES, LANES), 0) < lax.broadcasted_iota(I32, (LANES, LANES), 1)
    bstart = _dot(jnp.broadcast_to(nblk, (SUBLANES, LANES)).astype(BF16),
                  jnp.where(before, 1.0, 0.0).astype(BF16))[0:1, :]
    pos = float(BLK) * bstart + rank
    pos1 = jnp.sum(jnp.where(hit1, pos, 0.0), axis=1, keepdims=True)
    pos2 = jnp.sum(jnp.where(hit2, pos, 0.0), axis=1, keepdims=True)
    route = jnp.where(lane == ROUTE_POS1, pos1,
                      jnp.where(lane == ROUTE_POS2, pos2,
                                jnp.where(lane == ROUTE_W1, w1, jnp.where(lane == ROUTE_W2, w2, 0.0))))
    route_t = route.T
    p1 = route_t[ROUTE_POS1:ROUTE_POS1 + 1, :].astype(I32)
    p2 = route_t[ROUTE_POS2:ROUTE_POS2 + 1, :].astype(I32)
    r_idx = lax.broadcasted_iota(I32, (TILE_ROWS, rows), 0)
    onehot = jnp.where((r_idx == p1) | (r_idx == p2), 1.0, 0.0).astype(BF16)
    return route, _dot(onehot, h2).astype(BF16), nblk


N_MIXER_IN = 17
N_MIXER_OUT = 6
N_MIXER_SCRATCH = 5


def _mixer_kernel(past, sb, ls, nj, n_valid, n_steps, cast_experts, *refs):
    t = pl.program_id(0)
    if n_valid == n_steps:
        assert not cast_experts
        _mixer_tile(past, sb, ls, t % nj, nj, *refs)
        return
    outs = refs[-(N_MIXER_OUT + N_MIXER_SCRATCH):][:N_MIXER_OUT]

    @pl.when(t < n_valid)
    def _():
        _mixer_tile(past, sb, ls, t % nj, nj, *refs)
        if cast_experts:
            wg_ref, wu_ref, wd_ref = refs[N_MIXER_IN:N_MIXER_IN + 3]
            wgu_out, wd_out = refs[-(N_MIXER_OUT + N_MIXER_SCRATCH + 2):-(N_MIXER_OUT + N_MIXER_SCRATCH)]
            wgu_out[:, 0:D_EXPERT] = wg_ref[...].astype(BF16)
            wgu_out[:, D_EXPERT:2 * D_EXPERT] = wu_ref[...].astype(BF16)
            wd_out[...] = wd_ref[...].astype(BF16)

    @pl.when(t >= n_valid)
    def _():
        for ref in outs[:4]:
            ref[...] = jnp.zeros(ref.shape, ref.dtype)


def _mixer_tile(past, sb, ls, j, nj, *refs):
    (x_ref, hp_ref, hc_ref, gmix_ref, win_ref, wgrp_ref, pscale_ref, wpo_ref, wdw_ref, bdw_ref,
     lng_ref, lnb_ref, wco_ref, wout_ref, gffn_ref, wrc_ref, br_ref) = refs[:N_MIXER_IN]
    x1_ref, route_ref, xs_ref, nblk_ref, npool_ref, nconv_ref = refs[-(N_MIXER_OUT + N_MIXER_SCRATCH):][:N_MIXER_OUT]
    ubuf, vbuf, sbuf, cbuf, ybuf = refs[-N_MIXER_SCRATCH:]
    rows = sb * ls

    @pl.when(j == 0)
    def _():
        ubuf[:, POOL_PAD - POOL_HIST:POOL_PAD, :] = hp_ref[...]
        vbuf[:, CONV_PAD - CONV_HIST:CONV_PAD, :] = hc_ref[...]

    x = x_ref[...].reshape(rows, D_MODEL)
    h = _rmsnorm(x, gmix_ref[...]).astype(BF16)
    u = _dot(h, win_ref[:, 0:D_POOL])
    ubuf[:, POOL_PAD:POOL_PAD + ls, :] = u.reshape(sb, ls, D_POOL)
    ga = _dot(h, win_ref[:, D_POOL:D_POOL + D_CONV])
    gb = _dot(h, win_ref[:, D_POOL + D_CONV:D_POOL + 2 * D_CONV])
    vbuf[:, CONV_PAD:CONV_PAD + ls, :] = (ga * jax.nn.sigmoid(gb)).reshape(sb, ls, D_CONV)
    c_gp = D_POOL + 2 * D_CONV
    gate_pool = _dot(h, win_ref[:, c_gp:c_gp + D_MODEL])
    gate_conv = _dot(h, win_ref[:, c_gp + D_MODEL:c_gp + 2 * D_MODEL])

    t_idx = lax.broadcasted_iota(I32, (sb, ls, 1), 1)
    frames = (past + 1 + j * ls + t_idx).astype(F32)
    ys = []
    part, width = ubuf[...], 1
    for gi, w in enumerate(POOL_WINDOWS):
        c0, c1 = gi * POOL_GROUP, (gi + 1) * POOL_GROUP
        cur = ubuf[:, POOL_PAD:POOL_PAD + ls, c0:c1]
        while width < w:
            part = part + pltpu.roll(part, width, 1)
            width *= 2
        s = part[:, POOL_PAD:POOL_PAD + ls, 0:POOL_GROUP]
        part = part[:, :, POOL_GROUP:]
        mean = s / jnp.minimum(jnp.float32(w), frames)
        ys.append(_dot((mean - cur).reshape(rows, POOL_GROUP).astype(BF16), wgrp_ref[gi]))
    yp = (jnp.concatenate(ys, axis=1) * pscale_ref[...]).astype(BF16)
    y_pool = _dot(yp, wpo_ref[...])
    new_pool = ubuf[:, POOL_PAD + ls - POOL_HIST:POOL_PAD + ls, :]
    ubuf[:, POOL_PAD - POOL_HIST:POOL_PAD, :] = new_pool

    first = CONV_PAD - CONV_HIST
    taps = [[o for o in range(first, first + CONV_WIDTH) if o % SUBLANES == r] for r in range(SUBLANES)]
    for r in range(1, SUBLANES):
        span = taps[r][-1] - taps[r][0] + ls
        sbuf[r - 1, :, 0:span, :] = vbuf[:, taps[r][0]:taps[r][0] + span, :]
    nb_c, nt_c = (1, CONV_CHUNK) if ls >= CONV_CHUNK else (CONV_CHUNK // ls, ls)
    n_grp = nt_c // SUBLANES
    for i in range(rows // CONV_CHUNK):
        b0, t0 = (0, i * nt_c) if nb_c == 1 else (i * nb_c, 0)
        for c0 in range(0, D_CONV, LANES):
            cs = slice(c0, c0 + LANES)
            accs = [jnp.zeros((nb_c, SUBLANES, LANES), F32) + bdw_ref[:, cs] for _ in range(n_grp)]
            for r in range(SUBLANES):
                ws = [wdw_ref[o - first:o - first + 1, cs] for o in taps[r]]
                for q in range(n_grp + len(taps[r]) - 1):
                    lo = t0 + q * SUBLANES
                    if r == 0:
                        val = vbuf[b0:b0 + nb_c, taps[0][0] + lo:taps[0][0] + lo + SUBLANES, cs]
                    else:
                        val = sbuf[r - 1, b0:b0 + nb_c, lo:lo + SUBLANES, cs]
                    for a in range(len(taps[r])):
                        if 0 <= q - a < n_grp:
                            accs[q - a] = accs[q - a] + val * ws[a]
            cbuf[:, cs] = jnp.concatenate(accs, axis=1).reshape(CONV_CHUNK, LANES)
        yf = cbuf[...]
        mu = jnp.mean(yf, axis=-1, keepdims=True)
        yc = yf - mu
        var = jnp.mean(yc * yc, axis=-1, keepdims=True)
        yn = yc * lax.rsqrt(var + LN_EPS) * lng_ref[...] + lnb_ref[...]
        ybuf[i * CONV_CHUNK:(i + 1) * CONV_CHUNK, :] = (yn * jax.nn.sigmoid(yn)).astype(BF16)
    y_conv = _dot(ybuf[...], wco_ref[...])
    new_conv = vbuf[:, CONV_PAD + ls - CONV_HIST:CONV_PAD + ls, :]
    vbuf[:, CONV_PAD - CONV_HIST:CONV_PAD, :] = new_conv

    @pl.when(j == nj - 1)
    def _():
        npool_ref[...] = new_pool
        nconv_ref[...] = new_conv

    x1 = x
    for c0 in range(0, D_MODEL, D_MODEL // 2):
        cs = slice(c0, c0 + D_MODEL // 2)
        merged = jax.nn.sigmoid(gate_pool[:, cs]) * y_pool[:, cs] + jax.nn.sigmoid(gate_conv[:, cs]) * y_conv[:, cs]
        x1 = x1 + _dot(merged.astype(BF16), wout_ref[cs, :])
    x1_ref[...] = x1

    h2 = _rmsnorm(x1, gffn_ref[...])
    h2_hi = h2.astype(BF16)
    h2_lo = (h2 - h2_hi.astype(F32)).astype(BF16)
    hi = _dot(h2_hi, wrc_ref[...])
    logits = (hi[:, 0:ROUTER_COLS] + (hi[:, ROUTER_COLS:] + _dot(h2_lo, wrc_ref[:, 0:ROUTER_COLS]))) + br_ref[...]
    route, xs, nblk = _dispatch(h2_hi, *_routing(logits))
    route_ref[...] = route
    xs_ref[...] = xs
    nblk_ref[...] = jnp.broadcast_to(nblk, (SUBLANES, LANES)).astype(I32)


def _const_spec(shape):
    nd = len(shape)
    return pl.BlockSpec(shape, lambda t: (0,) * nd, pipeline_mode=pl.Buffered(1))


def _mixer(x, hist_pool, hist_conv, past, sb, ls, wts, n_tiles, tile0, shared, expert_w=None):
    nb, seq, _ = x.shape
    assert nb % sb == 0 and seq % ls == 0 and sb * ls == TOK_TILE and ls % SUBLANES == 0 and ls >= CONV_HIST
    nj = seq // ls
    n_valid = (nb // sb) * nj
    n_steps = n_valid if shared is not None else n_tiles - tile0
    own = lambda t: jnp.minimum(t, n_valid - 1)
    hist_spec = lambda n, width: pl.BlockSpec((sb, n, width), lambda t: (own(t) // nj, 0, 0))
    in_specs = [pl.BlockSpec((sb, ls, D_MODEL), lambda t: (own(t) // nj, own(t) % nj, 0)),
                hist_spec(POOL_HIST, D_POOL), hist_spec(CONV_HIST, D_CONV)]
    in_specs += [_const_spec(w.shape) for w in wts]
    shared_shapes = (
        jax.ShapeDtypeStruct((n_tiles * TOK_TILE, D_MODEL), F32),
        jax.ShapeDtypeStruct((n_tiles * TOK_TILE, ROUTER_COLS), F32),
        jax.ShapeDtypeStruct((n_tiles, TILE_ROWS, D_MODEL), BF16),
        jax.ShapeDtypeStruct((n_tiles, SUBLANES, LANES), I32),
    )
    out_shape = shared_shapes + (jax.ShapeDtypeStruct((nb, POOL_HIST, D_POOL), F32),
                                 jax.ShapeDtypeStruct((nb, CONV_HIST, D_CONV), F32))
    out_specs = (pl.BlockSpec((TOK_TILE, D_MODEL), lambda t: (tile0 + t, 0)),
                 pl.BlockSpec((TOK_TILE, ROUTER_COLS), lambda t: (tile0 + t, 0)),
                 pl.BlockSpec((None, TILE_ROWS, D_MODEL), lambda t: (tile0 + t, 0, 0)),
                 pl.BlockSpec((None, SUBLANES, LANES), lambda t: (tile0 + t, 0, 0)),
                 hist_spec(POOL_HIST, D_POOL), hist_spec(CONV_HIST, D_CONV))
    args = [x, hist_pool, hist_conv, *wts]
    if expert_w is not None:
        assert shared is None
        for w in expert_w:
            assert w.shape[0] % (n_valid * BLK) == 0
            in_specs.append(pl.BlockSpec((w.shape[0] // n_valid, w.shape[1]), lambda t: (own(t), 0)))
        wg, wu, wd = expert_w
        cast_shapes = (jax.ShapeDtypeStruct((wg.shape[0], wg.shape[1] + wu.shape[1]), BF16),
                       jax.ShapeDtypeStruct(wd.shape, BF16))
        out_shape = cast_shapes + out_shape
        out_specs = tuple(pl.BlockSpec((s.shape[0] // n_valid, s.shape[1]), lambda t: (own(t), 0))
                          for s in cast_shapes) + out_specs
        args += list(expert_w)
    aliases = {}
    if shared is not None:
        in_specs += [pl.BlockSpec(memory_space=pl.ANY)] * len(shared)
        aliases = {len(args) + k: k for k in range(len(shared))}
        args += list(shared)
    rows = sb * ls
    outs = pl.pallas_call(
        functools.partial(_mixer_kernel, past, sb, ls, nj, n_valid, n_steps, expert_w is not None),
        grid=(n_steps,),
        in_specs=in_specs,
        out_specs=out_specs,
        out_shape=out_shape,
        input_output_aliases=aliases,
        scratch_shapes=[pltpu.VMEM((sb, POOL_PAD + ls, D_POOL), F32),
                        pltpu.VMEM((sb, CONV_PAD + ls, D_CONV), F32),
                        pltpu.VMEM((SUBLANES - 1, sb, CONV_PAD + ls, D_CONV), F32),
                        pltpu.VMEM((CONV_CHUNK, D_CONV), F32),
                        pltpu.VMEM((rows, D_CONV), BF16)],
        compiler_params=pltpu.CompilerParams(dimension_semantics=("arbitrary",),
                                             vmem_limit_bytes=VMEM_LIMIT),
        name="mixer",
    )(*args)
    return outs if expert_w is None else (*outs[2:], *outs[:2])


def _chunk_tables(nblk, n_chunks_max):
    n_tiles = nblk.shape[0]
    e_ids = jnp.arange(N_EXPERTS, dtype=I32)
    per_e = jnp.sum(nblk, axis=0)
    chunks_e = -(-per_e // CHUNK_BLOCKS)
    cend = jnp.cumsum(chunks_e)
    cstart = cend - chunks_e
    n_chunks = cend[-1]
    c = jnp.arange(n_chunks_max, dtype=I32)
    chunk_e = jnp.minimum(jnp.sum((cend[None, :] <= c[:, None]).astype(I32), axis=1), N_EXPERTS - 1)
    is_e = chunk_e[:, None] == e_ids[None, :]
    of_chunk = lambda v: jnp.sum(jnp.where(is_e, v[None, :], 0), axis=1)
    of_chunk_t = lambda a: jnp.sum(jnp.where(is_e[:, None, :], a[None, :, :], 0), axis=2)
    cum_t = jnp.cumsum(nblk, axis=0)
    bstart = jnp.cumsum(nblk, axis=1) - nblk
    k = ((c - of_chunk(cstart)) * CHUNK_BLOCKS)[:, None] + jnp.arange(CHUNK_BLOCKS, dtype=I32)[None, :]
    tile = jnp.sum((of_chunk_t(cum_t)[:, None, :] <= k[:, :, None]).astype(I32), axis=2)
    is_t = tile[:, :, None] == jnp.arange(n_tiles, dtype=I32)[None, None, :]
    shift = of_chunk_t(bstart - (cum_t - nblk))
    blk = k + jnp.sum(jnp.where(is_t, shift[:, None, :], 0), axis=2)
    valid = (k < of_chunk(per_e)[:, None]) & (c[:, None] < n_chunks)
    table = jnp.where(valid, tile * TILE_BLOCKS + blk, -1)
    return table.reshape(-1).astype(I32), chunk_e.astype(I32), n_chunks.reshape(1).astype(I32)


def _expert_kernel(spare0, tab_ref, ce_ref, nch_ref, x_hbm, wgu_ref, wd_ref, y_hbm,
                   xin, yout, sem_in, sem_out):
    c = pl.program_id(0)
    nc = pl.num_programs(0)
    nch = nch_ref[0]

    zero_block = spare0 + RING_SLOTS * CHUNK_BLOCKS

    def src_block(chunk, s):
        t = tab_ref[jnp.minimum(chunk, nc - 1) * CHUNK_BLOCKS + s]
        return jnp.where((t < 0) | (chunk >= nch), zero_block, t)

    def dst_block(chunk, s):
        t = tab_ref[jnp.maximum(chunk, 0) * CHUNK_BLOCKS + s]
        return jnp.where((t < 0) | (chunk < 0), spare0 + (chunk % RING_SLOTS) * CHUNK_BLOCKS + s, t)

    def in_copy(chunk, s):
        slot = chunk % RING_SLOTS
        return pltpu.make_async_copy(x_hbm.at[src_block(chunk, s)], xin.at[slot, pl.ds(s * BLK, BLK), :],
                                     sem_in.at[slot])

    def out_copy(chunk, s):
        slot = chunk % RING_SLOTS
        return pltpu.make_async_copy(yout.at[slot, pl.ds(s * BLK, BLK), :], y_hbm.at[dst_block(chunk, s)],
                                     sem_out.at[slot])

    @pl.when(c < nch)
    def _():
        @pl.when(c == 0)
        def _():
            yout[...] = jnp.zeros(yout.shape, yout.dtype)
            for before in range(RING_SLOTS, 1, -1):
                for s in range(CHUNK_BLOCKS):
                    out_copy(c - before, s).start()
            for first in range(RING_SLOTS - 1):
                for s in range(CHUNK_BLOCKS):
                    in_copy(c + first, s).start()

        for s in range(CHUNK_BLOCKS):
            in_copy(c + RING_SLOTS - 1, s).start()
        for s in range(CHUNK_BLOCKS):
            in_copy(c, s).wait()

        ab = _dot(xin[c % RING_SLOTS], wgu_ref[...])

        for s in range(CHUNK_BLOCKS):
            out_copy(c - 1, s).start()

        a = ab[:, 0:D_EXPERT]
        b = ab[:, D_EXPERT:2 * D_EXPERT]
        act = ((a * jax.nn.sigmoid(a)) * b).astype(BF16)

        for s in range(CHUNK_BLOCKS):
            out_copy(c - RING_SLOTS, s).wait()
        yout[c % RING_SLOTS] = _dot(act, wd_ref[...]).astype(BF16)

        @pl.when(c == nch - 1)
        def _():
            for s in range(CHUNK_BLOCKS):
                out_copy(c, s).start()
            for ahead in range(1, RING_SLOTS):
                for s in range(CHUNK_BLOCKS):
                    in_copy(c + ahead, s).wait()
            for before in range(RING_SLOTS - 1, -1, -1):
                for s in range(CHUNK_BLOCKS):
                    out_copy(c - before, s).wait()


def _experts(xs_blocks, n_used_blocks, table, chunk_e, n_chunks, w_gate_up, w_down):
    assert xs_blocks.shape[0] - n_used_blocks > RING_SLOTS * CHUNK_BLOCKS
    n_steps = chunk_e.shape[0]
    w_spec = lambda shape: pl.BlockSpec((None,) + shape, lambda c, tab, ce, nch: (ce[c], 0, 0))
    return pl.pallas_call(
        functools.partial(_expert_kernel, n_used_blocks),
        grid_spec=pltpu.PrefetchScalarGridSpec(
            num_scalar_prefetch=3,
            grid=(n_steps,),
            in_specs=[pl.BlockSpec(memory_space=pl.ANY),
                      w_spec((D_MODEL, 2 * D_EXPERT)), w_spec((D_EXPERT, D_MODEL))],
            out_specs=pl.BlockSpec(memory_space=pl.ANY),
            scratch_shapes=[pltpu.VMEM((RING_SLOTS, CHUNK_ROWS, D_MODEL), BF16),
                            pltpu.VMEM((RING_SLOTS, CHUNK_ROWS, D_MODEL), BF16),
                            pltpu.SemaphoreType.DMA((RING_SLOTS,)),
                            pltpu.SemaphoreType.DMA((RING_SLOTS,))]),
        out_shape=jax.ShapeDtypeStruct(xs_blocks.shape, xs_blocks.dtype),
        input_output_aliases={3: 0},
        compiler_params=pltpu.CompilerParams(dimension_semantics=("arbitrary",),
                                             vmem_limit_bytes=VMEM_LIMIT),
        name="experts",
    )(table, chunk_e, n_chunks, xs_blocks, w_gate_up, w_down)


def _combine_kernel(tiles, y_ref, route_ref, x1_ref, gfin_ref, o_ref):
    for k in range(tiles):
        rs = slice(k * TOK_TILE, (k + 1) * TOK_TILE)
        route = route_ref[rs, :]
        pos1 = route[:, ROUTE_POS1:ROUTE_POS1 + 1].astype(I32)
        pos2 = route[:, ROUTE_POS2:ROUTE_POS2 + 1].astype(I32)
        w1 = route[:, ROUTE_W1:ROUTE_W1 + 1]
        w2 = route[:, ROUTE_W2:ROUTE_W2 + 1]
        r_idx = lax.broadcasted_iota(I32, (TOK_TILE, TILE_ROWS), 1)
        gather = jnp.where(r_idx == pos1, w1, jnp.where(r_idx == pos2, w2, 0.0)).astype(BF16)
        y = y_ref[k * TILE_BLOCKS:(k + 1) * TILE_BLOCKS].reshape(TILE_ROWS, D_MODEL)
        o_ref[rs, :] = _rmsnorm(x1_ref[rs, :] + _dot(gather, y), gfin_ref[...])


def _combine(y_blocks, route, x1, g_final, tile0, n_tiles, tiles_per_step):
    assert n_tiles % tiles_per_step == 0 and tile0 % tiles_per_step == 0
    step0 = tile0 // tiles_per_step
    tok_spec = lambda width: pl.BlockSpec((tiles_per_step * TOK_TILE, width), lambda i: (step0 + i, 0))
    return pl.pallas_call(
        functools.partial(_combine_kernel, tiles_per_step),
        grid=(n_tiles // tiles_per_step,),
        in_specs=[pl.BlockSpec((tiles_per_step * TILE_BLOCKS, BLK, D_MODEL), lambda i: (step0 + i, 0, 0)),
                  tok_spec(ROUTER_COLS), tok_spec(D_MODEL),
                  pl.BlockSpec((1, D_MODEL), lambda i: (0, 0))],
        out_specs=pl.BlockSpec((tiles_per_step * TOK_TILE, D_MODEL), lambda i: (i, 0)),
        out_shape=jax.ShapeDtypeStruct((n_tiles * TOK_TILE, D_MODEL), F32),
        compiler_params=pltpu.CompilerParams(dimension_semantics=("arbitrary",),
                                             vmem_limit_bytes=VMEM_LIMIT),
        name="combine",
    )(y_blocks, route, x1, g_final)


def kernel(x_prompt, x_sample, state_pool, state_conv, g_mix, w_in, w_pool_grp, pool_scale, w_pool_out, w_dw, b_dw, ln_g, ln_b, w_conv_out, w_out, g_ffn, w_rg, b_rg, w_re, b_re, w_gate, w_up, w_down, g_final):
    assert g_mix.shape[0] == 1
    l = 0
    bp, seq_p, _ = x_prompt.shape
    bs, seq_s, _ = x_sample.shape
    tiles_p = bp * seq_p // TOK_TILE
    tiles_s = bs * seq_s // TOK_TILE
    n_tiles = tiles_p + tiles_s

    n_pad = ROUTER_COLS - N_EXPERTS - N_EXPERT_GROUPS
    w_r = jnp.concatenate([w_re[l], w_rg[l], jnp.zeros((D_MODEL, n_pad), F32)], axis=1)
    b_r = jnp.concatenate([b_re[l], b_rg[l], jnp.zeros((n_pad,), F32)]).reshape(1, ROUTER_COLS)
    w_r_hi = w_r.astype(BF16)
    w_r_cat = jnp.concatenate([w_r_hi, (w_r - w_r_hi.astype(F32)).astype(BF16)], axis=1)
    row = lambda a: a.reshape(1, -1)
    wts = (row(g_mix[l]), w_in[l].astype(BF16), w_pool_grp[l].astype(BF16), row(pool_scale[l]),
           w_pool_out[l].astype(BF16), w_dw[l], row(b_dw[l]), row(ln_g[l]), row(ln_b[l]),
           w_conv_out[l].astype(BF16), w_out[l].astype(BF16), row(g_ffn[l]), w_r_cat, b_r)

    zp = jnp.zeros((bp, POOL_HIST, D_POOL), F32)
    zc = jnp.zeros((bp, CONV_HIST, D_CONV), F32)
    n_all = n_tiles + -(-(RING_SLOTS * CHUNK_BLOCKS + 1) // TILE_BLOCKS)
    expert_w = (w_gate[l].reshape(N_EXPERTS * D_MODEL, D_EXPERT), w_up[l].reshape(N_EXPERTS * D_MODEL, D_EXPERT),
                w_down[l].reshape(N_EXPERTS * D_EXPERT, D_MODEL))
    *shared, npp, ncp, w_gate_up, w_down_bf = _mixer(x_prompt, zp, zc, 0, 1, TOK_TILE, wts, n_all, 0, None, expert_w)
    x1, route, xs, nblk, nps, ncs = _mixer(x_sample, state_pool[l], state_conv[l], PAST_LEN, bs, seq_s, wts,
                                           n_all, tiles_p, shared)

    n_chunks_max = -(-n_tiles * TILE_BLOCKS // CHUNK_BLOCKS) + N_EXPERTS
    table, chunk_e, n_chunks = _chunk_tables(nblk[:n_tiles, 0, :N_EXPERTS], n_chunks_max)
    y_blocks = _experts(xs.reshape(n_all * TILE_BLOCKS, BLK, D_MODEL), n_tiles * TILE_BLOCKS,
                        table, chunk_e, n_chunks,
                        w_gate_up.reshape(N_EXPERTS, D_MODEL, 2 * D_EXPERT),
                        w_down_bf.reshape(N_EXPERTS, D_EXPERT, D_MODEL))
    tps = COMBINE_TILES if tiles_p % COMBINE_TILES == 0 else 1
    yp = _combine(y_blocks, route, x1, row(g_final), 0, tiles_p, tps)
    ys = _combine(y_blocks, route, x1, row(g_final), tiles_p, tiles_s, 1)
    return (yp.reshape(bp, seq_p, D_MODEL), ys.reshape(bs, seq_s, D_MODEL),
            npp[None], ncp[None], nps[None], ncs[None])
```
